```python
import math
import jax
import jax.numpy as jnp
from jax import lax
import numpy as np

D_MODEL = 2048
BATCH = 2
SEQ = 4096
DEPTH = 1
DEC_BATCH = 16
DEC_SEQ = 64
PAST_LEN = 1024

CHUNK = 64
Q_BLOCK = 128
HEAD_DIM = 128
DIFF_WIDTH = D_MODEL // 2
SB_WIDTH = D_MODEL - DIFF_WIDTH
DIFF_V_DIM = 2 * HEAD_DIM
N_DIFF_HEADS = DIFF_WIDTH // DIFF_V_DIM
N_SB_HEADS = SB_WIDTH // HEAD_DIM
MIX_WIDTH = N_DIFF_HEADS * DIFF_V_DIM + N_SB_HEADS * HEAD_DIM
ROPE_DIM = HEAD_DIM // 4
ROPE_THETA = 500000.0
N_MEM = 256
N_MEM_HEADS = 4
MEM_HEAD_DIM = D_MODEL // N_MEM_HEADS
D_FF = 5632
CONV_W = 3
LN_EPS = 1e-5
DN_ALPHA = (2 * DEPTH) ** 0.25
DN_BETA = (8 * DEPTH) ** -0.25
_W_DQ = N_DIFF_HEADS * 2 * HEAD_DIM
_W_DV = N_DIFF_HEADS * DIFF_V_DIM
_W_S = N_SB_HEADS * HEAD_DIM
PROJ_SPLITS = (_W_DQ, 2 * _W_DQ, 2 * _W_DQ + _W_DV, 2 * _W_DQ + _W_DV + _W_S, 2 * _W_DQ + _W_DV + 2 * _W_S)
PROJ_WIDTH = 2 * _W_DQ + _W_DV + 3 * _W_S

kernel_name = 'hybrid_diffattn_stickbreak_stream_encoder'


def _layer_norm(x, g, b):
    xf = x.astype(jnp.float32)
    mu = jnp.mean(xf, axis=-1, keepdims=True)
    xc = xf - mu
    var = jnp.mean(xc * xc, axis=-1, keepdims=True)
    return (xc * lax.rsqrt(var + LN_EPS) * g + b).astype(x.dtype)


def _partial_rope(x, pos):
    half = ROPE_DIM // 2
    inv_freq = jnp.power(ROPE_THETA, -jnp.arange(half, dtype=jnp.float32) * (2.0 / ROPE_DIM))
    ang = pos.astype(jnp.float32)[:, None] * inv_freq[None, :]
    shape = (1, x.shape[1]) + (1,) * (x.ndim - 3) + (half,)
    cos = jnp.cos(ang).reshape(shape)
    sin = jnp.sin(ang).reshape(shape)
    xr = x[..., :ROPE_DIM].astype(jnp.float32)
    x1, x2 = xr[..., :half], xr[..., half:]
    rot = jnp.concatenate([x1 * cos - x2 * sin, x2 * cos + x1 * sin], axis=-1).astype(x.dtype)
    return jnp.concatenate([rot, x[..., ROPE_DIM:]], axis=-1)


def _mixer_project(h, pos, w_in):
    B, T, _ = h.shape
    dq, dk, dv, sq, sk, sv = jnp.split(h @ w_in, PROJ_SPLITS, axis=-1)
    dq = _partial_rope(dq.reshape(B, T, N_DIFF_HEADS, 2, HEAD_DIM), pos)
    dk = _partial_rope(dk.reshape(B, T, N_DIFF_HEADS, 2, HEAD_DIM), pos)
    dv = dv.reshape(B, T, N_DIFF_HEADS, DIFF_V_DIM)
    sq = sq.reshape(B, T, N_SB_HEADS, HEAD_DIM)
    sk = sk.reshape(B, T, N_SB_HEADS, HEAD_DIM)
    sv = sv.reshape(B, T, N_SB_HEADS, HEAD_DIM)
    return dq, dk, dv, sq, sk, sv


def _diff_attn(q, qpos, k, v, kpos, lam):
    s = jnp.einsum('bqhcd,bkhcd->bhcqk', q, k).astype(jnp.float32) * (HEAD_DIM ** -0.5)
    visible = (kpos[None, :] // CHUNK) <= (qpos[:, None] // CHUNK)
    p = jax.nn.softmax(jnp.where(visible, s, -jnp.inf), axis=-1)
    w = p[:, :, 0] - lam * p[:, :, 1]
    return jnp.einsum('bhqk,bkhd->bqhd', w, v.astype(jnp.float32)).astype(v.dtype)


def _sb_attn(q, qpos, k, v, kpos):
    z = jnp.einsum('bqhd,bkhd->bhqk', q, k).astype(jnp.float32) * (HEAD_DIM ** -0.5)
    before = kpos[None, :] < qpos[:, None]
    log_keep = jnp.where(before, jax.nn.log_sigmoid(-z), 0.0)
    later = lax.cumsum(log_keep, axis=3, reverse=True) - log_keep
    a = jnp.where(before, jnp.exp(jax.nn.log_sigmoid(z) + later), 0.0)
    return jnp.einsum('bhqk,bkhd->bqhd', a, v.astype(jnp.float32)).astype(v.dtype)


def _over_query_blocks(fn, q, qpos, *kv):
    B, T = q.shape[:2]
    nb = T // Q_BLOCK
    qb = jnp.swapaxes(q.reshape((B, nb, Q_BLOCK) + q.shape[2:]), 0, 1)
    pb = qpos.reshape(nb, Q_BLOCK)
    ob = lax.map(lambda args: fn(args[0], args[1], *kv), (qb, pb))
    return jnp.swapaxes(ob, 0, 1).reshape((B, T) + ob.shape[3:])


def _direct(fn, q, qpos, *kv):
    return fn(q, qpos, *kv)


def _mixer_out(o_diff, o_sb, subln_g, lam_init, w_o):
    B, T = o_diff.shape[:2]
    of = o_diff.astype(jnp.float32)
    of = of * lax.rsqrt(jnp.mean(of * of, axis=-1, keepdims=True) + LN_EPS) * subln_g * (1.0 - lam_init)
    mixed = jnp.concatenate([of.astype(o_sb.dtype).reshape(B, T, -1), o_sb.reshape(B, T, -1)], axis=-1)
    return mixed @ w_o


def _memory_kv(mem, w_mk, w_mv):
    B, M, _ = mem.shape
    return ((mem @ w_mk).reshape(B, M, N_MEM_HEADS, MEM_HEAD_DIM),
            (mem @ w_mv).reshape(B, M, N_MEM_HEADS, MEM_HEAD_DIM))


def _cross_attn(h, mem_k, mem_v, w_mq, w_mo):
    B, T, _ = h.shape
    q = (h @ w_mq).reshape(B, T, N_MEM_HEADS, MEM_HEAD_DIM)
    s = jnp.einsum('bqhd,bmhd->bhqm', q, mem_k).astype(jnp.float32) * (MEM_HEAD_DIM ** -0.5)
    p = jax.nn.softmax(s, axis=-1)
    o = jnp.einsum('bhqm,bmhd->bqhd', p, mem_v.astype(jnp.float32)).astype(h.dtype)
    return o.reshape(B, T, -1) @ w_mo


def _conv_ffn(h, conv_state, w_gate, w_up, w_conv, b_conv, w_down):
    T = h.shape[1]
    g = jnp.concatenate([conv_state, h @ w_gate], axis=1)
    c = b_conv + g[:, 0:T] * w_conv[0]
    for i in range(1, CONV_W):
        c = c + g[:, i:i + T] * w_conv[i]
    a = jax.nn.gelu(c) * (h @ w_up)
    return a @ w_down, g[:, g.shape[1] - (CONV_W - 1):]


def _layer(h, pos, kpos, past, mem_k, mem_v, conv_state, w, lam_init, blocked):
    dq, dk, dv, sq, sk, sv = _mixer_project(h, pos, w['w_in'])
    if past is None:
        kd, vd, ks, vs = dk, dv, sk, sv
    else:
        kd = jnp.concatenate([past[0], dk], axis=1)
        vd = jnp.concatenate([past[1], dv], axis=1)
        ks = jnp.concatenate([past[2], sk], axis=1)
        vs = jnp.concatenate([past[3], sv], axis=1)
    f32 = jnp.float32
    lam = (jnp.exp(jnp.sum(w['lq1'].astype(f32) * w['lk1'].astype(f32)))
           - jnp.exp(jnp.sum(w['lq2'].astype(f32) * w['lk2'].astype(f32))) + lam_init)
    attend = _over_query_blocks if blocked else _direct
    o_diff = attend(_diff_attn, dq, pos, kd, vd, kpos, lam)
    o_sb = attend(_sb_attn, sq, pos, ks, vs, kpos)
    mix = _mixer_out(o_diff, o_sb, w['subln_g'], lam_init, w['w_o'])
    h = _layer_norm(DN_ALPHA * h + mix, w['ln1_g'], w['ln1_b'])
    h = _layer_norm(DN_ALPHA * h + _cross_attn(h, mem_k, mem_v, w['w_mq'], w['w_mo']), w['ln2_g'], w['ln2_b'])
    f, new_conv = _conv_ffn(h, conv_state, w['w_gate'], w['w_up'], w['w_conv'], w['b_conv'], w['w_down'])
    h = _layer_norm(DN_ALPHA * h + f, w['ln3_g'], w['ln3_b'])
    return h, (dk, dv, sk, sv, new_conv)


def setup_inputs(seed: int = 0) -> dict:
    key = jax.random.key(seed)
    keys = iter(jax.random.split(key, 40))

    def nrm(shape, scale):
        return scale * jax.random.normal(next(keys), shape, jnp.float32)

    L = DEPTH
    return {
        'x_prompt': nrm((BATCH, SEQ, D_MODEL), 1.0),
        'x_sample': nrm((DEC_BATCH, DEC_SEQ, D_MODEL), 1.0),
        'mem_prompt': nrm((BATCH, N_MEM, D_MODEL), 1.0),
        'cache_diff_k': nrm((L, DEC_BATCH, PAST_LEN, N_DIFF_HEADS, 2, HEAD_DIM), 1.0),
        'cache_diff_v': nrm((L, DEC_BATCH, PAST_LEN, N_DIFF_HEADS, DIFF_V_DIM), 1.0),
        'cache_sb_k': nrm((L, DEC_BATCH, PAST_LEN, N_SB_HEADS, HEAD_DIM), 1.0),
        'cache_sb_v': nrm((L, DEC_BATCH, PAST_LEN, N_SB_HEADS, HEAD_DIM), 1.0),
        'cache_mem_k': nrm((L, DEC_BATCH, N_MEM, N_MEM_HEADS, MEM_HEAD_DIM), 1.0),
        'cache_mem_v': nrm((L, DEC_BATCH, N_MEM, N_MEM_HEADS, MEM_HEAD_DIM), 1.0),
        'state_ffn_conv': nrm((L, DEC_BATCH, CONV_W - 1, D_FF), 1.0),
        'w_in': nrm((L, D_MODEL, PROJ_WIDTH), D_MODEL ** -0.5),
        'lambda_q1': nrm((L, HEAD_DIM), 0.1),
        'lambda_k1': nrm((L, HEAD_DIM), 0.1),
        'lambda_q2': nrm((L, HEAD_DIM), 0.1),
        'lambda_k2': nrm((L, HEAD_DIM), 0.1),
        'subln_g': 1.0 + nrm((L, DIFF_V_DIM), 0.02),
        'w_o': nrm((L, MIX_WIDTH, D_MODEL), DN_BETA * MIX_WIDTH ** -0.5),
        'ln1_g': 1.0 + nrm((L, D_MODEL), 0.02),
        'ln1_b': nrm((L, D_MODEL), 0.02),
        'w_mq': nrm((L, D_MODEL, D_MODEL), D_MODEL ** -0.5),
        'w_mk': nrm((L, D_MODEL, D_MODEL), D_MODEL ** -0.5),
        'w_mv': nrm((L, D_MODEL, D_MODEL), D_MODEL ** -0.5),
        'w_mo': nrm((L, D_MODEL, D_MODEL), DN_BETA * D_MODEL ** -0.5),
        'ln2_g': 1.0 + nrm((L, D_MODEL), 0.02),
        'ln2_b': nrm((L, D_MODEL), 0.02),
        'w_gate': nrm((L, D_MODEL, D_FF), D_MODEL ** -0.5),
        'w_up': nrm((L, D_MODEL, D_FF), D_MODEL ** -0.5),
        'w_conv': nrm((L, CONV_W, D_FF), CONV_W ** -0.5),
        'b_conv': nrm((L, D_FF), 0.02),
        'w_down': nrm((L, D_FF, D_MODEL), DN_BETA * D_FF ** -0.5),
        'ln3_g': 1.0 + nrm((L, D_MODEL), 0.02),
        'ln3_b': nrm((L, D_MODEL), 0.02),
    }


def reference(x_prompt, x_sample, mem_prompt, cache_diff_k, cache_diff_v, cache_sb_k, cache_sb_v,
              cache_mem_k, cache_mem_v, state_ffn_conv, w_in, lambda_q1, lambda_k1, lambda_q2, lambda_k2,
              subln_g, w_o, ln1_g, ln1_b, w_mq, w_mk, w_mv, w_mo, ln2_g, ln2_b,
              w_gate, w_up, w_conv, b_conv, w_down, ln3_g, ln3_b):
    t_p = x_prompt.shape[1]
    past_len = cache_diff_k.shape[2]
    t_s = x_sample.shape[1]
    pos_p = jnp.arange(t_p)
    pos_s = past_len + jnp.arange(t_s)
    kpos_s = jnp.arange(past_len + t_s)
    hp, hs = x_prompt, x_sample
    st_p = [[] for _ in range(7)]
    st_s = [[] for _ in range(5)]
    for l in range(DEPTH):
        lam_init = 0.8 - 0.6 * math.exp(-0.3 * l)
        w = {'w_in': w_in[l], 'lq1': lambda_q1[l], 'lk1': lambda_k1[l], 'lq2': lambda_q2[l],
             'lk2': lambda_k2[l], 'subln_g': subln_g[l], 'w_o': w_o[l], 'ln1_g': ln1_g[l], 'ln1_b': ln1_b[l],
             'w_mq': w_mq[l], 'w_mo': w_mo[l], 'ln2_g': ln2_g[l], 'ln2_b': ln2_b[l],
             'w_gate': w_gate[l], 'w_up': w_up[l], 'w_conv': w_conv[l], 'b_conv': b_conv[l],
             'w_down': w_down[l], 'ln3_g': ln3_g[l], 'ln3_b': ln3_b[l]}
        mk_p, mv_p = _memory_kv(mem_prompt, w_mk[l], w_mv[l])
        zero_conv = jnp.zeros((hp.shape[0], CONV_W - 1, D_FF), hp.dtype)
        hp, new_p = _layer(hp, pos_p, pos_p, None, mk_p, mv_p, zero_conv, w, lam_init, True)
        past = (cache_diff_k[l], cache_diff_v[l], cache_sb_k[l], cache_sb_v[l])
        hs, new_s = _layer(hs, pos_s, kpos_s, past, cache_mem_k[l], cache_mem_v[l], state_ffn_conv[l],
                           w, lam_init, False)
        for lst, arr in zip(st_p, (new_p[0], new_p[1], new_p[2], new_p[3], mk_p, mv_p, new_p[4])):
            lst.append(arr)
        for lst, arr in zip(st_s, new_s):
            lst.append(arr)
    p = [jnp.stack(a, axis=0) for a in st_p]
    s = [jnp.stack(a, axis=0) for a in st_s]
    return (hp, hs, p[0], p[1], p[2], p[3], p[4], p[5], p[6], s[0], s[1], s[2], s[3], s[4])
```

```python
import functools
import math

import jax
import jax.numpy as jnp
from jax import lax
from jax.experimental import pallas as pl
from jax.experimental.pallas import tpu as pltpu

BF = jnp.bfloat16
F32 = jnp.float32

D_MODEL = 2048
CHUNK = 64
HEAD_DIM = 128
N_DIFF_HEADS = 4
N_SB_HEADS = 8
DIFF_V_DIM = 2 * HEAD_DIM
ROPE_DIM = HEAD_DIM // 4
ROPE_THETA = 500000.0
N_MEM_HEADS = 4
MEM_HEAD_DIM = D_MODEL // N_MEM_HEADS
D_FF = 5632
CONV_W = 3
LN_EPS = 1e-5
DEPTH = 1
DN_ALPHA = (2 * DEPTH) ** 0.25
SEG = 1024
N_SEG = 6
VMEM_LIMIT = 56 * 1024 * 1024

_NT = (((1,), (1,)), ((), ()))


def _cparams(n_axes):
    return pltpu.CompilerParams(dimension_semantics=("arbitrary",) * n_axes,
                                vmem_limit_bytes=VMEM_LIMIT)


def _layer_norm(y, g, b):
    mu = jnp.mean(y, axis=-1, keepdims=True)
    yc = y - mu
    var = jnp.mean(yc * yc, axis=-1, keepdims=True)
    return yc * lax.rsqrt(var + LN_EPS) * g + b


def _inproj_kernel(x_ref, w_ref, c_ref, s1_ref, s2_ref,
                   pb_ref, dk_ref, dv_ref, sk_ref, sv_ref, xb_ref):
    j = pl.program_id(1)

    @pl.when(j == 0)
    def _():
        xb_ref[...] = x_ref[...].astype(BF)

    acc = jnp.dot(xb_ref[...], w_ref[...], preferred_element_type=F32)

    def rope_chunk(ch):
        xc = acc[:, ch * HEAD_DIM:(ch + 1) * HEAD_DIM]
        return (xc * c_ref[...] + pltpu.roll(xc, HEAD_DIM - ROPE_DIM // 2, 1) * s1_ref[...]
                + pltpu.roll(xc, ROPE_DIM // 2, 1) * s2_ref[...])

    @pl.when(j == 0)
    def _():
        for ch in range(SEG // HEAD_DIM):
            pb_ref[0, :, ch * HEAD_DIM:(ch + 1) * HEAD_DIM] = rope_chunk(ch).astype(BF)

    @pl.when(j == 1)
    def _():
        for ch in range(SEG // HEAD_DIM):
            r = rope_chunk(ch)
            dk_ref[:, ch * HEAD_DIM:(ch + 1) * HEAD_DIM] = r
            pb_ref[0, :, ch * HEAD_DIM:(ch + 1) * HEAD_DIM] = r.astype(BF)

    def plain(jj, f32_ref):
        @pl.when(j == jj)
        def _():
            if f32_ref is not None:
                f32_ref[...] = acc
            pb_ref[0] = acc.astype(BF)

    plain(2, dv_ref)
    plain(3, None)
    plain(4, sk_ref)
    plain(5, sv_ref)


def _rope_tables(pos):
    half = ROPE_DIM // 2
    inv_freq = jnp.power(ROPE_THETA, -jnp.arange(half, dtype=F32) * (2.0 / ROPE_DIM))
    ang = pos.astype(F32)[:, None] * inv_freq[None, :]
    cos, sin = jnp.cos(ang), jnp.sin(ang)
    n = pos.shape[0]
    c = jnp.concatenate([cos, cos, jnp.ones((n, HEAD_DIM - ROPE_DIM), F32)], axis=1)
    s1 = jnp.concatenate([-sin, jnp.zeros((n, HEAD_DIM - half), F32)], axis=1)
    s2 = jnp.concatenate([jnp.zeros((n, half), F32), sin, jnp.zeros((n, HEAD_DIM - ROPE_DIM), F32)], axis=1)
    return c, s1, s2


def _inproj(x, w_in_b, pos_rows, tm):
    t = x.shape[0]
    c, s1, s2 = _rope_tables(pos_rows)
    row = lambda i, j: (i, 0)
    tab = pl.BlockSpec((tm, HEAD_DIM), row)
    f32_out = pl.BlockSpec((tm, SEG), row)
    return pl.pallas_call(
        _inproj_kernel,
        grid=(t // tm, N_SEG),
        in_specs=[pl.BlockSpec((tm, D_MODEL), row),
                  pl.BlockSpec((D_MODEL, SEG), lambda i, j: (0, j)),
                  tab, tab, tab],
        out_specs=[pl.BlockSpec((1, tm, SEG), lambda i, j: (j, i, 0)),
                   f32_out, f32_out, f32_out, f32_out],
        out_shape=[jax.ShapeDtypeStruct((N_SEG, t, SEG), BF)] + [jax.ShapeDtypeStruct((t, SEG), F32)] * 4,
        scratch_shapes=[pltpu.VMEM((tm, D_MODEL), BF)],
        compiler_params=_cparams(2),
        name="inproj",
    )(x, w_in_b, c, s1, s2)


def _matmul_kernel(x_ref, w_ref, *o_refs):
    acc = jnp.dot(x_ref[...], w_ref[...], preferred_element_type=F32)
    for o in o_refs:
        o[...] = acc.astype(o.dtype)


def _matmul(x_b, w_b, out_dtypes, tm, tn, name):
    t, k = x_b.shape
    n = w_b.shape[1]
    return pl.pallas_call(
        _matmul_kernel,
        grid=(t // tm, n // tn),
        in_specs=[pl.BlockSpec((tm, k), lambda i, j: (i, 0)),
                  pl.BlockSpec((k, tn), lambda i, j: (0, j))],
        out_specs=[pl.BlockSpec((tm, tn), lambda i, j: (i, j)) for _ in out_dtypes],
        out_shape=[jax.ShapeDtypeStruct((t, n), d) for d in out_dtypes],
        compiler_params=_cparams(2),
        name=name,
    )(x_b, w_b)


def _proj_ln_kernel(*refs, n_in):
    a_refs = refs[:n_in]
    w_refs = refs[n_in:2 * n_in]
    res_ref, g_ref, b_ref, of_ref, ob_ref = refs[2 * n_in:]
    acc = None
    for a, w in zip(a_refs, w_refs):
        d = jnp.dot(a[...], w[...], preferred_element_type=F32)
        acc = d if acc is None else acc + d
    out = _layer_norm(DN_ALPHA * res_ref[...] + acc, g_ref[...], b_ref[...])
    of_ref[...] = out
    ob_ref[...] = out.astype(BF)


def _proj_ln(a_list, w_b, res, g, b, tm, name):
    t = res.shape[0]
    n_in = len(a_list)
    ka = a_list[0].shape[1]
    row = lambda i: (i, 0)
    in_specs = ([pl.BlockSpec((tm, ka), row) for _ in a_list]
                + [pl.BlockSpec((ka, D_MODEL), functools.partial(lambda i, r: (r, 0), r=r)) for r in range(n_in)]
                + [pl.BlockSpec((tm, D_MODEL), row),
                   pl.BlockSpec((1, D_MODEL), lambda i: (0, 0)),
                   pl.BlockSpec((1, D_MODEL), lambda i: (0, 0))])
    return pl.pallas_call(
        functools.partial(_proj_ln_kernel, n_in=n_in),
        grid=(t // tm,),
        in_specs=in_specs,
        out_specs=[pl.BlockSpec((tm, D_MODEL), row), pl.BlockSpec((tm, D_MODEL), row)],
        out_shape=[jax.ShapeDtypeStruct((t, D_MODEL), F32), jax.ShapeDtypeStruct((t, D_MODEL), BF)],
        compiler_params=_cparams(1),
        name=name,
    )(*a_list, *([w_b] * n_in), res, g.reshape(1, D_MODEL), b.reshape(1, D_MODEL))


def _diff_kernel(lam_ref, q_ref, k_ref, v_ref, g_ref, o_ref, m_ref, l_ref, acc_ref,
                 *, tq, tk, q_off, nq, lam_init):
    qi = pl.program_id(2)
    p0 = q_off + qi * tq
    if nq == 1:
        last = (q_off + tq + tk - 1) // tk - 1
    else:
        last = q_off // tk + qi
    scale = HEAD_DIM ** -0.5
    m_ref[...] = jnp.full(m_ref.shape, -1e30, F32)
    l_ref[...] = jnp.zeros(l_ref.shape, F32)
    acc_ref[...] = jnp.zeros(acc_ref.shape, F32)
    q = q_ref[0]

    def step(kb, masked):
        k0 = pl.multiple_of(kb * tk, tk)
        kblk = k_ref[0, pl.ds(k0, tk), :]
        vblk = v_ref[0, pl.ds(k0, tk), :]
        if masked:
            qpos = p0 + lax.broadcasted_iota(jnp.int32, (tq, tk), 0)
            kpos = k0 + lax.broadcasted_iota(jnp.int32, (tq, tk), 1)
            vis = lax.shift_right_logical(kpos, 6) <= lax.shift_right_logical(qpos, 6)
        for c in range(2):
            s = lax.dot_general(q[:, c * HEAD_DIM:(c + 1) * HEAD_DIM],
                                kblk[:, c * HEAD_DIM:(c + 1) * HEAD_DIM], _NT,
                                preferred_element_type=F32) * scale
            if masked:
                s = jnp.where(vis, s, -jnp.inf)
            m_old = m_ref[c]
            m_new = jnp.maximum(m_old, jnp.max(s, axis=-1, keepdims=True))
            alpha = jnp.exp(m_old - m_new)
            p = jnp.exp(s - m_new)
            l_ref[c] = alpha * l_ref[c] + jnp.sum(p, axis=-1, keepdims=True)
            acc_ref[c] = alpha * acc_ref[c] + jnp.dot(p.astype(BF), vblk, preferred_element_type=F32)
            m_ref[c] = m_new

    def body(kb, carry):
        step(kb, False)
        return carry

    lax.fori_loop(0, last, body, 0)
    step(last, True)

    of = acc_ref[0] / l_ref[0] - lam_ref[0] * (acc_ref[1] / l_ref[1])
    of = of * lax.rsqrt(jnp.mean(of * of, axis=-1, keepdims=True) + LN_EPS) * g_ref[...] * (1.0 - lam_init)
    o_ref[...] = of.astype(BF)


def _diff_attn(lam, q_arr, q_seg, k_arr, k_seg, v_arr, v_seg, subln_g, n_batch, t_q, t_k, q_off, tq, tk, lam_init):
    assert CHUNK == 64
    nq = t_q // tq
    if nq > 1:
        assert tq == tk and q_off % tk == 0
    else:
        assert ((q_off + tq + tk - 1) // tk - 1) * tk <= q_off + CHUNK
    assert ((q_off + t_q + tk - 1) // tk) * tk <= t_k
    kern = functools.partial(_diff_kernel, tq=tq, tk=tk, q_off=q_off, nq=nq, lam_init=lam_init)
    return pl.pallas_call(
        kern,
        grid=(n_batch, N_DIFF_HEADS, nq),
        in_specs=[pl.BlockSpec(memory_space=pltpu.SMEM),
                  pl.BlockSpec((1, tq, DIFF_V_DIM), lambda b, h, i: (q_seg, b * nq + i, h)),
                  pl.BlockSpec((1, t_k, DIFF_V_DIM), lambda b, h, i: (k_seg, b, h)),
                  pl.BlockSpec((1, t_k, DIFF_V_DIM), lambda b, h, i: (v_seg, b, h)),
                  pl.BlockSpec((1, DIFF_V_DIM), lambda b, h, i: (0, 0))],
        out_specs=pl.BlockSpec((tq, DIFF_V_DIM), lambda b, h, i: (b * nq + i, h)),
        out_shape=jax.ShapeDtypeStruct((n_batch * t_q, SEG), BF),
        scratch_shapes=[pltpu.VMEM((2, tq, 1), F32), pltpu.VMEM((2, tq, 1), F32),
                        pltpu.VMEM((2, tq, DIFF_V_DIM), F32)],
        compiler_params=_cparams(3),
        name="diff_attn",
    )(lam, q_arr, k_arr, v_arr, subln_g.reshape(1, DIFF_V_DIM))


def _sb_kernel(q_ref, k_ref, v_ref, o_ref, vt_ref, carry_ref, acc_ref, *, tq, tq_pad, tk, t_k, q_off, nq):
    qi = pl.program_id(2)
    p0 = q_off + qi * tq
    if nq == 1:
        last = (q_off + tq + tk - 1) // tk - 1
    else:
        last = q_off // tk + qi
    scale = HEAD_DIM ** -0.5

    @pl.when(qi == 0)
    def _():
        for kb in range(t_k // tk):
            vt_ref[kb] = v_ref[0, kb * tk:(kb + 1) * tk, :].astype(F32).T.astype(BF)

    q = q_ref[0]
    if tq_pad > tq:
        q = jnp.concatenate([q, jnp.zeros((tq_pad - tq, HEAD_DIM), BF)], axis=0)
    carry_ref[...] = jnp.zeros(carry_ref.shape, F32)
    acc_ref[...] = jnp.zeros(acc_ref.shape, F32)
    r_i = lax.broadcasted_iota(jnp.int32, (tk, tk), 0)
    c_i = lax.broadcasted_iota(jnp.int32, (tk, tk), 1)
    tri = jnp.where(c_i > r_i, 1.0, 0.0).astype(BF)

    def step(kb, masked):
        k0 = pl.multiple_of(kb * tk, tk)
        kblk = k_ref[0, pl.ds(k0, tk), :]
        z = lax.dot_general(kblk, q, _NT, preferred_element_type=F32) * scale
        t = jnp.log(1.0 + jnp.exp(-jnp.abs(z)))
        lk = -(jnp.maximum(z, 0.0) + t)
        lp = jnp.minimum(z, 0.0) - t
        if masked:
            kpos = k0 + lax.broadcasted_iota(jnp.int32, (tk, tq_pad), 0)
            qpos = p0 + lax.broadcasted_iota(jnp.int32, (tk, tq_pad), 1)
            before = kpos < qpos
            lk = jnp.where(before, lk, 0.0)
        hi = lk.astype(BF)
        lo = (lk - hi.astype(F32)).astype(BF)
        later = (jnp.dot(tri, hi, preferred_element_type=F32) + jnp.dot(tri, lo, preferred_element_type=F32)
                 + carry_ref[...])
        a = jnp.exp(lp + later)
        if masked:
            a = jnp.where(before, a, 0.0)
        acc_ref[...] += jnp.dot(vt_ref[kb], a.astype(BF), preferred_element_type=F32)
        carry_ref[...] += jnp.sum(lk, axis=0, keepdims=True)

    step(last, True)

    def body(i, carry):
        step(last - 1 - i, False)
        return carry

    lax.fori_loop(0, last, body, 0)
    o = acc_ref[...].T
    o_ref[...] = o[:tq].astype(BF)


def _sb_attn(q_arr, q_seg, k_arr, k_seg, v_arr, v_seg, n_batch, t_q, t_k, q_off, tq, tk):
    nq = t_q // tq
    tq_pad = max(tq, 128)
    if nq > 1:
        assert tq == tk and q_off % tk == 0
    else:
        assert ((q_off + tq + tk - 1) // tk - 1) * tk <= q_off
    assert ((q_off + t_q + tk - 1) // tk) * tk <= t_k
    kern = functools.partial(_sb_kernel, tq=tq, tq_pad=tq_pad, tk=tk, t_k=t_k, q_off=q_off, nq=nq)
    return pl.pallas_call(
        kern,
        grid=(n_batch, N_SB_HEADS, nq),
        in_specs=[pl.BlockSpec((1, tq, HEAD_DIM), lambda b, h, i: (q_seg, b * nq + i, h)),
                  pl.BlockSpec((1, t_k, HEAD_DIM), lambda b, h, i: (k_seg, b, h)),
                  pl.BlockSpec((1, t_k, HEAD_DIM), lambda b, h, i: (v_seg, b, h))],
        out_specs=pl.BlockSpec((tq, HEAD_DIM), lambda b, h, i: (b * nq + i, h)),
        out_shape=jax.ShapeDtypeStruct((n_batch * t_q, SEG), BF),
        scratch_shapes=[pltpu.VMEM((t_k // tk, HEAD_DIM, tk), BF), pltpu.VMEM((1, tq_pad), F32),
                        pltpu.VMEM((HEAD_DIM, tq_pad), F32)],
        compiler_params=_cparams(3),
        name="sb_attn",
    )(q_arr, k_arr, v_arr)


def _cross_kernel(q_ref, k_ref, v_ref, o_ref):
    scale = MEM_HEAD_DIM ** -0.5
    for h in range(N_MEM_HEADS):
        sl = slice(h * MEM_HEAD_DIM, (h + 1) * MEM_HEAD_DIM)
        s = lax.dot_general(q_ref[:, sl], k_ref[:, sl], _NT, preferred_element_type=F32) * scale
        p = jnp.exp(s - jnp.max(s, axis=-1, keepdims=True))
        l = jnp.sum(p, axis=-1, keepdims=True)
        o = jnp.dot(p.astype(BF), v_ref[:, sl], preferred_element_type=F32) / l
        o_ref[:, sl] = o.astype(BF)


def _cross_attn(q_b, memk_b, memv_b, n_batch, t_b, tm):
    n_mem = memk_b.shape[0] // n_batch
    nt = t_b // tm
    return pl.pallas_call(
        _cross_kernel,
        grid=(n_batch, nt),
        in_specs=[pl.BlockSpec((tm, D_MODEL), lambda b, i: (b * nt + i, 0)),
                  pl.BlockSpec((n_mem, D_MODEL), lambda b, i: (b, 0)),
                  pl.BlockSpec((n_mem, D_MODEL), lambda b, i: (b, 0))],
        out_specs=pl.BlockSpec((tm, D_MODEL), lambda b, i: (b * nt + i, 0)),
        out_shape=jax.ShapeDtypeStruct((n_batch * t_b, D_MODEL), BF),
        compiler_params=_cparams(2),
        name="cross_attn",
    )(q_b, memk_b, memv_b)


def _gelu_tanh(x):
    return 0.5 * x * (1.0 + jnp.tanh(math.sqrt(2.0 / math.pi) * (x + 0.044715 * (x * x * x))))


def _ffn_kernel(h_ref, wg_ref, wu_ref, wd_ref, wc_ref, bc_ref, st_ref, g_ref, b_ref,
                o_ref, last_ref, hb_ref, carry_ref, gs_ref, *, tm, fc, seq, n_ff):
    i = pl.program_id(0)
    c = pl.program_id(1)

    @pl.when(c == 0)
    def _():
        hb_ref[...] = h_ref[...].astype(BF)

    @pl.when((i == 0) & (c == 0))
    def _():
        carry_ref[...] = jnp.zeros(carry_ref.shape, F32)

    hb = hb_ref[...]
    g = jnp.dot(hb, wg_ref[...], preferred_element_type=F32)
    u = jnp.dot(hb, wu_ref[...], preferred_element_type=F32)
    row = lax.broadcasted_iota(jnp.int32, (tm, fc), 0)
    if seq >= tm:
        tiles_per_seq = seq // tm
        at_start = (i % tiles_per_seq) == 0
        prev = jnp.where(at_start, st_ref[0], carry_ref[c])
        pos = row
        ex0 = prev[0:1]
        ex1 = prev[1:2]
        carry_ref[c] = g[tm - 2:tm, :]
        last_ref[0] = g[tm - 2:tm, :]
    else:
        ns = tm // seq
        pos = jnp.bitwise_and(row, seq - 1)
        ex0 = jnp.broadcast_to(st_ref[0][:, None, :], (ns, seq, fc)).reshape(tm, fc)
        ex1 = jnp.broadcast_to(st_ref[1][:, None, :], (ns, seq, fc)).reshape(tm, fc)
        for k in range(fc // 128):
            lanes = slice(k * 128, (k + 1) * 128)
            gs_ref[k] = g[:, lanes]
            last_ref[0, :, lanes] = gs_ref[k, pl.ds(seq - 2, ns, stride=seq), :]
            last_ref[1, :, lanes] = gs_ref[k, pl.ds(seq - 1, ns, stride=seq), :]
    g_m1 = jnp.where(pos == 0, ex1, pltpu.roll(g, 1, 0))
    g_m2 = jnp.where(pos == 0, ex0, jnp.where(pos == 1, ex1, pltpu.roll(g, 2, 0)))
    conv = bc_ref[...] + g_m2 * wc_ref[0:1, :] + g_m1 * wc_ref[1:2, :] + g * wc_ref[2:3, :]
    a = (_gelu_tanh(conv) * u).astype(BF)
    d = jnp.dot(a, wd_ref[...], preferred_element_type=F32)

    @pl.when(c == 0)
    def _():
        o_ref[...] = d

    @pl.when(c > 0)
    def _():
        o_ref[...] += d

    @pl.when(c == n_ff - 1)
    def _():
        o_ref[...] = _layer_norm(DN_ALPHA * h_ref[...] + o_ref[...], g_ref[...], b_ref[...])


def _ffn(h, w_gate_b, w_up_b, w_down_b, w_conv, b_conv, state, ln_g, ln_b, seq, tm, fc):
    t = h.shape[0]
    n_ff = D_FF // fc
    nt = t // tm
    if seq >= tm:
        tiles_per_seq = seq // tm
        st_spec = pl.BlockSpec((1, 2, fc), lambda i, c: (i // tiles_per_seq, 0, c))
        last_spec = pl.BlockSpec((1, 2, fc), lambda i, c: (i, 0, c))
        last_shape = jax.ShapeDtypeStruct((nt, 2, D_FF), F32)
    else:
        ns = tm // seq
        st_spec = pl.BlockSpec((2, ns, fc), lambda i, c: (0, i, c))
        last_spec = pl.BlockSpec((2, ns, fc), lambda i, c: (0, i, c))
        last_shape = jax.ShapeDtypeStruct((2, t // seq, D_FF), F32)
    kern = functools.partial(_ffn_kernel, tm=tm, fc=fc, seq=seq, n_ff=n_ff)
    return pl.pallas_call(
        kern,
        grid=(nt, n_ff),
        in_specs=[pl.BlockSpec((tm, D_MODEL), lambda i, c: (i, 0)),
                  pl.BlockSpec((D_MODEL, fc), lambda i, c: (0, c)),
                  pl.BlockSpec((D_MODEL, fc), lambda i, c: (0, c)),
                  pl.BlockSpec((fc, D_MODEL), lambda i, c: (c, 0)),
                  pl.BlockSpec((CONV_W, fc), lambda i, c: (0, c)),
                  pl.BlockSpec((1, fc), lambda i, c: (0, c)),
                  st_spec,
                  pl.BlockSpec((1, D_MODEL), lambda i, c: (0, 0)),
                  pl.BlockSpec((1, D_MODEL), lambda i, c: (0, 0))],
        out_specs=[pl.BlockSpec((tm, D_MODEL), lambda i, c: (i, 0)), last_spec],
        out_shape=[jax.ShapeDtypeStruct((t, D_MODEL), F32), last_shape],
        scratch_shapes=[pltpu.VMEM((tm, D_MODEL), BF), pltpu.VMEM((n_ff, 2, fc), F32),
                        pltpu.VMEM((fc // 128, tm, 128), F32)],
        compiler_params=_cparams(2),
        name="conv_ffn",
    )(h, w_gate_b, w_up_b, w_down_b, w_conv, b_conv.reshape(1, D_FF), state,
      ln_g.reshape(1, D_MODEL), ln_b.reshape(1, D_MODEL))


def _group(x, pos_rows, n_batch, t_b, kv, q_off, t_k, tq, memk_b, memv_b, conv_state, seq_tm, w, lam, lam_init):
    tm = 512
    pb, dk, dv, sk, sv = _inproj(x, w['w_in'], pos_rows, tm)
    if kv is None:
        kd = vd = ks = vs = pb
        segs = (1, 2, 4, 5)
    else:
        kd, vd, ks, vs = kv(pb)
        segs = (0, 0, 0, 0)
    tk = 256
    od = _diff_attn(lam, pb, 0, kd, segs[0], vd, segs[1], w['subln_g'], n_batch, t_b, t_k, q_off, tq, tk, lam_init)
    os_ = _sb_attn(pb, 3, ks, segs[2], vs, segs[3], n_batch, t_b, t_k, q_off, tq, tk)
    h1, h1b = _proj_ln([od, os_], w['w_o'], x, w['ln1_g'], w['ln1_b'], tm, "out_proj_ln1")
    (qc,) = _matmul(h1b, w['w_mq'], [BF], tm, 1024, "cross_q")
    oc = _cross_attn(qc, memk_b, memv_b, n_batch, t_b, min(t_b, tm))
    h2, _ = _proj_ln([oc], w['w_mo'], h1, w['ln2_g'], w['ln2_b'], tm, "cross_out_ln2")
    y, last = _ffn(h2, w['w_gate'], w['w_up'], w['w_down'], w['w_conv'], w['b_conv'], conv_state,
                   w['ln3_g'], w['ln3_b'], t_b, seq_tm, 512)
    return y, dk, dv, sk, sv, last


def kernel(x_prompt, x_sample, mem_prompt, cache_diff_k, cache_diff_v, cache_sb_k, cache_sb_v, cache_mem_k, cache_mem_v, state_ffn_conv, w_in, lambda_q1, lambda_k1, lambda_q2, lambda_k2, subln_g, w_o, ln1_g, ln1_b, w_mq, w_mk, w_mv, w_mo, ln2_g, ln2_b, w_gate, w_up, w_conv, b_conv, w_down, ln3_g, ln3_b):
    assert w_in.shape[0] == DEPTH == 1
    n_p, t_p, _ = x_prompt.shape
    n_s, t_s, _ = x_sample.shape
    past = cache_diff_k.shape[2]
    n_mem = mem_prompt.shape[1]
    l = 0
    lam_init = 0.8 - 0.6 * math.exp(-0.3 * l)
    lam = (jnp.exp(jnp.sum(lambda_q1[l] * lambda_k1[l])) - jnp.exp(jnp.sum(lambda_q2[l] * lambda_k2[l]))
           + lam_init).reshape(1).astype(F32)
    w = {'w_in': w_in[l].astype(BF), 'w_o': w_o[l].astype(BF), 'w_mq': w_mq[l].astype(BF),
         'w_mo': w_mo[l].astype(BF), 'w_gate': w_gate[l].astype(BF), 'w_up': w_up[l].astype(BF),
         'w_down': w_down[l].astype(BF), 'w_conv': w_conv[l], 'b_conv': b_conv[l],
         'subln_g': subln_g[l], 'ln1_g': ln1_g[l], 'ln1_b': ln1_b[l], 'ln2_g': ln2_g[l], 'ln2_b': ln2_b[l],
         'ln3_g': ln3_g[l], 'ln3_b': ln3_b[l]}

    w_mkv = jnp.concatenate([w_mk[l], w_mv[l]], axis=1).astype(BF)
    mem_b = mem_prompt.reshape(n_p * n_mem, D_MODEL).astype(BF)
    mkv_f, mkv_b = _matmul(mem_b, w_mkv, [F32, BF], n_p * n_mem, 1024, "memory_kv")
    pos_p = jnp.tile(jnp.arange(t_p), n_p)
    zero_state = jnp.zeros((n_p, CONV_W - 1, D_FF), F32)
    yp, dkp, dvp, skp, svp, lastp = _group(
        x_prompt.reshape(n_p * t_p, D_MODEL), pos_p, n_p, t_p, None, 0, t_p, 256,
        mkv_b[:, :D_MODEL], mkv_b[:, D_MODEL:], zero_state, 512, w, lam, lam_init)

    t_k = ((past + t_s + 255) // 256) * 256

    def sample_kv(pb):
        def cat(cache, seg):
            new = pb[seg].reshape(n_s, t_s, SEG)
            old = cache[l].reshape(n_s, past, SEG).astype(BF)
            pad = jnp.zeros((n_s, t_k - past - t_s, SEG), BF)
            return jnp.concatenate([old, new, pad], axis=1).reshape(1, n_s * t_k, SEG)
        return cat(cache_diff_k, 1), cat(cache_diff_v, 2), cat(cache_sb_k, 4), cat(cache_sb_v, 5)

    pos_s = jnp.tile(past + jnp.arange(t_s), n_s)
    memk_s = cache_mem_k[l].reshape(n_s * n_mem, D_MODEL).astype(BF)
    memv_s = cache_mem_v[l].reshape(n_s * n_mem, D_MODEL).astype(BF)
    state_s = jnp.transpose(state_ffn_conv[l], (1, 0, 2))
    ys, dks, dvs, sks, svs, lasts = _group(
        x_sample.reshape(n_s * t_s, D_MODEL), pos_s, n_s, t_s, sample_kv, past, t_k, t_s,
        memk_s, memv_s, state_s, 512, w, lam, lam_init)

    tiles_per_seq = t_p // 512
    new_conv_p = lastp[tiles_per_seq - 1::tiles_per_seq]
    new_conv_s = jnp.transpose(lasts, (1, 0, 2))
    return (yp.reshape(n_p, t_p, D_MODEL), ys.reshape(n_s, t_s, D_MODEL),
            dkp.reshape(1, n_p, t_p, N_DIFF_HEADS, 2, HEAD_DIM), dvp.reshape(1, n_p, t_p, N_DIFF_HEADS, DIFF_V_DIM),
            skp.reshape(1, n_p, t_p, N_SB_HEADS, HEAD_DIM), svp.reshape(1, n_p, t_p, N_SB_HEADS, HEAD_DIM),
            mkv_f[:, :D_MODEL].reshape(1, n_p, n_mem, N_MEM_HEADS, MEM_HEAD_DIM),
            mkv_f[:, D_MODEL:].reshape(1, n_p, n_mem, N_MEM_HEADS, MEM_HEAD_DIM),
            new_conv_p[None],
            dks.reshape(1, n_s, t_s, N_DIFF_HEADS, 2, HEAD_DIM), dvs.reshape(1, n_s, t_s, N_DIFF_HEADS, DIFF_V_DIM),
            sks.reshape(1, n_s, t_s, N_SB_HEADS, HEAD_DIM), svs.reshape(1, n_s, t_s, N_SB_HEADS, HEAD_DIM),
            new_conv_s[None])
```

```python
import functools
import math

import jax
import jax.numpy as jnp
from jax import lax
from jax.experimental import pallas as pl
from jax.experimental.pallas import tpu as pltpu

BF = jnp.bfloat16
F32 = jnp.float32

D_MODEL = 2048
CHUNK = 64
CHUNK_SHIFT = 6
HEAD_DIM = 128
N_DIFF_HEADS = 4
N_SB_HEADS = 8
DIFF_V_DIM = 2 * HEAD_DIM
ROPE_DIM = HEAD_DIM // 4
ROPE_THETA = 500000.0
N_MEM_HEADS = 4
MEM_HEAD_DIM = D_MODEL // N_MEM_HEADS
D_FF = 5632
CONV_W = 3
LN_EPS = 1e-5
DEPTH = 1
DN_ALPHA = (2 * DEPTH) ** 0.25
SEG = 1024
N_SEG = 6
VMEM_LIMIT = 56 * 1024 * 1024
LOG2E = math.log2(math.e)
QK_SCALE2 = HEAD_DIM ** -0.5 * LOG2E
NEG_BIG = -1e30

_NT = (((1,), (1,)), ((), ()))


def _cparams(n_axes):
    return pltpu.CompilerParams(dimension_semantics=("arbitrary",) * n_axes,
                                vmem_limit_bytes=VMEM_LIMIT)


def _layer_norm(y, g, b):
    mu = jnp.mean(y, axis=-1, keepdims=True)
    yc = y - mu
    var = jnp.mean(yc * yc, axis=-1, keepdims=True)
    return yc * lax.rsqrt(var + LN_EPS) * g + b


def _inproj_kernel(x_ref, w_ref, c_ref, s1_ref, s2_ref,
                   pb_ref, dk_ref, dv_ref, sk_ref, sv_ref, xb_ref):
    j = pl.program_id(1)

    @pl.when(j == 0)
    def _():
        xb_ref[...] = x_ref[...].astype(BF)

    acc = jnp.dot(xb_ref[...], w_ref[...], preferred_element_type=F32)

    def rope_chunk(ch):
        xc = acc[:, ch * HEAD_DIM:(ch + 1) * HEAD_DIM]
        return (xc * c_ref[...] + pltpu.roll(xc, HEAD_DIM - ROPE_DIM // 2, 1) * s1_ref[...]
                + pltpu.roll(xc, ROPE_DIM // 2, 1) * s2_ref[...])

    @pl.when(j == 0)
    def _():
        for ch in range(SEG // HEAD_DIM):
            pb_ref[0, :, ch * HEAD_DIM:(ch + 1) * HEAD_DIM] = rope_chunk(ch).astype(BF)

    @pl.when(j == 1)
    def _():
        for ch in range(SEG // HEAD_DIM):
            r = rope_chunk(ch)
            dk_ref[:, ch * HEAD_DIM:(ch + 1) * HEAD_DIM] = r
            pb_ref[0, :, ch * HEAD_DIM:(ch + 1) * HEAD_DIM] = r.astype(BF)

    def plain(jj, f32_ref):
        @pl.when(j == jj)
        def _():
            if f32_ref is not None:
                f32_ref[...] = acc
            pb_ref[0] = acc.astype(BF)

    plain(2, dv_ref)
    plain(3, None)
    plain(4, sk_ref)
    plain(5, sv_ref)


def _rope_tables(pos):
    half = ROPE_DIM // 2
    inv_freq = jnp.power(ROPE_THETA, -jnp.arange(half, dtype=F32) * (2.0 / ROPE_DIM))
    ang = pos.astype(F32)[:, None] * inv_freq[None, :]
    cos, sin = jnp.cos(ang), jnp.sin(ang)
    n = pos.shape[0]
    c = jnp.concatenate([cos, cos, jnp.ones((n, HEAD_DIM - ROPE_DIM), F32)], axis=1)
    s1 = jnp.concatenate([-sin, jnp.zeros((n, HEAD_DIM - half), F32)], axis=1)
    s2 = jnp.concatenate([jnp.zeros((n, half), F32), sin, jnp.zeros((n, HEAD_DIM - ROPE_DIM), F32)], axis=1)
    return c, s1, s2


def _inproj(x, w_in_b, pos_rows, tm):
    t = x.shape[0]
    c, s1, s2 = _rope_tables(pos_rows)
    row = lambda i, j: (i, 0)
    tab = pl.BlockSpec((tm, HEAD_DIM), row)
    f32_out = pl.BlockSpec((tm, SEG), row)
    return pl.pallas_call(
        _inproj_kernel,
        grid=(t // tm, N_SEG),
        in_specs=[pl.BlockSpec((tm, D_MODEL), row),
                  pl.BlockSpec((D_MODEL, SEG), lambda i, j: (0, j)),
                  tab, tab, tab],
        out_specs=[pl.BlockSpec((1, tm, SEG), lambda i, j: (j, i, 0)),
                   f32_out, f32_out, f32_out, f32_out],
        out_shape=[jax.ShapeDtypeStruct((N_SEG, t, SEG), BF)] + [jax.ShapeDtypeStruct((t, SEG), F32)] * 4,
        scratch_shapes=[pltpu.VMEM((tm, D_MODEL), BF)],
        compiler_params=_cparams(2),
        name="inproj",
    )(x, w_in_b, c, s1, s2)


def _matmul_kernel(x_ref, w_ref, *o_refs):
    acc = jnp.dot(x_ref[...], w_ref[...], preferred_element_type=F32)
    for o in o_refs:
        o[...] = acc.astype(o.dtype)


def _matmul(x_b, w_b, out_dtypes, tm, tn, name):
    t, k = x_b.shape
    n = w_b.shape[1]
    return pl.pallas_call(
        _matmul_kernel,
        grid=(t // tm, n // tn),
        in_specs=[pl.BlockSpec((tm, k), lambda i, j: (i, 0)),
                  pl.BlockSpec((k, tn), lambda i, j: (0, j))],
        out_specs=[pl.BlockSpec((tm, tn), lambda i, j: (i, j)) for _ in out_dtypes],
        out_shape=[jax.ShapeDtypeStruct((t, n), d) for d in out_dtypes],
        compiler_params=_cparams(2),
        name=name,
    )(x_b, w_b)


def _proj_ln_kernel(*refs, n_in):
    a_refs = refs[:n_in]
    w_refs = refs[n_in:2 * n_in]
    res_ref, g_ref, b_ref, of_ref, ob_ref = refs[2 * n_in:]
    acc = None
    for a, w in zip(a_refs, w_refs):
        d = jnp.dot(a[...], w[...], preferred_element_type=F32)
        acc = d if acc is None else acc + d
    out = _layer_norm(DN_ALPHA * res_ref[...] + acc, g_ref[...], b_ref[...])
    of_ref[...] = out
    ob_ref[...] = out.astype(BF)


def _proj_ln(a_list, w_b, res, g, b, tm, name):
    t = res.shape[0]
    n_in = len(a_list)
    ka = a_list[0].shape[1]
    row = lambda i: (i, 0)
    in_specs = ([pl.BlockSpec((tm, ka), row) for _ in a_list]
                + [pl.BlockSpec((ka, D_MODEL), functools.partial(lambda i, r: (r, 0), r=r)) for r in range(n_in)]
                + [pl.BlockSpec((tm, D_MODEL), row),
                   pl.BlockSpec((1, D_MODEL), lambda i: (0, 0)),
                   pl.BlockSpec((1, D_MODEL), lambda i: (0, 0))])
    return pl.pallas_call(
        functools.partial(_proj_ln_kernel, n_in=n_in),
        grid=(t // tm,),
        in_specs=in_specs,
        out_specs=[pl.BlockSpec((tm, D_MODEL), row), pl.BlockSpec((tm, D_MODEL), row)],
        out_shape=[jax.ShapeDtypeStruct((t, D_MODEL), F32), jax.ShapeDtypeStruct((t, D_MODEL), BF)],
        compiler_params=_cparams(1),
        name=name,
    )(*a_list, *([w_b] * n_in), res, g.reshape(1, D_MODEL), b.reshape(1, D_MODEL))


def _chunk_visible(q0, k0, nq, nk):
    qpos = q0 + lax.broadcasted_iota(jnp.int32, (nq, nk), 0)
    kpos = k0 + lax.broadcasted_iota(jnp.int32, (nq, nk), 1)
    return lax.shift_right_logical(kpos, CHUNK_SHIFT) <= lax.shift_right_logical(qpos, CHUNK_SHIFT)


def _subln(o0, o1, lam, g, lam_init):
    of = o0 - lam * o1
    return of * lax.rsqrt(jnp.mean(of * of, axis=-1, keepdims=True) + LN_EPS) * g * (1.0 - lam_init)


def _diff_prompt_kernel(lam_ref, q_ref, k_ref, v_ref, g_ref, o_ref, m_ref, l_ref, acc_ref, *, tq, sub, lam_init):
    qi = pl.program_id(2)
    m_ref[...] = jnp.full(m_ref.shape, NEG_BIG, F32)
    l_ref[...] = jnp.zeros(l_ref.shape, F32)
    acc_ref[...] = jnp.zeros(acc_ref.shape, F32)

    def run(k0, items):
        scores = []
        for c, rows, nk, vis in items:
            cs = slice(c * HEAD_DIM, (c + 1) * HEAD_DIM)
            s = lax.dot_general(q_ref[0, rows, cs], k_ref[0, pl.ds(k0, nk), cs], _NT,
                                preferred_element_type=F32) * QK_SCALE2
            scores.append(s if vis is None else jnp.where(vis, s, -jnp.inf))
        probs = []
        for s, (c, rows, nk, _) in zip(scores, items):
            m_old = m_ref[c, rows]
            m_new = jnp.maximum(m_old, jnp.max(s, axis=-1, keepdims=True))
            alpha = jnp.exp2(m_old - m_new)
            p = jnp.exp2(s - m_new)
            l_ref[c, rows] = alpha * l_ref[c, rows] + jnp.sum(p, axis=-1, keepdims=True)
            m_ref[c, rows] = m_new
            probs.append((alpha, p.astype(BF)))
        for (alpha, p), (c, rows, nk, _) in zip(probs, items):
            acc_ref[c, rows] = alpha * acc_ref[c, rows] + jnp.dot(p, v_ref[0, pl.ds(k0, nk), :],
                                                                  preferred_element_type=F32)

    half = tq // 2
    full_items = [(c, slice(r * half, (r + 1) * half), tq, None) for r in range(2) for c in range(2)]

    def body(g, carry):
        run(pl.multiple_of(g * tq, tq), full_items)
        return carry

    lax.fori_loop(0, qi, body, 0)

    diag_items = [(c, slice(r * sub, (r + 1) * sub), (r + 1) * sub, _chunk_visible(r * sub, 0, sub, (r + 1) * sub))
                  for r in range(tq // sub) for c in range(2)]
    run(pl.multiple_of(qi * tq, tq), diag_items)

    o = _subln(acc_ref[0] / l_ref[0], acc_ref[1] / l_ref[1], lam_ref[0], g_ref[...], lam_init)
    o_ref[...] = o.astype(BF)


def _diff_attn_prompt(lam, pb, subln_g, n_batch, t, lam_init, tq=1024, sub=256):
    assert t % tq == 0 and tq % sub == 0 and sub % CHUNK == 0
    nq = t // tq
    kern = functools.partial(_diff_prompt_kernel, tq=tq, sub=sub, lam_init=lam_init)
    return pl.pallas_call(
        kern,
        grid=(n_batch, N_DIFF_HEADS, nq),
        in_specs=[pl.BlockSpec(memory_space=pltpu.SMEM),
                  pl.BlockSpec((1, tq, DIFF_V_DIM), lambda b, h, i: (0, b * nq + i, h)),
                  pl.BlockSpec((1, t, DIFF_V_DIM), lambda b, h, i: (1, b, h)),
                  pl.BlockSpec((1, t, DIFF_V_DIM), lambda b, h, i: (2, b, h)),
                  pl.BlockSpec((1, DIFF_V_DIM), lambda b, h, i: (0, 0))],
        out_specs=pl.BlockSpec((tq, DIFF_V_DIM), lambda b, h, i: (b * nq + i, h)),
        out_shape=jax.ShapeDtypeStruct((n_batch * t, SEG), BF),
        scratch_shapes=[pltpu.VMEM((2, tq, 1), F32), pltpu.VMEM((2, tq, 1), F32),
                        pltpu.VMEM((2, tq, DIFF_V_DIM), F32)],
        compiler_params=_cparams(3),
        name="diff_attn_prompt",
    )(lam, pb, pb, pb, subln_g.reshape(1, DIFF_V_DIM))


def _diff_sample_kernel(lam_ref, q_ref, kn_ref, vn_ref, kc_ref, vc_ref, g_ref, o_ref, *, t_s, past, lam_init):
    vis = _chunk_visible(past, past, t_s, t_s)
    for h in range(N_DIFF_HEADS):
        vs = slice(h * DIFF_V_DIM, (h + 1) * DIFF_V_DIM)
        v_old = vc_ref[0, :, vs].astype(BF)
        v_new = vn_ref[0, :, vs]
        o = []
        for c in range(2):
            cs = slice((2 * h + c) * HEAD_DIM, (2 * h + c + 1) * HEAD_DIM)
            q = q_ref[0, :, cs]
            s_old = lax.dot_general(q, kc_ref[0, :, cs].astype(BF), _NT, preferred_element_type=F32) * QK_SCALE2
            s_new = lax.dot_general(q, kn_ref[0, :, cs], _NT, preferred_element_type=F32) * QK_SCALE2
            s_new = jnp.where(vis, s_new, -jnp.inf)
            m = jnp.maximum(jnp.max(s_old, axis=-1, keepdims=True), jnp.max(s_new, axis=-1, keepdims=True))
            p_old = jnp.exp2(s_old - m)
            p_new = jnp.exp2(s_new - m)
            l = jnp.sum(p_old, axis=-1, keepdims=True) + jnp.sum(p_new, axis=-1, keepdims=True)
            acc = (jnp.dot(p_old.astype(BF), v_old, preferred_element_type=F32)
                   + jnp.dot(p_new.astype(BF), v_new, preferred_element_type=F32))
            o.append(acc / l)
        o_ref[:, vs] = _subln(o[0], o[1], lam_ref[0], g_ref[...], lam_init).astype(BF)


def _sample_attn_specs(t_s, past):
    new = lambda seg: pl.BlockSpec((1, t_s, SEG), lambda b: (seg, b, 0))
    old = pl.BlockSpec((1, past, SEG), lambda b: (b, 0, 0))
    return new, old


def _diff_attn_sample(lam, pb, cache_k, cache_v, subln_g, n_batch, t_s, past, lam_init):
    new, old = _sample_attn_specs(t_s, past)
    kern = functools.partial(_diff_sample_kernel, t_s=t_s, past=past, lam_init=lam_init)
    return pl.pallas_call(
        kern,
        grid=(n_batch,),
        in_specs=[pl.BlockSpec(memory_space=pltpu.SMEM), new(0), new(1), new(2), old, old,
                  pl.BlockSpec((1, DIFF_V_DIM), lambda b: (0, 0))],
        out_specs=pl.BlockSpec((t_s, SEG), lambda b: (b, 0)),
        out_shape=jax.ShapeDtypeStruct((n_batch * t_s, SEG), BF),
        compiler_params=_cparams(1),
        name="diff_attn_sample",
    )(lam, pb, pb, pb, cache_k, cache_v, subln_g.reshape(1, DIFF_V_DIM))


def _sb_logs(zraw):
    z2 = zraw * QK_SCALE2
    mx = jnp.maximum(z2, 0.0)
    mn = jnp.minimum(z2, 0.0)
    t = jnp.log(1.0 + jnp.exp2(mn - mx)) * LOG2E
    return mx + t, mn - t


def _split_bf16(x):
    hi = x.astype(BF)
    return hi, (x - hi.astype(F32)).astype(BF)


def _sb_group(items, carry, acc, key_axis):
    if key_axis == 0:
        zs = [lax.dot_general(k, q, _NT, preferred_element_type=F32) for q, k, _, _, _, _ in items]
    else:
        zs = [lax.dot_general(q, k, _NT, preferred_element_type=F32) for q, k, _, _, _, _ in items]
    logs = []
    for z, (_, _, _, tri, before, _) in zip(zs, items):
        nlk, lp = _sb_logs(z)
        if before is not None:
            nlk = jnp.where(before, nlk, 0.0)
        hi, lo = _split_bf16(nlk)
        fused = tri.shape[1 - key_axis] == 2 * tri.shape[key_axis]
        logs.append((nlk, lp, jnp.concatenate([hi, lo], axis=key_axis) if fused else (hi, lo)))
    sufs = []
    for (_, _, hl), (_, _, _, tri, _, _) in zip(logs, items):
        if key_axis == 0:
            mm = lambda x: jnp.dot(tri, x, preferred_element_type=F32)
        else:
            mm = lambda x: jnp.dot(x, tri, preferred_element_type=F32)
        sufs.append(mm(hl[0]) + mm(hl[1]) if isinstance(hl, tuple) else mm(hl))
    probs = []
    for (nlk, lp, _), suf, (_, _, _, _, before, ch) in zip(logs, sufs, items):
        a = jnp.exp2(lp - suf - carry[ch])
        if before is not None:
            a = jnp.where(before, a, 0.0)
        probs.append(a.astype(BF))
        if key_axis == 0:
            carry[ch] = carry[ch] + (suf[0:1, :] + nlk[0:1, :])
        else:
            carry[ch] = carry[ch] + (suf[:, 0:1] + nlk[:, 0:1])
    for a, (_, _, v_op, _, _, ch) in zip(probs, items):
        if key_axis == 0:
            acc[ch] = acc[ch] + jnp.dot(v_op, a, preferred_element_type=F32)
        else:
            acc[ch] = acc[ch] + jnp.dot(a, v_op, preferred_element_type=F32)


def _tri(n, upper):
    r = lax.broadcasted_iota(jnp.int32, (n, n), 0)
    c = lax.broadcasted_iota(jnp.int32, (n, n), 1)
    return jnp.where((c > r) if upper else (r > c), 1.0, 0.0).astype(BF)


def _sb_prompt_kernel(q_ref, k_ref, v_ref, o_ref, vt_ref, carry_ref, acc_ref, *, tq, sub, t, unroll):
    qi = pl.program_id(2)
    n_sub = tq // sub

    @pl.when(qi == 0)
    def _():
        for kb in range(t // sub):
            vt_ref[kb] = v_ref[0, kb * sub:(kb + 1) * sub, :].astype(F32).T.astype(BF)

    u = _tri(sub, True)
    tri2 = jnp.concatenate([u, u], axis=1)
    r_i = lax.broadcasted_iota(jnp.int32, (sub, sub), 0)
    c_i = lax.broadcasted_iota(jnp.int32, (sub, sub), 1)
    before = r_i < c_i
    qs = [q_ref[0, c * sub:(c + 1) * sub, :] for c in range(n_sub)]
    carry = [jnp.zeros((1, sub), F32) for _ in range(n_sub)]
    acc = [jnp.zeros((HEAD_DIM, sub), F32) for _ in range(n_sub)]

    base = qi * n_sub
    items = []
    for j in reversed(range(n_sub)):
        kblk = k_ref[0, pl.ds(pl.multiple_of((base + j) * sub, sub), sub), :]
        vt_blk = vt_ref[base + j]
        items += [(qs[c], kblk, vt_blk, tri2, before if c == j else None, c) for c in range(j, n_sub)]
    _sb_group(items, carry, acc, 0)
    for c in range(n_sub):
        carry_ref[c] = carry[c]
        acc_ref[c] = acc[c]

    def body(i, x):
        cr = [carry_ref[c] for c in range(n_sub)]
        ac = [acc_ref[c] for c in range(n_sub)]
        items = []
        for uu in range(unroll):
            kb = base - 1 - (i * unroll + uu)
            kblk = k_ref[0, pl.ds(pl.multiple_of(kb * sub, sub), sub), :]
            vt_blk = vt_ref[kb]
            items += [(qs[c], kblk, vt_blk, tri2, None, c) for c in range(n_sub)]
        _sb_group(items, cr, ac, 0)
        for c in range(n_sub):
            carry_ref[c] = cr[c]
            acc_ref[c] = ac[c]
        return x

    lax.fori_loop(0, (qi * n_sub) // unroll, body, 0)
    for c in range(n_sub):
        o_ref[c * sub:(c + 1) * sub, :] = acc_ref[c].T.astype(BF)


def _sb_attn_prompt(pb, n_batch, t, tq=1024, sub=256, unroll=2):
    assert t % tq == 0 and tq % sub == 0 and (tq // sub) % unroll == 0
    nq = t // tq
    n_sub = tq // sub
    kern = functools.partial(_sb_prompt_kernel, tq=tq, sub=sub, t=t, unroll=unroll)
    return pl.pallas_call(
        kern,
        grid=(n_batch, N_SB_HEADS, nq),
        in_specs=[pl.BlockSpec((1, tq, HEAD_DIM), lambda b, h, i: (3, b * nq + i, h)),
                  pl.BlockSpec((1, t, HEAD_DIM), lambda b, h, i: (4, b, h)),
                  pl.BlockSpec((1, t, HEAD_DIM), lambda b, h, i: (5, b, h))],
        out_specs=pl.BlockSpec((tq, HEAD_DIM), lambda b, h, i: (b * nq + i, h)),
        out_shape=jax.ShapeDtypeStruct((n_batch * t, SEG), BF),
        scratch_shapes=[pltpu.VMEM((t // sub, HEAD_DIM, sub), BF), pltpu.VMEM((n_sub, 1, sub), F32),
                        pltpu.VMEM((n_sub, HEAD_DIM, sub), F32)],
        compiler_params=_cparams(3),
        name="sb_attn_prompt",
    )(pb, pb, pb)


def _sb_sample_kernel(q_ref, kn_ref, vn_ref, kc_ref, vc_ref, o_ref, *, t_s, past, tk):
    tri_new = _tri(t_s, False)
    l_old = _tri(tk, False)
    tri_old = jnp.concatenate([l_old, l_old], axis=0)
    before = (lax.broadcasted_iota(jnp.int32, (t_s, t_s), 1) < lax.broadcasted_iota(jnp.int32, (t_s, t_s), 0))
    heads = range(N_SB_HEADS)
    cols = [slice(h * HEAD_DIM, (h + 1) * HEAD_DIM) for h in heads]
    qs = [q_ref[0, :, cs] for cs in cols]
    carry = [jnp.zeros((t_s, 1), F32) for _ in heads]
    acc = [jnp.zeros((t_s, HEAD_DIM), F32) for _ in heads]
    _sb_group([(qs[h], kn_ref[0, :, cols[h]], vn_ref[0, :, cols[h]], tri_new, before, h) for h in heads],
              carry, acc, 1)
    for kb in reversed(range(past // tk)):
        rows = slice(kb * tk, (kb + 1) * tk)
        _sb_group([(qs[h], kc_ref[0, rows, cols[h]].astype(BF), vc_ref[0, rows, cols[h]].astype(BF),
                    tri_old, None, h) for h in heads], carry, acc, 1)
    for h in heads:
        o_ref[:, cols[h]] = acc[h].astype(BF)


def _sb_attn_sample(pb, cache_k, cache_v, n_batch, t_s, past, tk=256):
    assert past % tk == 0
    new, old = _sample_attn_specs(t_s, past)
    kern = functools.partial(_sb_sample_kernel, t_s=t_s, past=past, tk=tk)
    return pl.pallas_call(
        kern,
        grid=(n_batch,),
        in_specs=[new(3), new(4), new(5), old, old],
        out_specs=pl.BlockSpec((t_s, SEG), lambda b: (b, 0)),
        out_shape=jax.ShapeDtypeStruct((n_batch * t_s, SEG), BF),
        compiler_params=_cparams(1),
        name="sb_attn_sample",
    )(pb, pb, pb, cache_k, cache_v)


def _cross_kernel(q_ref, k_ref, v_ref, o_ref):
    scale = MEM_HEAD_DIM ** -0.5
    for h in range(N_MEM_HEADS):
        sl = slice(h * MEM_HEAD_DIM, (h + 1) * MEM_HEAD_DIM)
        s = lax.dot_general(q_ref[:, sl], k_ref[:, sl].astype(BF), _NT, preferred_element_type=F32) * scale
        p = jnp.exp(s - jnp.max(s, axis=-1, keepdims=True))
        l = jnp.sum(p, axis=-1, keepdims=True)
        o = jnp.dot(p.astype(BF), v_ref[:, sl].astype(BF), preferred_element_type=F32) / l
        o_ref[:, sl] = o.astype(BF)


def _cross_attn(q_b, memk, memv, n_batch, t_b, tm):
    n_mem = memk.shape[0] // n_batch
    nt = t_b // tm
    return pl.pallas_call(
        _cross_kernel,
        grid=(n_batch, nt),
        in_specs=[pl.BlockSpec((tm, D_MODEL), lambda b, i: (b * nt + i, 0)),
                  pl.BlockSpec((n_mem, D_MODEL), lambda b, i: (b, 0)),
                  pl.BlockSpec((n_mem, D_MODEL), lambda b, i: (b, 0))],
        out_specs=pl.BlockSpec((tm, D_MODEL), lambda b, i: (b * nt + i, 0)),
        out_shape=jax.ShapeDtypeStruct((n_batch * t_b, D_MODEL), BF),
        compiler_params=_cparams(2),
        name="cross_attn",
    )(q_b, memk, memv)


def _gelu_tanh(x):
    return 0.5 * x * (1.0 + jnp.tanh(math.sqrt(2.0 / math.pi) * (x + 0.044715 * (x * x * x))))


def _ffn_kernel(h_ref, wg_ref, wu_ref, wd_ref, wc_ref, bc_ref, st_ref, g_ref, b_ref,
                o_ref, last_ref, hb_ref, carry_ref, gs_ref, *, tm, fc, seq, n_ff):
    i = pl.program_id(0)
    c = pl.program_id(1)

    @pl.when(c == 0)
    def _():
        hb_ref[...] = h_ref[...].astype(BF)

    @pl.when((i == 0) & (c == 0))
    def _():
        carry_ref[...] = jnp.zeros(carry_ref.shape, F32)

    hb = hb_ref[...]
    g = jnp.dot(hb, wg_ref[...], preferred_element_type=F32)
    u = jnp.dot(hb, wu_ref[...], preferred_element_type=F32)
    row = lax.broadcasted_iota(jnp.int32, (tm, fc), 0)
    if seq >= tm:
        tiles_per_seq = seq // tm
        at_start = (i % tiles_per_seq) == 0
        prev = jnp.where(at_start, st_ref[0], carry_ref[c])
        pos = row
        ex0 = prev[0:1]
        ex1 = prev[1:2]
        carry_ref[c] = g[tm - 2:tm, :]
        last_ref[0] = g[tm - 2:tm, :]
    else:
        ns = tm // seq
        pos = jnp.bitwise_and(row, seq - 1)
        ex0 = jnp.broadcast_to(st_ref[0][:, None, :], (ns, seq, fc)).reshape(tm, fc)
        ex1 = jnp.broadcast_to(st_ref[1][:, None, :], (ns, seq, fc)).reshape(tm, fc)
        for k in range(fc // 128):
            lanes = slice(k * 128, (k + 1) * 128)
            gs_ref[k] = g[:, lanes]
            last_ref[0, :, lanes] = gs_ref[k, pl.ds(seq - 2, ns, stride=seq), :]
            last_ref[1, :, lanes] = gs_ref[k, pl.ds(seq - 1, ns, stride=seq), :]
    g_m1 = jnp.where(pos == 0, ex1, pltpu.roll(g, 1, 0))
    g_m2 = jnp.where(pos == 0, ex0, jnp.where(pos == 1, ex1, pltpu.roll(g, 2, 0)))
    conv = bc_ref[...] + g_m2 * wc_ref[0:1, :] + g_m1 * wc_ref[1:2, :] + g * wc_ref[2:3, :]
    a = (_gelu_tanh(conv) * u).astype(BF)
    d = jnp.dot(a, wd_ref[...], preferred_element_type=F32)

    @pl.when(c == 0)
    def _():
        o_ref[...] = d

    @pl.when(c > 0)
    def _():
        o_ref[...] += d

    @pl.when(c == n_ff - 1)
    def _():
        o_ref[...] = _layer_norm(DN_ALPHA * h_ref[...] + o_ref[...], g_ref[...], b_ref[...])


def _ffn(h, w_gate_b, w_up_b, w_down_b, w_conv, b_conv, state, ln_g, ln_b, seq, tm, fc):
    t = h.shape[0]
    n_ff = D_FF // fc
    nt = t // tm
    if seq >= tm:
        tiles_per_seq = seq // tm
        st_spec = pl.BlockSpec((1, 2, fc), lambda i, c: (i // tiles_per_seq, 0, c))
        last_spec = pl.BlockSpec((1, 2, fc), lambda i, c: (i, 0, c))
        last_shape = jax.ShapeDtypeStruct((nt, 2, D_FF), F32)
    else:
        ns = tm // seq
        st_spec = pl.BlockSpec((2, ns, fc), lambda i, c: (0, i, c))
        last_spec = pl.BlockSpec((2, ns, fc), lambda i, c: (0, i, c))
        last_shape = jax.ShapeDtypeStruct((2, t // seq, D_FF), F32)
    kern = functools.partial(_ffn_kernel, tm=tm, fc=fc, seq=seq, n_ff=n_ff)
    return pl.pallas_call(
        kern,
        grid=(nt, n_ff),
        in_specs=[pl.BlockSpec((tm, D_MODEL), lambda i, c: (i, 0)),
                  pl.BlockSpec((D_MODEL, fc), lambda i, c: (0, c)),
                  pl.BlockSpec((D_MODEL, fc), lambda i, c: (0, c)),
                  pl.BlockSpec((fc, D_MODEL), lambda i, c: (c, 0)),
                  pl.BlockSpec((CONV_W, fc), lambda i, c: (0, c)),
                  pl.BlockSpec((1, fc), lambda i, c: (0, c)),
                  st_spec,
                  pl.BlockSpec((1, D_MODEL), lambda i, c: (0, 0)),
                  pl.BlockSpec((1, D_MODEL), lambda i, c: (0, 0))],
        out_specs=[pl.BlockSpec((tm, D_MODEL), lambda i, c: (i, 0)), last_spec],
        out_shape=[jax.ShapeDtypeStruct((t, D_MODEL), F32), last_shape],
        scratch_shapes=[pltpu.VMEM((tm, D_MODEL), BF), pltpu.VMEM((n_ff, 2, fc), F32),
                        pltpu.VMEM((fc // 128, tm, 128), F32)],
        compiler_params=_cparams(2),
        name="conv_ffn",
    )(h, w_gate_b, w_up_b, w_down_b, w_conv, b_conv.reshape(1, D_FF), state,
      ln_g.reshape(1, D_MODEL), ln_b.reshape(1, D_MODEL))


def _after_mixers(x, od, os_, memk, memv, conv_state, n_batch, t_b, w):
    tm = 512
    h1, h1b = _proj_ln([od, os_], w['w_o'], x, w['ln1_g'], w['ln1_b'], tm, "out_proj_ln1")
    (qc,) = _matmul(h1b, w['w_mq'], [BF], tm, 1024, "cross_q")
    oc = _cross_attn(qc, memk, memv, n_batch, t_b, min(t_b, tm))
    h2, _ = _proj_ln([oc], w['w_mo'], h1, w['ln2_g'], w['ln2_b'], tm, "cross_out_ln2")
    return _ffn(h2, w['w_gate'], w['w_up'], w['w_down'], w['w_conv'], w['b_conv'], conv_state,
                w['ln3_g'], w['ln3_b'], t_b, tm, 512)


def kernel(x_prompt, x_sample, mem_prompt, cache_diff_k, cache_diff_v, cache_sb_k, cache_sb_v, cache_mem_k, cache_mem_v, state_ffn_conv, w_in, lambda_q1, lambda_k1, lambda_q2, lambda_k2, subln_g, w_o, ln1_g, ln1_b, w_mq, w_mk, w_mv, w_mo, ln2_g, ln2_b, w_gate, w_up, w_conv, b_conv, w_down, ln3_g, ln3_b):
    assert w_in.shape[0] == DEPTH == 1
    n_p, t_p, _ = x_prompt.shape
    n_s, t_s, _ = x_sample.shape
    past = cache_diff_k.shape[2]
    n_mem = mem_prompt.shape[1]
    l = 0
    lam_init = 0.8 - 0.6 * math.exp(-0.3 * l)
    lam = (jnp.exp(jnp.sum(lambda_q1[l] * lambda_k1[l])) - jnp.exp(jnp.sum(lambda_q2[l] * lambda_k2[l]))
           + lam_init).reshape(1).astype(F32)
    w = {'w_in': w_in[l].astype(BF), 'w_o': w_o[l].astype(BF), 'w_mq': w_mq[l].astype(BF),
         'w_mo': w_mo[l].astype(BF), 'w_gate': w_gate[l].astype(BF), 'w_up': w_up[l].astype(BF),
         'w_down': w_down[l].astype(BF), 'w_conv': w_conv[l], 'b_conv': b_conv[l],
         'subln_g': subln_g[l], 'ln1_g': ln1_g[l], 'ln1_b': ln1_b[l], 'ln2_g': ln2_g[l], 'ln2_b': ln2_b[l],
         'ln3_g': ln3_g[l], 'ln3_b': ln3_b[l]}
    tm = 512

    w_mkv = jnp.concatenate([w_mk[l], w_mv[l]], axis=1).astype(BF)
    mem_b = mem_prompt.reshape(n_p * n_mem, D_MODEL).astype(BF)
    mkv_f, mkv_b = _matmul(mem_b, w_mkv, [F32, BF], n_p * n_mem, 1024, "memory_kv")
    xp = x_prompt.reshape(n_p * t_p, D_MODEL)
    pbp, dkp, dvp, skp, svp = _inproj(xp, w['w_in'], jnp.tile(jnp.arange(t_p), n_p), tm)
    odp = _diff_attn_prompt(lam, pbp, w['subln_g'], n_p, t_p, lam_init)
    osp = _sb_attn_prompt(pbp, n_p, t_p)
    zero_state = jnp.zeros((n_p, CONV_W - 1, D_FF), F32)
    yp, lastp = _after_mixers(xp, odp, osp, mkv_b[:, :D_MODEL], mkv_b[:, D_MODEL:], zero_state, n_p, t_p, w)

    xs = x_sample.reshape(n_s * t_s, D_MODEL)
    pbs, dks, dvs, sks, svs = _inproj(xs, w['w_in'], jnp.tile(past + jnp.arange(t_s), n_s), tm)
    flat = lambda c: c[l].reshape(n_s, past, SEG)
    ods = _diff_attn_sample(lam, pbs, flat(cache_diff_k), flat(cache_diff_v), w['subln_g'], n_s, t_s, past, lam_init)
    oss = _sb_attn_sample(pbs, flat(cache_sb_k), flat(cache_sb_v), n_s, t_s, past)
    state_s = jnp.transpose(state_ffn_conv[l], (1, 0, 2))
    ys, lasts = _after_mixers(xs, ods, oss, cache_mem_k[l].reshape(n_s * n_mem, D_MODEL),
                              cache_mem_v[l].reshape(n_s * n_mem, D_MODEL), state_s, n_s, t_s, w)

    tiles_per_seq = t_p // tm
    new_conv_p = lastp[tiles_per_seq - 1::tiles_per_seq]
    new_conv_s = jnp.transpose(lasts, (1, 0, 2))
    return (yp.reshape(n_p, t_p, D_MODEL), ys.reshape(n_s, t_s, D_MODEL),
            dkp.reshape(1, n_p, t_p, N_DIFF_HEADS, 2, HEAD_DIM), dvp.reshape(1, n_p, t_p, N_DIFF_HEADS, DIFF_V_DIM),
            skp.reshape(1, n_p, t_p, N_SB_HEADS, HEAD_DIM), svp.reshape(1, n_p, t_p, N_SB_HEADS, HEAD_DIM),
            mkv_f[:, :D_MODEL].reshape(1, n_p, n_mem, N_MEM_HEADS, MEM_HEAD_DIM),
            mkv_f[:, D_MODEL:].reshape(1, n_p, n_mem, N_MEM_HEADS, MEM_HEAD_DIM),
            new_conv_p[None],
            dks.reshape(1, n_s, t_s, N_DIFF_HEADS, 2, HEAD_DIM), dvs.reshape(1, n_s, t_s, N_DIFF_HEADS, DIFF_V_DIM),
            sks.reshape(1, n_s, t_s, N_SB_HEADS, HEAD_DIM), svs.reshape(1, n_s, t_s, N_SB_HEADS, HEAD_DIM),
            new_conv_s[None])
```

```python
import functools
import math

import jax
import jax.numpy as jnp
from jax import lax
from jax.experimental import pallas as pl
from jax.experimental.pallas import tpu as pltpu

BF = jnp.bfloat16
F32 = jnp.float32

D_MODEL = 2048
CHUNK = 64
CHUNK_SHIFT = 6
HEAD_DIM = 128
N_DIFF_HEADS = 4
N_SB_HEADS = 8
DIFF_V_DIM = 2 * HEAD_DIM
ROPE_DIM = HEAD_DIM // 4
ROPE_THETA = 500000.0
N_MEM_HEADS = 4
MEM_HEAD_DIM = D_MODEL // N_MEM_HEADS
D_FF = 5632
CONV_W = 3
LN_EPS = 1e-5
DEPTH = 1
DN_ALPHA = (2 * DEPTH) ** 0.25
SEG = 1024
N_SEG = 6
VMEM_LIMIT = 56 * 1024 * 1024
LOG2E = math.log2(math.e)
QK_SCALE2 = HEAD_DIM ** -0.5 * LOG2E
NEG_BIG = -1e30
LN_ROWS = 128
FFN_COLS = 256

_NT = (((1,), (1,)), ((), ()))


def _cparams(n_axes):
    return pltpu.CompilerParams(dimension_semantics=("arbitrary",) * n_axes,
                                vmem_limit_bytes=VMEM_LIMIT)


def _layer_norm(y, g, b):
    mu = jnp.mean(y, axis=-1, keepdims=True)
    yc = y - mu
    var = jnp.mean(yc * yc, axis=-1, keepdims=True)
    return yc * lax.rsqrt(var + LN_EPS) * g + b


def _inproj_rope_kernel(x_ref, w_ref, c_ref, s1_ref, s2_ref, pb_ref, pf_ref, xb_ref):
    @pl.when(pl.program_id(1) == 0)
    def _():
        xb_ref[...] = x_ref[...].astype(BF)

    acc = jnp.dot(xb_ref[...], w_ref[...], preferred_element_type=F32)
    for ch in range(SEG // HEAD_DIM):
        lanes = slice(ch * HEAD_DIM, (ch + 1) * HEAD_DIM)
        xc = acc[:, lanes]
        r = (xc * c_ref[...] + pltpu.roll(xc, HEAD_DIM - ROPE_DIM // 2, 1) * s1_ref[...]
             + pltpu.roll(xc, ROPE_DIM // 2, 1) * s2_ref[...])
        pf_ref[0, :, lanes] = r
        pb_ref[0, :, lanes] = r.astype(BF)


def _inproj_plain_kernel(xb_ref, w_ref, pb_ref, pf_ref):
    acc = jnp.dot(xb_ref[...], w_ref[...], preferred_element_type=F32)
    pf_ref[0] = acc
    pb_ref[0] = acc.astype(BF)


def _rope_tables(pos):
    half = ROPE_DIM // 2
    inv_freq = jnp.power(ROPE_THETA, -jnp.arange(half, dtype=F32) * (2.0 / ROPE_DIM))
    ang = pos.astype(F32)[:, None] * inv_freq[None, :]
    cos, sin = jnp.cos(ang), jnp.sin(ang)
    n = pos.shape[0]
    c = jnp.concatenate([cos, cos, jnp.ones((n, HEAD_DIM - ROPE_DIM), F32)], axis=1)
    s1 = jnp.concatenate([-sin, jnp.zeros((n, HEAD_DIM - half), F32)], axis=1)
    s2 = jnp.concatenate([jnp.zeros((n, half), F32), sin, jnp.zeros((n, HEAD_DIM - ROPE_DIM), F32)], axis=1)
    return c, s1, s2


N_ROPE_SEG = 2


def _inproj(x, w_in_b, pos_rows, tm):
    t = x.shape[0]
    c, s1, s2 = _rope_tables(pos_rows)
    row = lambda i, j: (i, 0)
    seg = lambda i, j: (j, i, 0)
    tab = pl.BlockSpec((tm, HEAD_DIM), row)
    n_pln = N_SEG - N_ROPE_SEG
    rot_b, rot_f, xb = pl.pallas_call(
        _inproj_rope_kernel,
        grid=(t // tm, N_ROPE_SEG),
        in_specs=[pl.BlockSpec((tm, D_MODEL), row),
                  pl.BlockSpec((D_MODEL, SEG), lambda i, j: (0, j)),
                  tab, tab, tab],
        out_specs=[pl.BlockSpec((1, tm, SEG), seg), pl.BlockSpec((1, tm, SEG), seg),
                   pl.BlockSpec((tm, D_MODEL), row)],
        out_shape=[jax.ShapeDtypeStruct((N_ROPE_SEG, t, SEG), BF), jax.ShapeDtypeStruct((N_ROPE_SEG, t, SEG), F32),
                   jax.ShapeDtypeStruct((t, D_MODEL), BF)],
        compiler_params=_cparams(2),
        name="inproj_rope",
    )(x, w_in_b, c, s1, s2)
    pln_b, pln_f = pl.pallas_call(
        _inproj_plain_kernel,
        grid=(t // tm, n_pln),
        in_specs=[pl.BlockSpec((tm, D_MODEL), row),
                  pl.BlockSpec((D_MODEL, SEG), lambda i, j: (0, j + N_ROPE_SEG))],
        out_specs=[pl.BlockSpec((1, tm, SEG), seg), pl.BlockSpec((1, tm, SEG), seg)],
        out_shape=[jax.ShapeDtypeStruct((n_pln, t, SEG), BF), jax.ShapeDtypeStruct((n_pln, t, SEG), F32)],
        compiler_params=_cparams(2),
        name="inproj_plain",
    )(xb, w_in_b)
    return rot_b, rot_f, pln_b, pln_f


def _matmul_kernel(x_ref, w_ref, *o_refs):
    acc = jnp.dot(x_ref[...], w_ref[...], preferred_element_type=F32)
    for o in o_refs:
        o[...] = acc.astype(o.dtype)


def _matmul(x_b, w_b, out_dtypes, tm, tn, name):
    t, k = x_b.shape
    n = w_b.shape[1]
    return pl.pallas_call(
        _matmul_kernel,
        grid=(t // tm, n // tn),
        in_specs=[pl.BlockSpec((tm, k), lambda i, j: (i, 0)),
                  pl.BlockSpec((k, tn), lambda i, j: (0, j))],
        out_specs=[pl.BlockSpec((tm, tn), lambda i, j: (i, j)) for _ in out_dtypes],
        out_shape=[jax.ShapeDtypeStruct((t, n), d) for d in out_dtypes],
        compiler_params=_cparams(2),
        name=name,
    )(x_b, w_b)


def _proj_ln_kernel(*refs, n_in):
    a_refs = refs[:n_in]
    w_refs = refs[n_in:2 * n_in]
    res_ref, g_ref, b_ref, of_ref, ob_ref = refs[2 * n_in:]
    tm = res_ref.shape[0]
    slices = [slice(r, r + LN_ROWS) for r in range(0, tm, LN_ROWS)]
    accs = []
    for rows in slices:
        acc = None
        for a, w in zip(a_refs, w_refs):
            d = jnp.dot(a[rows, :], w[...], preferred_element_type=F32)
            acc = d if acc is None else acc + d
        accs.append(acc)
    for rows, acc in zip(slices, accs):
        out = _layer_norm(DN_ALPHA * res_ref[rows, :] + acc, g_ref[...], b_ref[...])
        of_ref[rows, :] = out
        ob_ref[rows, :] = out.astype(BF)


def _proj_ln(a_list, w_b, res, g, b, tm, name):
    t = res.shape[0]
    n_in = len(a_list)
    ka = a_list[0].shape[1]
    row = lambda i: (i, 0)
    in_specs = ([pl.BlockSpec((tm, ka), row) for _ in a_list]
                + [pl.BlockSpec((ka, D_MODEL), functools.partial(lambda i, r: (r, 0), r=r)) for r in range(n_in)]
                + [pl.BlockSpec((tm, D_MODEL), row),
                   pl.BlockSpec((1, D_MODEL), lambda i: (0, 0)),
                   pl.BlockSpec((1, D_MODEL), lambda i: (0, 0))])
    return pl.pallas_call(
        functools.partial(_proj_ln_kernel, n_in=n_in),
        grid=(t // tm,),
        in_specs=in_specs,
        out_specs=[pl.BlockSpec((tm, D_MODEL), row), pl.BlockSpec((tm, D_MODEL), row)],
        out_shape=[jax.ShapeDtypeStruct((t, D_MODEL), F32), jax.ShapeDtypeStruct((t, D_MODEL), BF)],
        compiler_params=_cparams(1),
        name=name,
    )(*a_list, *([w_b] * n_in), res, g.reshape(1, D_MODEL), b.reshape(1, D_MODEL))


def _chunk_visible(q0, k0, nq, nk):
    qpos = q0 + lax.broadcasted_iota(jnp.int32, (nq, nk), 0)
    kpos = k0 + lax.broadcasted_iota(jnp.int32, (nq, nk), 1)
    return lax.shift_right_logical(kpos, CHUNK_SHIFT) <= lax.shift_right_logical(qpos, CHUNK_SHIFT)


def _subln(o0, o1, lam, g, lam_init):
    of = o0 - lam * o1
    return of * lax.rsqrt(jnp.mean(of * of, axis=-1, keepdims=True) + LN_EPS) * g * (1.0 - lam_init)


def _diff_prompt_kernel(lam_ref, q_ref, k_ref, v_ref, g_ref, o_ref, m_ref, l_ref, acc_ref, *, tq, sub, lam_init):
    qi = pl.program_id(2)
    m_ref[...] = jnp.full(m_ref.shape, NEG_BIG, F32)
    l_ref[...] = jnp.zeros(l_ref.shape, F32)
    acc_ref[...] = jnp.zeros(acc_ref.shape, F32)

    def run(k0, items):
        scores = []
        for c, rows, nk, vis in items:
            cs = slice(c * HEAD_DIM, (c + 1) * HEAD_DIM)
            s = lax.dot_general(q_ref[0, rows, cs], k_ref[0, pl.ds(k0, nk), cs], _NT,
                                preferred_element_type=F32) * QK_SCALE2
            scores.append(s if vis is None else jnp.where(vis, s, -jnp.inf))
        probs = []
        for s, (c, rows, nk, _) in zip(scores, items):
            m_old = m_ref[c, rows]
            m_new = jnp.maximum(m_old, jnp.max(s, axis=-1, keepdims=True))
            alpha = jnp.exp2(m_old - m_new)
            p = jnp.exp2(s - m_new)
            l_ref[c, rows] = alpha * l_ref[c, rows] + jnp.sum(p, axis=-1, keepdims=True)
            m_ref[c, rows] = m_new
            probs.append((alpha, p.astype(BF)))
        for (alpha, p), (c, rows, nk, _) in zip(probs, items):
            acc_ref[c, rows] = alpha * acc_ref[c, rows] + jnp.dot(p, v_ref[0, pl.ds(k0, nk), :],
                                                                  preferred_element_type=F32)

    half = tq // 2
    full_items = [(c, slice(r * half, (r + 1) * half), tq, None) for r in range(2) for c in range(2)]

    def body(g, carry):
        run(pl.multiple_of(g * tq, tq), full_items)
        return carry

    lax.fori_loop(0, qi, body, 0)

    diag_items = [(c, slice(r * sub, (r + 1) * sub), (r + 1) * sub, _chunk_visible(r * sub, 0, sub, (r + 1) * sub))
                  for r in range(tq // sub) for c in range(2)]
    run(pl.multiple_of(qi * tq, tq), diag_items)

    o = _subln(acc_ref[0] / l_ref[0], acc_ref[1] / l_ref[1], lam_ref[0], g_ref[...], lam_init)
    o_ref[...] = o.astype(BF)


def _diff_attn_prompt(lam, rot_b, pln_b, subln_g, n_batch, t, lam_init, tq=1024, sub=256):
    assert t % tq == 0 and tq % sub == 0 and sub % CHUNK == 0
    nq = t // tq
    kern = functools.partial(_diff_prompt_kernel, tq=tq, sub=sub, lam_init=lam_init)
    return pl.pallas_call(
        kern,
        grid=(n_batch, N_DIFF_HEADS, nq),
        in_specs=[pl.BlockSpec(memory_space=pltpu.SMEM),
                  pl.BlockSpec((1, tq, DIFF_V_DIM), lambda b, h, i: (0, b * nq + i, h)),
                  pl.BlockSpec((1, t, DIFF_V_DIM), lambda b, h, i: (1, b, h)),
                  pl.BlockSpec((1, t, DIFF_V_DIM), lambda b, h, i: (0, b, h)),
                  pl.BlockSpec((1, DIFF_V_DIM), lambda b, h, i: (0, 0))],
        out_specs=pl.BlockSpec((tq, DIFF_V_DIM), lambda b, h, i: (b * nq + i, h)),
        out_shape=jax.ShapeDtypeStruct((n_batch * t, SEG), BF),
        scratch_shapes=[pltpu.VMEM((2, tq, 1), F32), pltpu.VMEM((2, tq, 1), F32),
                        pltpu.VMEM((2, tq, DIFF_V_DIM), F32)],
        compiler_params=_cparams(3),
        name="diff_attn_prompt",
    )(lam, rot_b, rot_b, pln_b, subln_g.reshape(1, DIFF_V_DIM))


def _diff_sample_kernel(lam_ref, q_ref, kn_ref, vn_ref, kc_ref, vc_ref, g_ref, o_ref, *, t_s, past, lam_init):
    vis = _chunk_visible(past, past, t_s, t_s)
    for h in range(N_DIFF_HEADS):
        vs = slice(h * DIFF_V_DIM, (h + 1) * DIFF_V_DIM)
        v_old = _lane_rows(vc_ref, 0, past, h, DIFF_V_DIM, N_DIFF_HEADS * DIFF_V_DIM).astype(BF)
        v_new = vn_ref[0, :, vs]
        o = []
        for c in range(2):
            cs = slice((2 * h + c) * HEAD_DIM, (2 * h + c + 1) * HEAD_DIM)
            q = q_ref[0, :, cs]
            k_old = _lane_rows(kc_ref, 0, past, 2 * h + c, HEAD_DIM, SEG).astype(BF)
            s_old = lax.dot_general(q, k_old, _NT, preferred_element_type=F32) * QK_SCALE2
            s_new = lax.dot_general(q, kn_ref[0, :, cs], _NT, preferred_element_type=F32) * QK_SCALE2
            s_new = jnp.where(vis, s_new, -jnp.inf)
            m = jnp.maximum(jnp.max(s_old, axis=-1, keepdims=True), jnp.max(s_new, axis=-1, keepdims=True))
            p_old = jnp.exp2(s_old - m)
            p_new = jnp.exp2(s_new - m)
            l = jnp.sum(p_old, axis=-1, keepdims=True) + jnp.sum(p_new, axis=-1, keepdims=True)
            acc = (jnp.dot(p_old.astype(BF), v_old, preferred_element_type=F32)
                   + jnp.dot(p_new.astype(BF), v_new, preferred_element_type=F32))
            o.append(acc / l)
        o_ref[:, vs] = _subln(o[0], o[1], lam_ref[0], g_ref[...], lam_init).astype(BF)


def _new_rows_spec(t_s, seg):
    return pl.BlockSpec((1, t_s, SEG), lambda b: (seg, b, 0))


def _as_lane_rows(x, n_batch):
    return x.reshape(n_batch, -1, HEAD_DIM)


def _lane_rows(ref, t0, nt, slot, slot_width, width):
    per_row = width // HEAD_DIM
    per_slot = slot_width // HEAD_DIM
    parts = [ref[0, pl.ds(t0 * per_row + slot * per_slot + k, nt, stride=per_row), :] for k in range(per_slot)]
    return parts[0] if per_slot == 1 else jnp.concatenate(parts, axis=1)


def _cache_spec(cache):
    return pl.BlockSpec((1,) + cache.shape[1:], lambda b: (b, 0, 0))


def _diff_attn_sample(lam, rot_b, pln_b, cache_k, cache_v, subln_g, n_batch, t_s, past, lam_init):
    kern = functools.partial(_diff_sample_kernel, t_s=t_s, past=past, lam_init=lam_init)
    return pl.pallas_call(
        kern,
        grid=(n_batch,),
        in_specs=[pl.BlockSpec(memory_space=pltpu.SMEM), _new_rows_spec(t_s, 0), _new_rows_spec(t_s, 1),
                  _new_rows_spec(t_s, 0), _cache_spec(cache_k), _cache_spec(cache_v),
                  pl.BlockSpec((1, DIFF_V_DIM), lambda b: (0, 0))],
        out_specs=pl.BlockSpec((t_s, SEG), lambda b: (b, 0)),
        out_shape=jax.ShapeDtypeStruct((n_batch * t_s, SEG), BF),
        compiler_params=_cparams(1),
        name="diff_attn_sample",
    )(lam, rot_b, rot_b, pln_b, cache_k, cache_v, subln_g.reshape(1, DIFF_V_DIM))


def _sb_logs(zraw):
    z2 = zraw * QK_SCALE2
    mx = jnp.maximum(z2, 0.0)
    mn = jnp.minimum(z2, 0.0)
    t = jnp.log(1.0 + jnp.exp2(mn - mx)) * LOG2E
    return mx + t, mn - t


def _split_bf16(x):
    hi = x.astype(BF)
    return hi, (x - hi.astype(F32)).astype(BF)


def _sb_group(items, carry, acc, key_axis):
    if key_axis == 0:
        zs = [lax.dot_general(k, q, _NT, preferred_element_type=F32) for q, k, _, _, _, _ in items]
    else:
        zs = [lax.dot_general(q, k, _NT, preferred_element_type=F32) for q, k, _, _, _, _ in items]
    logs = []
    for z, (_, _, _, tri, before, _) in zip(zs, items):
        nlk, lp = _sb_logs(z)
        if before is not None:
            nlk = jnp.where(before, nlk, 0.0)
        hi, lo = _split_bf16(nlk)
        fused = tri.shape[1 - key_axis] == 2 * tri.shape[key_axis]
        logs.append((nlk, lp, jnp.concatenate([hi, lo], axis=key_axis) if fused else (hi, lo)))
    sufs = []
    for (_, _, hl), (_, _, _, tri, _, _) in zip(logs, items):
        if key_axis == 0:
            mm = lambda x: jnp.dot(tri, x, preferred_element_type=F32)
        else:
            mm = lambda x: jnp.dot(x, tri, preferred_element_type=F32)
        sufs.append(mm(hl[0]) + mm(hl[1]) if isinstance(hl, tuple) else mm(hl))
    probs = []
    for (nlk, lp, _), suf, (_, _, _, _, before, ch) in zip(logs, sufs, items):
        a = jnp.exp2(lp - suf - carry[ch])
        if before is not None:
            a = jnp.where(before, a, 0.0)
        probs.append(a.astype(BF))
        if key_axis == 0:
            carry[ch] = carry[ch] + (suf[0:1, :] + nlk[0:1, :])
        else:
            carry[ch] = carry[ch] + (suf[:, 0:1] + nlk[:, 0:1])
    for a, (_, _, v_op, _, _, ch) in zip(probs, items):
        if key_axis == 0:
            acc[ch] = acc[ch] + jnp.dot(v_op, a, preferred_element_type=F32)
        else:
            acc[ch] = acc[ch] + jnp.dot(a, v_op, preferred_element_type=F32)


def _tri(n, upper):
    r = lax.broadcasted_iota(jnp.int32, (n, n), 0)
    c = lax.broadcasted_iota(jnp.int32, (n, n), 1)
    return jnp.where((c > r) if upper else (r > c), 1.0, 0.0).astype(BF)


def _sb_prompt_kernel(q_ref, k_ref, v_ref, o_ref, vt_ref, carry_ref, acc_ref, *, tq, sub, t, unroll):
    qi = pl.program_id(2)
    n_sub = tq // sub

    @pl.when(qi == 0)
    def _():
        for kb in range(t // sub):
            vt_ref[kb] = v_ref[0, kb * sub:(kb + 1) * sub, :].astype(F32).T.astype(BF)

    u = _tri(sub, True)
    tri2 = jnp.concatenate([u, u], axis=1)
    r_i = lax.broadcasted_iota(jnp.int32, (sub, sub), 0)
    c_i = lax.broadcasted_iota(jnp.int32, (sub, sub), 1)
    before = r_i < c_i
    qs = [q_ref[0, c * sub:(c + 1) * sub, :] for c in range(n_sub)]
    carry = [jnp.zeros((1, sub), F32) for _ in range(n_sub)]
    acc = [jnp.zeros((HEAD_DIM, sub), F32) for _ in range(n_sub)]

    base = qi * n_sub
    items = []
    for j in reversed(range(n_sub)):
        kblk = k_ref[0, pl.ds(pl.multiple_of((base + j) * sub, sub), sub), :]
        vt_blk = vt_ref[base + j]
        items += [(qs[c], kblk, vt_blk, tri2, before if c == j else None, c) for c in range(j, n_sub)]
    _sb_group(items, carry, acc, 0)
    for c in range(n_sub):
        carry_ref[c] = carry[c]
        acc_ref[c] = acc[c]

    def body(i, x):
        cr = [carry_ref[c] for c in range(n_sub)]
        ac = [acc_ref[c] for c in range(n_sub)]
        items = []
        for uu in range(unroll):
            kb = base - 1 - (i * unroll + uu)
            kblk = k_ref[0, pl.ds(pl.multiple_of(kb * sub, sub), sub), :]
            vt_blk = vt_ref[kb]
            items += [(qs[c], kblk, vt_blk, tri2, None, c) for c in range(n_sub)]
        _sb_group(items, cr, ac, 0)
        for c in range(n_sub):
            carry_ref[c] = cr[c]
            acc_ref[c] = ac[c]
        return x

    lax.fori_loop(0, (qi * n_sub) // unroll, body, 0)
    for c in range(n_sub):
        o_ref[c * sub:(c + 1) * sub, :] = acc_ref[c].T.astype(BF)


def _sb_attn_prompt(pln_b, n_batch, t, tq=1024, sub=256, unroll=2):
    assert t % tq == 0 and tq % sub == 0 and (tq // sub) % unroll == 0
    nq = t // tq
    n_sub = tq // sub
    kern = functools.partial(_sb_prompt_kernel, tq=tq, sub=sub, t=t, unroll=unroll)
    return pl.pallas_call(
        kern,
        grid=(n_batch, N_SB_HEADS, nq),
        in_specs=[pl.BlockSpec((1, tq, HEAD_DIM), lambda b, h, i: (1, b * nq + i, h)),
                  pl.BlockSpec((1, t, HEAD_DIM), lambda b, h, i: (2, b, h)),
                  pl.BlockSpec((1, t, HEAD_DIM), lambda b, h, i: (3, b, h))],
        out_specs=pl.BlockSpec((tq, HEAD_DIM), lambda b, h, i: (b * nq + i, h)),
        out_shape=jax.ShapeDtypeStruct((n_batch * t, SEG), BF),
        scratch_shapes=[pltpu.VMEM((t // sub, HEAD_DIM, sub), BF), pltpu.VMEM((n_sub, 1, sub), F32),
                        pltpu.VMEM((n_sub, HEAD_DIM, sub), F32)],
        compiler_params=_cparams(3),
        name="sb_attn_prompt",
    )(pln_b, pln_b, pln_b)


def _sb_sample_kernel(q_ref, kn_ref, vn_ref, kc_ref, vc_ref, o_ref, *, t_s, past, tk):
    tri_new = _tri(t_s, False)
    l_old = _tri(tk, False)
    tri_old = jnp.concatenate([l_old, l_old], axis=0)
    before = (lax.broadcasted_iota(jnp.int32, (t_s, t_s), 1) < lax.broadcasted_iota(jnp.int32, (t_s, t_s), 0))
    heads = range(N_SB_HEADS)
    cols = [slice(h * HEAD_DIM, (h + 1) * HEAD_DIM) for h in heads]
    qs = [q_ref[0, :, cs] for cs in cols]
    carry = [jnp.zeros((t_s, 1), F32) for _ in heads]
    acc = [jnp.zeros((t_s, HEAD_DIM), F32) for _ in heads]
    _sb_group([(qs[h], kn_ref[0, :, cols[h]], vn_ref[0, :, cols[h]], tri_new, before, h) for h in heads],
              carry, acc, 1)
    for kb in reversed(range(past // tk)):
        rows = slice(kb * tk, (kb + 1) * tk)
        _sb_group([(qs[h], _lane_rows(kc_ref, kb * tk, tk, h, HEAD_DIM, SEG).astype(BF),
                    _lane_rows(vc_ref, kb * tk, tk, h, HEAD_DIM, SEG).astype(BF), tri_old, None, h)
                   for h in heads], carry, acc, 1)
    for h in heads:
        o_ref[:, cols[h]] = acc[h].astype(BF)


def _sb_attn_sample(pln_b, cache_k, cache_v, n_batch, t_s, past, tk=256):
    assert past % tk == 0
    kern = functools.partial(_sb_sample_kernel, t_s=t_s, past=past, tk=tk)
    return pl.pallas_call(
        kern,
        grid=(n_batch,),
        in_specs=[_new_rows_spec(t_s, 1), _new_rows_spec(t_s, 2), _new_rows_spec(t_s, 3),
                  _cache_spec(cache_k), _cache_spec(cache_v)],
        out_specs=pl.BlockSpec((t_s, SEG), lambda b: (b, 0)),
        out_shape=jax.ShapeDtypeStruct((n_batch * t_s, SEG), BF),
        compiler_params=_cparams(1),
        name="sb_attn_sample",
    )(pln_b, pln_b, pln_b, cache_k, cache_v)


def _cross_kernel(q_ref, k_ref, v_ref, o_ref):
    scale = MEM_HEAD_DIM ** -0.5
    n_mem = k_ref.shape[1] * HEAD_DIM // D_MODEL
    for h in range(N_MEM_HEADS):
        sl = slice(h * MEM_HEAD_DIM, (h + 1) * MEM_HEAD_DIM)
        k = _lane_rows(k_ref, 0, n_mem, h, MEM_HEAD_DIM, D_MODEL).astype(BF)
        v = _lane_rows(v_ref, 0, n_mem, h, MEM_HEAD_DIM, D_MODEL).astype(BF)
        s = lax.dot_general(q_ref[:, sl], k, _NT, preferred_element_type=F32) * scale
        p = jnp.exp(s - jnp.max(s, axis=-1, keepdims=True))
        l = jnp.sum(p, axis=-1, keepdims=True)
        o = jnp.dot(p.astype(BF), v, preferred_element_type=F32) / l
        o_ref[:, sl] = o.astype(BF)


def _cross_attn(q_b, memk, memv, n_batch, t_b, tm):
    nt = t_b // tm
    mem_spec = pl.BlockSpec((1,) + memk.shape[1:], lambda b, i: (b, 0, 0))
    return pl.pallas_call(
        _cross_kernel,
        grid=(n_batch, nt),
        in_specs=[pl.BlockSpec((tm, D_MODEL), lambda b, i: (b * nt + i, 0)), mem_spec, mem_spec],
        out_specs=pl.BlockSpec((tm, D_MODEL), lambda b, i: (b * nt + i, 0)),
        out_shape=jax.ShapeDtypeStruct((n_batch * t_b, D_MODEL), BF),
        compiler_params=_cparams(2),
        name="cross_attn",
    )(q_b, memk, memv)


def _gelu_tanh(x):
    return 0.5 * x * (1.0 + jnp.tanh(math.sqrt(2.0 / math.pi) * (x + 0.044715 * (x * x * x))))


def _ffn_kernel(h_ref, wg_ref, wu_ref, wd_ref, wc_ref, bc_ref, st_ref, g_ref, b_ref,
                o_ref, last_ref, hb_ref, carry_ref, gs_ref, *, tm, fc, seq, n_ff):
    i = pl.program_id(0)
    c = pl.program_id(1)

    @pl.when(c == 0)
    def _():
        hb_ref[...] = h_ref[...].astype(BF)
        o_ref[...] = jnp.zeros(o_ref.shape, F32)

    @pl.when((i == 0) & (c == 0))
    def _():
        carry_ref[...] = jnp.zeros(carry_ref.shape, F32)

    hb = hb_ref[...]
    halves = [slice(k, k + FFN_COLS) for k in range(0, fc, FFN_COLS)]
    gs = [jnp.dot(hb, wg_ref[:, cols], preferred_element_type=F32) for cols in halves]
    us = [jnp.dot(hb, wu_ref[:, cols], preferred_element_type=F32) for cols in halves]
    row = lax.broadcasted_iota(jnp.int32, (tm, FFN_COLS), 0)
    acts = []
    for g, u, cols in zip(gs, us, halves):
        if seq >= tm:
            tiles_per_seq = seq // tm
            at_start = (i % tiles_per_seq) == 0
            prev = jnp.where(at_start, st_ref[0, :, cols], carry_ref[c, :, cols])
            pos = row
            ex0 = prev[0:1]
            ex1 = prev[1:2]
            carry_ref[c, :, cols] = g[tm - 2:tm, :]
            last_ref[0, :, cols] = g[tm - 2:tm, :]
        else:
            ns = tm // seq
            pos = jnp.bitwise_and(row, seq - 1)
            ex0 = jnp.broadcast_to(st_ref[0, :, cols][:, None, :], (ns, seq, FFN_COLS)).reshape(tm, FFN_COLS)
            ex1 = jnp.broadcast_to(st_ref[1, :, cols][:, None, :], (ns, seq, FFN_COLS)).reshape(tm, FFN_COLS)
            for k in range(FFN_COLS // 128):
                lanes = slice(cols.start + k * 128, cols.start + (k + 1) * 128)
                gs_ref[k] = g[:, k * 128:(k + 1) * 128]
                last_ref[0, :, lanes] = gs_ref[k, pl.ds(seq - 2, ns, stride=seq), :]
                last_ref[1, :, lanes] = gs_ref[k, pl.ds(seq - 1, ns, stride=seq), :]
        g_m1 = jnp.where(pos == 0, ex1, pltpu.roll(g, 1, 0))
        g_m2 = jnp.where(pos == 0, ex0, jnp.where(pos == 1, ex1, pltpu.roll(g, 2, 0)))
        conv = (bc_ref[:, cols] + g_m2 * wc_ref[0:1, cols] + g_m1 * wc_ref[1:2, cols] + g * wc_ref[2:3, cols])
        acts.append((_gelu_tanh(conv) * u).astype(BF))
    d = None
    for a, cols in zip(acts, halves):
        dd = jnp.dot(a, wd_ref[cols, :], preferred_element_type=F32)
        d = dd if d is None else d + dd
    o_ref[...] += d

    @pl.when(c == n_ff - 1)
    def _():
        o_ref[...] = _layer_norm(DN_ALPHA * h_ref[...] + o_ref[...], g_ref[...], b_ref[...])


def _ffn(h, w_gate_b, w_up_b, w_down_b, w_conv, b_conv, state, ln_g, ln_b, seq, tm, fc):
    t = h.shape[0]
    n_ff = D_FF // fc
    nt = t // tm
    if seq >= tm:
        tiles_per_seq = seq // tm
        st_spec = pl.BlockSpec((1, 2, fc), lambda i, c: (i // tiles_per_seq, 0, c))
        last_spec = pl.BlockSpec((1, 2, fc), lambda i, c: (i, 0, c))
        last_shape = jax.ShapeDtypeStruct((nt, 2, D_FF), F32)
    else:
        ns = tm // seq
        st_spec = pl.BlockSpec((2, ns, fc), lambda i, c: (0, i, c))
        last_spec = pl.BlockSpec((2, ns, fc), lambda i, c: (0, i, c))
        last_shape = jax.ShapeDtypeStruct((2, t // seq, D_FF), F32)
    kern = functools.partial(_ffn_kernel, tm=tm, fc=fc, seq=seq, n_ff=n_ff)
    return pl.pallas_call(
        kern,
        grid=(nt, n_ff),
        in_specs=[pl.BlockSpec((tm, D_MODEL), lambda i, c: (i, 0)),
                  pl.BlockSpec((D_MODEL, fc), lambda i, c: (0, c)),
                  pl.BlockSpec((D_MODEL, fc), lambda i, c: (0, c)),
                  pl.BlockSpec((fc, D_MODEL), lambda i, c: (c, 0)),
                  pl.BlockSpec((CONV_W, fc), lambda i, c: (0, c)),
                  pl.BlockSpec((1, fc), lambda i, c: (0, c)),
                  st_spec,
                  pl.BlockSpec((1, D_MODEL), lambda i, c: (0, 0)),
                  pl.BlockSpec((1, D_MODEL), lambda i, c: (0, 0))],
        out_specs=[pl.BlockSpec((tm, D_MODEL), lambda i, c: (i, 0)), last_spec],
        out_shape=[jax.ShapeDtypeStruct((t, D_MODEL), F32), last_shape],
        scratch_shapes=[pltpu.VMEM((tm, D_MODEL), BF), pltpu.VMEM((n_ff, 2, fc), F32),
                        pltpu.VMEM((fc // 128, tm, 128), F32)],
        compiler_params=_cparams(2),
        name="conv_ffn",
    )(h, w_gate_b, w_up_b, w_down_b, w_conv, b_conv.reshape(1, D_FF), state,
      ln_g.reshape(1, D_MODEL), ln_b.reshape(1, D_MODEL))


def _after_mixers(x, od, os_, memk, memv, conv_state, n_batch, t_b, w):
    tm = 512
    h1, h1b = _proj_ln([od, os_], w['w_o'], x, w['ln1_g'], w['ln1_b'], tm, "out_proj_ln1")
    (qc,) = _matmul(h1b, w['w_mq'], [BF], tm, 1024, "cross_q")
    oc = _cross_attn(qc, memk, memv, n_batch, t_b, min(t_b, tm))
    h2, _ = _proj_ln([oc], w['w_mo'], h1, w['ln2_g'], w['ln2_b'], tm, "cross_out_ln2")
    return _ffn(h2, w['w_gate'], w['w_up'], w['w_down'], w['w_conv'], w['b_conv'], conv_state,
                w['ln3_g'], w['ln3_b'], t_b, tm, 512)


def kernel(x_prompt, x_sample, mem_prompt, cache_diff_k, cache_diff_v, cache_sb_k, cache_sb_v, cache_mem_k, cache_mem_v, state_ffn_conv, w_in, lambda_q1, lambda_k1, lambda_q2, lambda_k2, subln_g, w_o, ln1_g, ln1_b, w_mq, w_mk, w_mv, w_mo, ln2_g, ln2_b, w_gate, w_up, w_conv, b_conv, w_down, ln3_g, ln3_b):
    assert w_in.shape[0] == DEPTH == 1
    n_p, t_p, _ = x_prompt.shape
    n_s, t_s, _ = x_sample.shape
    past = cache_diff_k.shape[2]
    n_mem = mem_prompt.shape[1]
    l = 0
    lam_init = 0.8 - 0.6 * math.exp(-0.3 * l)
    lam = (jnp.exp(jnp.sum(lambda_q1[l] * lambda_k1[l])) - jnp.exp(jnp.sum(lambda_q2[l] * lambda_k2[l]))
           + lam_init).reshape(1).astype(F32)
    w = {'w_in': w_in[l].astype(BF), 'w_o': w_o[l].astype(BF), 'w_mq': w_mq[l].astype(BF),
         'w_mo': w_mo[l].astype(BF), 'w_gate': w_gate[l].astype(BF), 'w_up': w_up[l].astype(BF),
         'w_down': w_down[l].astype(BF), 'w_conv': w_conv[l], 'b_conv': b_conv[l],
         'subln_g': subln_g[l], 'ln1_g': ln1_g[l], 'ln1_b': ln1_b[l], 'ln2_g': ln2_g[l], 'ln2_b': ln2_b[l],
         'ln3_g': ln3_g[l], 'ln3_b': ln3_b[l]}
    tm = 512

    w_mkv = jnp.concatenate([w_mk[l], w_mv[l]], axis=1).astype(BF)
    mem_b = mem_prompt.reshape(n_p * n_mem, D_MODEL).astype(BF)
    (mkv_f,) = _matmul(mem_b, w_mkv, [F32], n_p * n_mem, 1024, "memory_kv")
    xp = x_prompt.reshape(n_p * t_p, D_MODEL)
    rot_bp, rot_fp, pln_bp, pln_fp = _inproj(xp, w['w_in'], jnp.tile(jnp.arange(t_p), n_p), tm)
    odp = _diff_attn_prompt(lam, rot_bp, pln_bp, w['subln_g'], n_p, t_p, lam_init)
    osp = _sb_attn_prompt(pln_bp, n_p, t_p)
    zero_state = jnp.zeros((n_p, CONV_W - 1, D_FF), F32)
    mem_heads = (n_p, n_mem, N_MEM_HEADS, MEM_HEAD_DIM)
    yp, lastp = _after_mixers(xp, odp, osp, _as_lane_rows(mkv_f[:, :D_MODEL], n_p),
                              _as_lane_rows(mkv_f[:, D_MODEL:], n_p), zero_state, n_p, t_p, w)

    xs = x_sample.reshape(n_s * t_s, D_MODEL)
    rot_bs, rot_fs, pln_bs, pln_fs = _inproj(xs, w['w_in'], jnp.tile(past + jnp.arange(t_s), n_s), tm)
    rows = lambda c: _as_lane_rows(c[l], n_s)
    ods = _diff_attn_sample(lam, rot_bs, pln_bs, rows(cache_diff_k), rows(cache_diff_v), w['subln_g'],
                            n_s, t_s, past, lam_init)
    oss = _sb_attn_sample(pln_bs, rows(cache_sb_k), rows(cache_sb_v), n_s, t_s, past)
    state_s = jnp.transpose(state_ffn_conv[l], (1, 0, 2))
    ys, lasts = _after_mixers(xs, ods, oss, rows(cache_mem_k), rows(cache_mem_v), state_s, n_s, t_s, w)

    tiles_per_seq = t_p // tm
    new_conv_p = lastp[tiles_per_seq - 1::tiles_per_seq]
    new_conv_s = jnp.transpose(lasts, (1, 0, 2))

    def new_rows(rot_f, pln_f, n, t):
        return (rot_f[1].reshape(1, n, t, N_DIFF_HEADS, 2, HEAD_DIM), pln_f[0].reshape(1, n, t, N_DIFF_HEADS, DIFF_V_DIM),
                pln_f[2].reshape(1, n, t, N_SB_HEADS, HEAD_DIM), pln_f[3].reshape(1, n, t, N_SB_HEADS, HEAD_DIM))

    return ((yp.reshape(n_p, t_p, D_MODEL), ys.reshape(n_s, t_s, D_MODEL))
            + new_rows(rot_fp, pln_fp, n_p, t_p)
            + (mkv_f[:, :D_MODEL].reshape((1,) + mem_heads), mkv_f[:, D_MODEL:].reshape((1,) + mem_heads),
               new_conv_p[None])
            + new_rows(rot_fs, pln_fs, n_s, t_s)
            + (new_conv_s[None],))
```

```python
import functools
import math

import jax
import jax.numpy as jnp
from jax import lax
from jax.experimental import pallas as pl
from jax.experimental.pallas import tpu as pltpu

BF = jnp.bfloat16
F32 = jnp.float32

D_MODEL = 2048
CHUNK = 64
CHUNK_SHIFT = 6
HEAD_DIM = 128
N_DIFF_HEADS = 4
N_SB_HEADS = 8
DIFF_V_DIM = 2 * HEAD_DIM
ROPE_DIM = HEAD_DIM // 4
ROPE_THETA = 500000.0
N_MEM_HEADS = 4
MEM_HEAD_DIM = D_MODEL // N_MEM_HEADS
D_FF = 5632
CONV_W = 3
LN_EPS = 1e-5
DEPTH = 1
DN_ALPHA = (2 * DEPTH) ** 0.25
SEG = 1024
N_SEG = 6
VMEM_LIMIT = 56 * 1024 * 1024
LOG2E = math.log2(math.e)
QK_SCALE2 = HEAD_DIM ** -0.5 * LOG2E
NEG_BIG = -1e30
LN_ROWS = 128
FFN_COLS = 256
FFN_ROWS = 1024
FFN_CHUNK = 512

_NT = (((1,), (1,)), ((), ()))


def _cparams(n_axes):
    return pltpu.CompilerParams(dimension_semantics=("arbitrary",) * n_axes,
                                vmem_limit_bytes=VMEM_LIMIT)


def _layer_norm(y, g, b):
    mu = jnp.mean(y, axis=-1, keepdims=True)
    yc = y - mu
    var = jnp.mean(yc * yc, axis=-1, keepdims=True)
    return yc * lax.rsqrt(var + LN_EPS) * g + b


SLOTS = SEG // HEAD_DIM


def _inproj_kernel(*refs, rope, want_f32, tile_rows):
    xb_ref, w_ref = refs[:2]
    tabs = refs[2:5] if rope else None
    outs = refs[5:] if rope else refs[2:]
    pb_ref = outs[0]
    acc = jnp.dot(xb_ref[...], w_ref[...], preferred_element_type=F32)
    tm = acc.shape[0]
    for ch in range(SLOTS):
        lanes = slice(ch * HEAD_DIM, (ch + 1) * HEAD_DIM)
        r = acc[:, lanes]
        if rope:
            c_ref, s1_ref, s2_ref = tabs
            r = (r * c_ref[...] + pltpu.roll(r, HEAD_DIM - ROPE_DIM // 2, 1) * s1_ref[...]
                 + pltpu.roll(r, ROPE_DIM // 2, 1) * s2_ref[...])
        pb_ref[0, :, lanes] = r.astype(BF)
        if want_f32:
            head, col_tile = divmod(ch, SLOTS // tile_rows) if tile_rows else (0, 0)
            off = col_tile * tile_rows + head if tile_rows else ch
            outs[1][pl.ds(off, tm, stride=SLOTS), :] = r


def _rope_tables(pos):
    half = ROPE_DIM // 2
    inv_freq = jnp.power(ROPE_THETA, -jnp.arange(half, dtype=F32) * (2.0 / ROPE_DIM))
    ang = pos.astype(F32)[:, None] * inv_freq[None, :]
    cos, sin = jnp.cos(ang), jnp.sin(ang)
    n = pos.shape[0]
    c = jnp.concatenate([cos, cos, jnp.ones((n, HEAD_DIM - ROPE_DIM), F32)], axis=1)
    s1 = jnp.concatenate([-sin, jnp.zeros((n, HEAD_DIM - half), F32)], axis=1)
    s2 = jnp.concatenate([jnp.zeros((n, half), F32), sin, jnp.zeros((n, HEAD_DIM - ROPE_DIM), F32)], axis=1)
    return c, s1, s2


IN_GROUPS = (("dq", True, False, 0), ("dk", True, True, 0), ("dv", False, True, N_DIFF_HEADS),
             ("sq", False, False, 0), ("sk", False, True, 0), ("sv", False, True, 0))


def _inproj(x, w_in_b, pos_rows, tm):
    t = x.shape[0]
    xb = x.astype(BF)
    tabs = _rope_tables(pos_rows)
    row = lambda i: (i, 0)
    tab = pl.BlockSpec((tm, HEAD_DIM), row)
    bf, f32 = {}, {}
    for j, (name, rope, want_f32, tile_rows) in enumerate(IN_GROUPS):
        outs = pl.pallas_call(
            functools.partial(_inproj_kernel, rope=rope, want_f32=want_f32, tile_rows=tile_rows),
            grid=(t // tm,),
            in_specs=[pl.BlockSpec((tm, D_MODEL), row),
                      pl.BlockSpec((D_MODEL, SEG), functools.partial(lambda i, j: (0, j), j=j))]
                     + ([tab, tab, tab] if rope else []),
            out_specs=[pl.BlockSpec((1, tm, SEG), lambda i: (0, i, 0))]
                      + ([pl.BlockSpec((tm * SLOTS, HEAD_DIM), row)] if want_f32 else []),
            out_shape=[jax.ShapeDtypeStruct((1, t, SEG), BF)]
                      + ([jax.ShapeDtypeStruct((t * SLOTS, HEAD_DIM), F32)] if want_f32 else []),
            compiler_params=_cparams(1),
            name="inproj_" + name,
        )(xb, w_in_b, *(tabs if rope else ()))
        bf[name] = outs[0]
        if want_f32:
            f32[name] = outs[1]
    return bf, f32


def _matmul_kernel(x_ref, w_ref, *o_refs):
    acc = jnp.dot(x_ref[...], w_ref[...], preferred_element_type=F32)
    for o in o_refs:
        o[...] = acc.astype(o.dtype)


def _matmul(x_b, w_b, out_dtypes, tm, tn, name):
    t, k = x_b.shape
    n = w_b.shape[1]
    return pl.pallas_call(
        _matmul_kernel,
        grid=(t // tm, n // tn),
        in_specs=[pl.BlockSpec((tm, k), lambda i, j: (i, 0)),
                  pl.BlockSpec((k, tn), lambda i, j: (0, j))],
        out_specs=[pl.BlockSpec((tm, tn), lambda i, j: (i, j)) for _ in out_dtypes],
        out_shape=[jax.ShapeDtypeStruct((t, n), d) for d in out_dtypes],
        compiler_params=_cparams(2),
        name=name,
    )(x_b, w_b)


def _proj_ln_kernel(*refs, n_in):
    a_refs = refs[:n_in]
    w_refs = refs[n_in:2 * n_in]
    res_ref, g_ref, b_ref, of_ref, ob_ref = refs[2 * n_in:]
    tm = res_ref.shape[0]
    slices = [slice(r, r + LN_ROWS) for r in range(0, tm, LN_ROWS)]
    accs = []
    for rows in slices:
        acc = None
        for a, w in zip(a_refs, w_refs):
            d = jnp.dot(a[rows, :], w[...], preferred_element_type=F32)
            acc = d if acc is None else acc + d
        accs.append(acc)
    for rows, acc in zip(slices, accs):
        out = _layer_norm(DN_ALPHA * res_ref[rows, :] + acc, g_ref[...], b_ref[...])
        of_ref[rows, :] = out
        ob_ref[rows, :] = out.astype(BF)


def _proj_ln(a_list, w_b, res, g, b, tm, name):
    t = res.shape[0]
    n_in = len(a_list)
    ka = a_list[0].shape[1]
    row = lambda i: (i, 0)
    in_specs = ([pl.BlockSpec((tm, ka), row) for _ in a_list]
                + [pl.BlockSpec((ka, D_MODEL), functools.partial(lambda i, r: (r, 0), r=r)) for r in range(n_in)]
                + [pl.BlockSpec((tm, D_MODEL), row),
                   pl.BlockSpec((1, D_MODEL), lambda i: (0, 0)),
                   pl.BlockSpec((1, D_MODEL), lambda i: (0, 0))])
    return pl.pallas_call(
        functools.partial(_proj_ln_kernel, n_in=n_in),
        grid=(t // tm,),
        in_specs=in_specs,
        out_specs=[pl.BlockSpec((tm, D_MODEL), row), pl.BlockSpec((tm, D_MODEL), row)],
        out_shape=[jax.ShapeDtypeStruct((t, D_MODEL), F32), jax.ShapeDtypeStruct((t, D_MODEL), BF)],
        compiler_params=_cparams(1),
        name=name,
    )(*a_list, *([w_b] * n_in), res, g.reshape(1, D_MODEL), b.reshape(1, D_MODEL))


def _chunk_visible(q0, k0, nq, nk):
    qpos = q0 + lax.broadcasted_iota(jnp.int32, (nq, nk), 0)
    kpos = k0 + lax.broadcasted_iota(jnp.int32, (nq, nk), 1)
    return lax.shift_right_logical(kpos, CHUNK_SHIFT) <= lax.shift_right_logical(qpos, CHUNK_SHIFT)


def _subln(o0, o1, lam, g, lam_init):
    of = o0 - lam * o1
    return of * lax.rsqrt(jnp.mean(of * of, axis=-1, keepdims=True) + LN_EPS) * g * (1.0 - lam_init)


def _diff_prompt_kernel(lam_ref, q_ref, k_ref, v_ref, g_ref, o_ref, m_ref, l_ref, acc_ref, *, tq, sub, lam_init):
    qi = pl.program_id(2)
    m_ref[...] = jnp.full(m_ref.shape, NEG_BIG, F32)
    l_ref[...] = jnp.zeros(l_ref.shape, F32)
    acc_ref[...] = jnp.zeros(acc_ref.shape, F32)

    def run(k0, items):
        scores = []
        for c, rows, nk, vis in items:
            cs = slice(c * HEAD_DIM, (c + 1) * HEAD_DIM)
            s = lax.dot_general(q_ref[0, rows, cs], k_ref[0, pl.ds(k0, nk), cs], _NT,
                                preferred_element_type=F32) * QK_SCALE2
            scores.append(s if vis is None else jnp.where(vis, s, -jnp.inf))
        probs = []
        for s, (c, rows, nk, _) in zip(scores, items):
            m_old = m_ref[c, rows]
            m_new = jnp.maximum(m_old, jnp.max(s, axis=-1, keepdims=True))
            alpha = jnp.exp2(m_old - m_new)
            p = jnp.exp2(s - m_new)
            l_ref[c, rows] = alpha * l_ref[c, rows] + jnp.sum(p, axis=-1, keepdims=True)
            m_ref[c, rows] = m_new
            probs.append((alpha, p.astype(BF)))
        for (alpha, p), (c, rows, nk, _) in zip(probs, items):
            acc_ref[c, rows] = alpha * acc_ref[c, rows] + jnp.dot(p, v_ref[0, pl.ds(k0, nk), :],
                                                                  preferred_element_type=F32)

    half = tq // 2
    full_items = [(c, slice(r * half, (r + 1) * half), tq, None) for r in range(2) for c in range(2)]

    def body(g, carry):
        run(pl.multiple_of(g * tq, tq), full_items)
        return carry

    lax.fori_loop(0, qi, body, 0)

    diag_items = [(c, slice(r * sub, (r + 1) * sub), (r + 1) * sub, _chunk_visible(r * sub, 0, sub, (r + 1) * sub))
                  for r in range(tq // sub) for c in range(2)]
    run(pl.multiple_of(qi * tq, tq), diag_items)

    o = _subln(acc_ref[0] / l_ref[0], acc_ref[1] / l_ref[1], lam_ref[0], g_ref[...], lam_init)
    o_ref[...] = o.astype(BF)


def _diff_attn_prompt(lam, q, k, v, subln_g, n_batch, t, lam_init, tq=1024, sub=256):
    assert t % tq == 0 and tq % sub == 0 and sub % CHUNK == 0
    nq = t // tq
    kern = functools.partial(_diff_prompt_kernel, tq=tq, sub=sub, lam_init=lam_init)
    return pl.pallas_call(
        kern,
        grid=(n_batch, N_DIFF_HEADS, nq),
        in_specs=[pl.BlockSpec(memory_space=pltpu.SMEM),
                  pl.BlockSpec((1, tq, DIFF_V_DIM), lambda b, h, i: (0, b * nq + i, h)),
                  pl.BlockSpec((1, t, DIFF_V_DIM), lambda b, h, i: (0, b, h)),
                  pl.BlockSpec((1, t, DIFF_V_DIM), lambda b, h, i: (0, b, h)),
                  pl.BlockSpec((1, DIFF_V_DIM), lambda b, h, i: (0, 0))],
        out_specs=pl.BlockSpec((tq, DIFF_V_DIM), lambda b, h, i: (b * nq + i, h)),
        out_shape=jax.ShapeDtypeStruct((n_batch * t, SEG), BF),
        scratch_shapes=[pltpu.VMEM((2, tq, 1), F32), pltpu.VMEM((2, tq, 1), F32),
                        pltpu.VMEM((2, tq, DIFF_V_DIM), F32)],
        compiler_params=_cparams(3),
        name="diff_attn_prompt",
    )(lam, q, k, v, subln_g.reshape(1, DIFF_V_DIM))


def _diff_sample_kernel(lam_ref, q_ref, kn_ref, vn_ref, kc_ref, vc_ref, g_ref, o_ref, *, t_s, past, lam_init):
    vis = _chunk_visible(past, past, t_s, t_s)
    for h in range(N_DIFF_HEADS):
        vs = slice(h * DIFF_V_DIM, (h + 1) * DIFF_V_DIM)
        v_old = _lane_rows(vc_ref, 0, past, h, DIFF_V_DIM, SEG, tiled=True).astype(BF)
        v_new = vn_ref[0, :, vs]
        o = []
        for c in range(2):
            cs = slice((2 * h + c) * HEAD_DIM, (2 * h + c + 1) * HEAD_DIM)
            q = q_ref[0, :, cs]
            k_old = _lane_rows(kc_ref, 0, past, 2 * h + c, HEAD_DIM, SEG).astype(BF)
            s_old = lax.dot_general(q, k_old, _NT, preferred_element_type=F32) * QK_SCALE2
            s_new = lax.dot_general(q, kn_ref[0, :, cs], _NT, preferred_element_type=F32) * QK_SCALE2
            s_new = jnp.where(vis, s_new, -jnp.inf)
            m = jnp.maximum(jnp.max(s_old, axis=-1, keepdims=True), jnp.max(s_new, axis=-1, keepdims=True))
            p_old = jnp.exp2(s_old - m)
            p_new = jnp.exp2(s_new - m)
            l = jnp.sum(p_old, axis=-1, keepdims=True) + jnp.sum(p_new, axis=-1, keepdims=True)
            acc = (jnp.dot(p_old.astype(BF), v_old, preferred_element_type=F32)
                   + jnp.dot(p_new.astype(BF), v_new, preferred_element_type=F32))
            o.append(acc / l)
        o_ref[:, vs] = _subln(o[0], o[1], lam_ref[0], g_ref[...], lam_init).astype(BF)


def _new_rows_spec(t_s):
    return pl.BlockSpec((1, t_s, SEG), lambda b: (0, b, 0))


def _as_lane_rows(x, n_batch):
    return x.reshape(n_batch, -1, HEAD_DIM)


def _lane_rows(ref, t0, nt, slot, slot_width, width, tiled=False):
    per_row = width // HEAD_DIM
    per_slot = slot_width // HEAD_DIM
    n_slots = width // slot_width
    offs = [(k * n_slots + slot) if tiled else (slot * per_slot + k) for k in range(per_slot)]
    parts = [ref[0, pl.ds(t0 * per_row + o, nt, stride=per_row), :] for o in offs]
    return parts[0] if per_slot == 1 else jnp.concatenate(parts, axis=1)


def _tiled_lane_rows(x, n_batch, n_slots):
    rows, w = x.shape[1], x.shape[3]
    x = x.reshape(n_batch, rows, n_slots, w // HEAD_DIM, HEAD_DIM)
    return jnp.transpose(x, (0, 1, 3, 2, 4)).reshape(n_batch, -1, HEAD_DIM)


def _cache_spec(cache):
    return pl.BlockSpec((1,) + cache.shape[1:], lambda b: (b, 0, 0))


def _diff_attn_sample(lam, q, k_new, v_new, cache_k, cache_v, subln_g, n_batch, t_s, past, lam_init):
    kern = functools.partial(_diff_sample_kernel, t_s=t_s, past=past, lam_init=lam_init)
    return pl.pallas_call(
        kern,
        grid=(n_batch,),
        in_specs=[pl.BlockSpec(memory_space=pltpu.SMEM), _new_rows_spec(t_s), _new_rows_spec(t_s),
                  _new_rows_spec(t_s), _cache_spec(cache_k), _cache_spec(cache_v),
                  pl.BlockSpec((1, DIFF_V_DIM), lambda b: (0, 0))],
        out_specs=pl.BlockSpec((t_s, SEG), lambda b: (b, 0)),
        out_shape=jax.ShapeDtypeStruct((n_batch * t_s, SEG), BF),
        compiler_params=_cparams(1),
        name="diff_attn_sample",
    )(lam, q, k_new, v_new, cache_k, cache_v, subln_g.reshape(1, DIFF_V_DIM))


def _sb_logs(zraw):
    z2 = zraw * QK_SCALE2
    mx = jnp.maximum(z2, 0.0)
    mn = jnp.minimum(z2, 0.0)
    t = jnp.log(1.0 + jnp.exp2(mn - mx)) * LOG2E
    return mx + t, mn - t


def _split_bf16(x):
    hi = x.astype(BF)
    return hi, (x - hi.astype(F32)).astype(BF)


def _sb_group(items, carry, acc, key_axis):
    if key_axis == 0:
        zs = [lax.dot_general(k, q, _NT, preferred_element_type=F32) for q, k, _, _, _, _ in items]
    else:
        zs = [lax.dot_general(q, k, _NT, preferred_element_type=F32) for q, k, _, _, _, _ in items]
    logs = []
    for z, (_, _, _, tri, before, _) in zip(zs, items):
        nlk, lp = _sb_logs(z)
        if before is not None:
            nlk = jnp.where(before, nlk, 0.0)
        hi, lo = _split_bf16(nlk)
        fused = tri.shape[1 - key_axis] == 2 * tri.shape[key_axis]
        logs.append((nlk, lp, jnp.concatenate([hi, lo], axis=key_axis) if fused else (hi, lo)))
    sufs = []
    for (_, _, hl), (_, _, _, tri, _, _) in zip(logs, items):
        if key_axis == 0:
            mm = lambda x: jnp.dot(tri, x, preferred_element_type=F32)
        else:
            mm = lambda x: jnp.dot(x, tri, preferred_element_type=F32)
        sufs.append(mm(hl[0]) + mm(hl[1]) if isinstance(hl, tuple) else mm(hl))
    probs = []
    for (nlk, lp, _), suf, (_, _, _, _, before, ch) in zip(logs, sufs, items):
        a = jnp.exp2(lp - suf - carry[ch])
        if before is not None:
            a = jnp.where(before, a, 0.0)
        probs.append(a.astype(BF))
        if key_axis == 0:
            carry[ch] = carry[ch] + (suf[0:1, :] + nlk[0:1, :])
        else:
            carry[ch] = carry[ch] + (suf[:, 0:1] + nlk[:, 0:1])
    for a, (_, _, v_op, _, _, ch) in zip(probs, items):
        if key_axis == 0:
            acc[ch] = acc[ch] + jnp.dot(v_op, a, preferred_element_type=F32)
        else:
            acc[ch] = acc[ch] + jnp.dot(a, v_op, preferred_element_type=F32)


def _tri(n, upper):
    r = lax.broadcasted_iota(jnp.int32, (n, n), 0)
    c = lax.broadcasted_iota(jnp.int32, (n, n), 1)
    return jnp.where((c > r) if upper else (r > c), 1.0, 0.0).astype(BF)


def _sb_prompt_kernel(q_ref, k_ref, v_ref, o_ref, vt_ref, carry_ref, acc_ref, *, tq, sub, t, unroll):
    qi = pl.program_id(2)
    n_sub = tq // sub

    @pl.when(qi == 0)
    def _():
        for kb in range(t // sub):
            vt_ref[kb] = v_ref[0, kb * sub:(kb + 1) * sub, :].astype(F32).T.astype(BF)

    u = _tri(sub, True)
    tri2 = jnp.concatenate([u, u], axis=1)
    r_i = lax.broadcasted_iota(jnp.int32, (sub, sub), 0)
    c_i = lax.broadcasted_iota(jnp.int32, (sub, sub), 1)
    before = r_i < c_i
    qs = [q_ref[0, c * sub:(c + 1) * sub, :] for c in range(n_sub)]
    carry = [jnp.zeros((1, sub), F32) for _ in range(n_sub)]
    acc = [jnp.zeros((HEAD_DIM, sub), F32) for _ in range(n_sub)]

    base = qi * n_sub
    items = []
    for j in reversed(range(n_sub)):
        kblk = k_ref[0, pl.ds(pl.multiple_of((base + j) * sub, sub), sub), :]
        vt_blk = vt_ref[base + j]
        items += [(qs[c], kblk, vt_blk, tri2, before if c == j else None, c) for c in range(j, n_sub)]
    _sb_group(items, carry, acc, 0)
    for c in range(n_sub):
        carry_ref[c] = carry[c]
        acc_ref[c] = acc[c]

    def body(i, x):
        cr = [carry_ref[c] for c in range(n_sub)]
        ac = [acc_ref[c] for c in range(n_sub)]
        items = []
        for uu in range(unroll):
            kb = base - 1 - (i * unroll + uu)
            kblk = k_ref[0, pl.ds(pl.multiple_of(kb * sub, sub), sub), :]
            vt_blk = vt_ref[kb]
            items += [(qs[c], kblk, vt_blk, tri2, None, c) for c in range(n_sub)]
        _sb_group(items, cr, ac, 0)
        for c in range(n_sub):
            carry_ref[c] = cr[c]
            acc_ref[c] = ac[c]
        return x

    lax.fori_loop(0, (qi * n_sub) // unroll, body, 0)
    for c in range(n_sub):
        o_ref[c * sub:(c + 1) * sub, :] = acc_ref[c].T.astype(BF)


def _sb_attn_prompt(q, k, v, n_batch, t, tq=1024, sub=256, unroll=2):
    assert t % tq == 0 and tq % sub == 0 and (tq // sub) % unroll == 0
    nq = t // tq
    n_sub = tq // sub
    kern = functools.partial(_sb_prompt_kernel, tq=tq, sub=sub, t=t, unroll=unroll)
    return pl.pallas_call(
        kern,
        grid=(n_batch, N_SB_HEADS, nq),
        in_specs=[pl.BlockSpec((1, tq, HEAD_DIM), lambda b, h, i: (0, b * nq + i, h)),
                  pl.BlockSpec((1, t, HEAD_DIM), lambda b, h, i: (0, b, h)),
                  pl.BlockSpec((1, t, HEAD_DIM), lambda b, h, i: (0, b, h))],
        out_specs=pl.BlockSpec((tq, HEAD_DIM), lambda b, h, i: (b * nq + i, h)),
        out_shape=jax.ShapeDtypeStruct((n_batch * t, SEG), BF),
        scratch_shapes=[pltpu.VMEM((t // sub, HEAD_DIM, sub), BF), pltpu.VMEM((n_sub, 1, sub), F32),
                        pltpu.VMEM((n_sub, HEAD_DIM, sub), F32)],
        compiler_params=_cparams(3),
        name="sb_attn_prompt",
    )(q, k, v)


def _sb_sample_kernel(q_ref, kn_ref, vn_ref, kc_ref, vc_ref, o_ref, *, t_s, past, tk):
    tri_new = _tri(t_s, False)
    l_old = _tri(tk, False)
    tri_old = jnp.concatenate([l_old, l_old], axis=0)
    before = (lax.broadcasted_iota(jnp.int32, (t_s, t_s), 1) < lax.broadcasted_iota(jnp.int32, (t_s, t_s), 0))
    heads = range(N_SB_HEADS)
    cols = [slice(h * HEAD_DIM, (h + 1) * HEAD_DIM) for h in heads]
    qs = [q_ref[0, :, cs] for cs in cols]
    carry = [jnp.zeros((t_s, 1), F32) for _ in heads]
    acc = [jnp.zeros((t_s, HEAD_DIM), F32) for _ in heads]
    _sb_group([(qs[h], kn_ref[0, :, cols[h]], vn_ref[0, :, cols[h]], tri_new, before, h) for h in heads],
              carry, acc, 1)
    for kb in reversed(range(past // tk)):
        rows = slice(kb * tk, (kb + 1) * tk)
        _sb_group([(qs[h], _lane_rows(kc_ref, kb * tk, tk, h, HEAD_DIM, SEG).astype(BF),
                    _lane_rows(vc_ref, kb * tk, tk, h, HEAD_DIM, SEG).astype(BF), tri_old, None, h)
                   for h in heads], carry, acc, 1)
    for h in heads:
        o_ref[:, cols[h]] = acc[h].astype(BF)


def _sb_attn_sample(q, k_new, v_new, cache_k, cache_v, n_batch, t_s, past, tk=256):
    assert past % tk == 0
    kern = functools.partial(_sb_sample_kernel, t_s=t_s, past=past, tk=tk)
    return pl.pallas_call(
        kern,
        grid=(n_batch,),
        in_specs=[_new_rows_spec(t_s), _new_rows_spec(t_s), _new_rows_spec(t_s),
                  _cache_spec(cache_k), _cache_spec(cache_v)],
        out_specs=pl.BlockSpec((t_s, SEG), lambda b: (b, 0)),
        out_shape=jax.ShapeDtypeStruct((n_batch * t_s, SEG), BF),
        compiler_params=_cparams(1),
        name="sb_attn_sample",
    )(q, k_new, v_new, cache_k, cache_v)


def _cross_kernel(q_ref, k_ref, v_ref, o_ref):
    scale = MEM_HEAD_DIM ** -0.5
    n_mem = k_ref.shape[1] * HEAD_DIM // D_MODEL
    for h in range(N_MEM_HEADS):
        sl = slice(h * MEM_HEAD_DIM, (h + 1) * MEM_HEAD_DIM)
        k = _lane_rows(k_ref, 0, n_mem, h, MEM_HEAD_DIM, D_MODEL, tiled=True).astype(BF)
        v = _lane_rows(v_ref, 0, n_mem, h, MEM_HEAD_DIM, D_MODEL, tiled=True).astype(BF)
        s = lax.dot_general(q_ref[:, sl], k, _NT, preferred_element_type=F32) * scale
        p = jnp.exp(s - jnp.max(s, axis=-1, keepdims=True))
        l = jnp.sum(p, axis=-1, keepdims=True)
        o = jnp.dot(p.astype(BF), v, preferred_element_type=F32) / l
        o_ref[:, sl] = o.astype(BF)


def _cross_attn(q_b, memk, memv, n_batch, t_b, tm):
    nt = t_b // tm
    mem_spec = pl.BlockSpec((1,) + memk.shape[1:], lambda b, i: (b, 0, 0))
    return pl.pallas_call(
        _cross_kernel,
        grid=(n_batch, nt),
        in_specs=[pl.BlockSpec((tm, D_MODEL), lambda b, i: (b * nt + i, 0)), mem_spec, mem_spec],
        out_specs=pl.BlockSpec((tm, D_MODEL), lambda b, i: (b * nt + i, 0)),
        out_shape=jax.ShapeDtypeStruct((n_batch * t_b, D_MODEL), BF),
        compiler_params=_cparams(2),
        name="cross_attn",
    )(q_b, memk, memv)


def _gelu_tanh(x):
    return 0.5 * x * (1.0 + jnp.tanh(math.sqrt(2.0 / math.pi) * (x + 0.044715 * (x * x * x))))


def _ffn_kernel(h_ref, wg_ref, wu_ref, wd_ref, wc_ref, bc_ref, st_ref, g_ref, b_ref,
                o_ref, last_ref, hb_ref, carry_ref, gs_ref, *, tm, fc, seq, n_ff):
    i = pl.program_id(0)
    c = pl.program_id(1)

    @pl.when(c == 0)
    def _():
        hb_ref[...] = h_ref[...].astype(BF)
        o_ref[...] = jnp.zeros(o_ref.shape, F32)

    @pl.when((i == 0) & (c == 0))
    def _():
        carry_ref[...] = jnp.zeros(carry_ref.shape, F32)

    hb = hb_ref[...]
    halves = [slice(k, k + FFN_COLS) for k in range(0, fc, FFN_COLS)]
    gs = [jnp.dot(hb, wg_ref[:, cols], preferred_element_type=F32) for cols in halves]
    us = [jnp.dot(hb, wu_ref[:, cols], preferred_element_type=F32) for cols in halves]
    row = lax.broadcasted_iota(jnp.int32, (tm, FFN_COLS), 0)
    acts = []
    for g, u, cols in zip(gs, us, halves):
        if seq >= tm:
            tiles_per_seq = seq // tm
            at_start = (i % tiles_per_seq) == 0
            prev = jnp.where(at_start, st_ref[0, :, cols], carry_ref[c, :, cols])
            pos = row
            ex0 = prev[0:1]
            ex1 = prev[1:2]
            carry_ref[c, :, cols] = g[tm - 2:tm, :]
            last_ref[0, :, cols] = g[tm - 2:tm, :]
        else:
            ns = tm // seq
            pos = jnp.bitwise_and(row, seq - 1)
            ex0 = jnp.broadcast_to(st_ref[0, :, cols][:, None, :], (ns, seq, FFN_COLS)).reshape(tm, FFN_COLS)
            ex1 = jnp.broadcast_to(st_ref[1, :, cols][:, None, :], (ns, seq, FFN_COLS)).reshape(tm, FFN_COLS)
            for k in range(FFN_COLS // 128):
                lanes = slice(cols.start + k * 128, cols.start + (k + 1) * 128)
                gs_ref[k] = g[:, k * 128:(k + 1) * 128]
                last_ref[0, :, lanes] = gs_ref[k, pl.ds(seq - 2, ns, stride=seq), :]
                last_ref[1, :, lanes] = gs_ref[k, pl.ds(seq - 1, ns, stride=seq), :]
        g_m1 = jnp.where(pos == 0, ex1, pltpu.roll(g, 1, 0))
        g_m2 = jnp.where(pos == 0, ex0, jnp.where(pos == 1, ex1, pltpu.roll(g, 2, 0)))
        conv = (bc_ref[:, cols] + g_m2 * wc_ref[0:1, cols] + g_m1 * wc_ref[1:2, cols] + g * wc_ref[2:3, cols])
        acts.append((_gelu_tanh(conv) * u).astype(BF))
    d = None
    for a, cols in zip(acts, halves):
        dd = jnp.dot(a, wd_ref[cols, :], preferred_element_type=F32)
        d = dd if d is None else d + dd
    o_ref[...] += d

    @pl.when(c == n_ff - 1)
    def _():
        o_ref[...] = _layer_norm(DN_ALPHA * h_ref[...] + o_ref[...], g_ref[...], b_ref[...])


def _ffn(h, w_gate_b, w_up_b, w_down_b, w_conv, b_conv, state, ln_g, ln_b, seq, tm, fc):
    t = h.shape[0]
    n_ff = D_FF // fc
    nt = t // tm
    if seq >= tm:
        tiles_per_seq = seq // tm
        st_spec = pl.BlockSpec((1, 2, fc), lambda i, c: (i // tiles_per_seq, 0, c))
        last_spec = pl.BlockSpec((1, 2, fc), lambda i, c: (i, 0, c))
        last_shape = jax.ShapeDtypeStruct((nt, 2, D_FF), F32)
    else:
        ns = tm // seq
        st_spec = pl.BlockSpec((2, ns, fc), lambda i, c: (0, i, c))
        last_spec = pl.BlockSpec((2, ns, fc), lambda i, c: (0, i, c))
        last_shape = jax.ShapeDtypeStruct((2, t // seq, D_FF), F32)
    kern = functools.partial(_ffn_kernel, tm=tm, fc=fc, seq=seq, n_ff=n_ff)
    return pl.pallas_call(
        kern,
        grid=(nt, n_ff),
        in_specs=[pl.BlockSpec((tm, D_MODEL), lambda i, c: (i, 0), pipeline_mode=pl.Buffered(1)),
                  pl.BlockSpec((D_MODEL, fc), lambda i, c: (0, c)),
                  pl.BlockSpec((D_MODEL, fc), lambda i, c: (0, c)),
                  pl.BlockSpec((fc, D_MODEL), lambda i, c: (c, 0)),
                  pl.BlockSpec((CONV_W, fc), lambda i, c: (0, c)),
                  pl.BlockSpec((1, fc), lambda i, c: (0, c)),
                  st_spec,
                  pl.BlockSpec((1, D_MODEL), lambda i, c: (0, 0)),
                  pl.BlockSpec((1, D_MODEL), lambda i, c: (0, 0))],
        out_specs=[pl.BlockSpec((tm, D_MODEL), lambda i, c: (i, 0)), last_spec],
        out_shape=[jax.ShapeDtypeStruct((t, D_MODEL), F32), last_shape],
        scratch_shapes=[pltpu.VMEM((tm, D_MODEL), BF), pltpu.VMEM((n_ff, 2, fc), F32),
                        pltpu.VMEM((fc // 128, tm, 128), F32)],
        compiler_params=_cparams(2),
        name="conv_ffn",
    )(h, w_gate_b, w_up_b, w_down_b, w_conv, b_conv.reshape(1, D_FF), state,
      ln_g.reshape(1, D_MODEL), ln_b.reshape(1, D_MODEL))


def _after_mixers(x, od, os_, memk, memv, conv_state, n_batch, t_b, w):
    tm = 512
    h1, h1b = _proj_ln([od, os_], w['w_o'], x, w['ln1_g'], w['ln1_b'], tm, "out_proj_ln1")
    (qc,) = _matmul(h1b, w['w_mq'], [BF], tm, 1024, "cross_q")
    oc = _cross_attn(qc, memk, memv, n_batch, t_b, min(t_b, tm))
    h2, _ = _proj_ln([oc], w['w_mo'], h1, w['ln2_g'], w['ln2_b'], tm, "cross_out_ln2")
    return _ffn(h2, w['w_gate'], w['w_up'], w['w_down'], w['w_conv'], w['b_conv'], conv_state,
                w['ln3_g'], w['ln3_b'], t_b, FFN_ROWS, FFN_CHUNK)


def kernel(x_prompt, x_sample, mem_prompt, cache_diff_k, cache_diff_v, cache_sb_k, cache_sb_v, cache_mem_k, cache_mem_v, state_ffn_conv, w_in, lambda_q1, lambda_k1, lambda_q2, lambda_k2, subln_g, w_o, ln1_g, ln1_b, w_mq, w_mk, w_mv, w_mo, ln2_g, ln2_b, w_gate, w_up, w_conv, b_conv, w_down, ln3_g, ln3_b):
    assert w_in.shape[0] == DEPTH == 1
    n_p, t_p, _ = x_prompt.shape
    n_s, t_s, _ = x_sample.shape
    past = cache_diff_k.shape[2]
    n_mem = mem_prompt.shape[1]
    l = 0
    lam_init = 0.8 - 0.6 * math.exp(-0.3 * l)
    lam = (jnp.exp(jnp.sum(lambda_q1[l] * lambda_k1[l])) - jnp.exp(jnp.sum(lambda_q2[l] * lambda_k2[l]))
           + lam_init).reshape(1).astype(F32)
    w = {'w_in': w_in[l].astype(BF), 'w_o': w_o[l].astype(BF), 'w_mq': w_mq[l].astype(BF),
         'w_mo': w_mo[l].astype(BF), 'w_gate': w_gate[l].astype(BF), 'w_up': w_up[l].astype(BF),
         'w_down': w_down[l].astype(BF), 'w_conv': w_conv[l], 'b_conv': b_conv[l],
         'subln_g': subln_g[l], 'ln1_g': ln1_g[l], 'ln1_b': ln1_b[l], 'ln2_g': ln2_g[l], 'ln2_b': ln2_b[l],
         'ln3_g': ln3_g[l], 'ln3_b': ln3_b[l]}
    tm = 512

    w_mkv = jnp.concatenate([w_mk[l], w_mv[l]], axis=1).astype(BF)
    mem_b = mem_prompt.reshape(n_p * n_mem, D_MODEL).astype(BF)
    (mkv_f,) = _matmul(mem_b, w_mkv, [F32], n_p * n_mem, 1024, "memory_kv")
    xp = x_prompt.reshape(n_p * t_p, D_MODEL)
    pb, pf = _inproj(xp, w['w_in'], jnp.tile(jnp.arange(t_p), n_p), tm)
    odp = _diff_attn_prompt(lam, pb['dq'], pb['dk'], pb['dv'], w['subln_g'], n_p, t_p, lam_init)
    osp = _sb_attn_prompt(pb['sq'], pb['sk'], pb['sv'], n_p, t_p)
    zero_state = jnp.zeros((n_p, CONV_W - 1, D_FF), F32)
    mem_heads = (n_p, n_mem, N_MEM_HEADS, MEM_HEAD_DIM)
    yp, lastp = _after_mixers(xp, odp, osp, _tiled_lane_rows(mkv_f[:, :D_MODEL].reshape(mem_heads), n_p, N_MEM_HEADS),
                              _tiled_lane_rows(mkv_f[:, D_MODEL:].reshape(mem_heads), n_p, N_MEM_HEADS),
                              zero_state, n_p, t_p, w)

    xs = x_sample.reshape(n_s * t_s, D_MODEL)
    sb, sf = _inproj(xs, w['w_in'], jnp.tile(past + jnp.arange(t_s), n_s), tm)
    rows = lambda c: _as_lane_rows(c[l], n_s)
    ods = _diff_attn_sample(lam, sb['dq'], sb['dk'], sb['dv'], rows(cache_diff_k),
                            _tiled_lane_rows(cache_diff_v[l], n_s, N_DIFF_HEADS), w['subln_g'],
                            n_s, t_s, past, lam_init)
    oss = _sb_attn_sample(sb['sq'], sb['sk'], sb['sv'], rows(cache_sb_k), rows(cache_sb_v), n_s, t_s, past)
    state_s = jnp.transpose(state_ffn_conv[l], (1, 0, 2))
    ys, lasts = _after_mixers(xs, ods, oss, _tiled_lane_rows(cache_mem_k[l], n_s, N_MEM_HEADS),
                              _tiled_lane_rows(cache_mem_v[l], n_s, N_MEM_HEADS), state_s, n_s, t_s, w)

    tiles_per_seq = t_p // FFN_ROWS
    new_conv_p = lastp[tiles_per_seq - 1::tiles_per_seq]
    new_conv_s = jnp.transpose(lasts, (1, 0, 2))

    def new_rows(f, n, t):
        dv = jnp.transpose(f['dv'].reshape(n, t, DIFF_V_DIM // HEAD_DIM, N_DIFF_HEADS, HEAD_DIM), (0, 1, 3, 2, 4))
        return (f['dk'].reshape(1, n, t, N_DIFF_HEADS, 2, HEAD_DIM), dv.reshape(1, n, t, N_DIFF_HEADS, DIFF_V_DIM),
                f['sk'].reshape(1, n, t, N_SB_HEADS, HEAD_DIM), f['sv'].reshape(1, n, t, N_SB_HEADS, HEAD_DIM))

    return ((yp.reshape(n_p, t_p, D_MODEL), ys.reshape(n_s, t_s, D_MODEL))
            + new_rows(pf, n_p, t_p)
            + (mkv_f[:, :D_MODEL].reshape((1,) + mem_heads), mkv_f[:, D_MODEL:].reshape((1,) + mem_heads),
               new_conv_p[None])
            + new_rows(sf, n_s, t_s)
            + (new_conv_s[None],))
```

```python
import functools
import math

import jax
import jax.numpy as jnp
from jax import lax
from jax.experimental import pallas as pl
from jax.experimental.pallas import tpu as pltpu

BF = jnp.bfloat16
F32 = jnp.float32

D_MODEL = 2048
CHUNK = 64
CHUNK_SHIFT = 6
HEAD_DIM = 128
N_DIFF_HEADS = 4
N_SB_HEADS = 8
DIFF_V_DIM = 2 * HEAD_DIM
ROPE_DIM = HEAD_DIM // 4
ROPE_THETA = 500000.0
N_MEM_HEADS = 4
MEM_HEAD_DIM = D_MODEL // N_MEM_HEADS
D_FF = 5632
CONV_W = 3
LN_EPS = 1e-5
DEPTH = 1
DN_ALPHA = (2 * DEPTH) ** 0.25
SEG = 1024
N_SEG = 6
VMEM_LIMIT = 56 * 1024 * 1024
LOG2E = math.log2(math.e)
QK_SCALE2 = HEAD_DIM ** -0.5 * LOG2E
NEG_BIG = -1e30
LN_ROWS = 128
FFN_COLS = 256
FFN_ROWS = 1024
FFN_CHUNK = 512

_NT = (((1,), (1,)), ((), ()))


def _cparams(n_axes):
    return pltpu.CompilerParams(dimension_semantics=("arbitrary",) * n_axes,
                                vmem_limit_bytes=VMEM_LIMIT)


def _layer_norm(y, g, b):
    mu = jnp.mean(y, axis=-1, keepdims=True)
    yc = y - mu
    var = jnp.mean(yc * yc, axis=-1, keepdims=True)
    return yc * lax.rsqrt(var + LN_EPS) * g + b


SLOTS = SEG // HEAD_DIM


def _inproj_kernel(*refs, rope, want_f32, tile_rows, scale):
    xb_ref, w_ref = refs[:2]
    tabs = refs[2:5] if rope else None
    outs = refs[5:] if rope else refs[2:]
    pb_ref = outs[0]
    acc = jnp.dot(xb_ref[...], w_ref[...], preferred_element_type=F32)
    tm = acc.shape[0]
    for ch in range(SLOTS):
        lanes = slice(ch * HEAD_DIM, (ch + 1) * HEAD_DIM)
        r = acc[:, lanes]
        if rope:
            c_ref, s1_ref, s2_ref = tabs
            r = (r * c_ref[...] + pltpu.roll(r, HEAD_DIM - ROPE_DIM // 2, 1) * s1_ref[...]
                 + pltpu.roll(r, ROPE_DIM // 2, 1) * s2_ref[...])
        elif scale != 1.0:
            r = r * scale
        pb_ref[0, :, lanes] = r.astype(BF)
        if want_f32:
            head, col_tile = divmod(ch, SLOTS // tile_rows) if tile_rows else (0, 0)
            off = col_tile * tile_rows + head if tile_rows else ch
            outs[1][pl.ds(off, tm, stride=SLOTS), :] = r


def _rope_tables(pos):
    half = ROPE_DIM // 2
    inv_freq = jnp.power(ROPE_THETA, -jnp.arange(half, dtype=F32) * (2.0 / ROPE_DIM))
    ang = pos.astype(F32)[:, None] * inv_freq[None, :]
    cos, sin = jnp.cos(ang), jnp.sin(ang)
    n = pos.shape[0]
    c = jnp.concatenate([cos, cos, jnp.ones((n, HEAD_DIM - ROPE_DIM), F32)], axis=1)
    s1 = jnp.concatenate([-sin, jnp.zeros((n, HEAD_DIM - half), F32)], axis=1)
    s2 = jnp.concatenate([jnp.zeros((n, half), F32), sin, jnp.zeros((n, HEAD_DIM - ROPE_DIM), F32)], axis=1)
    return c, s1, s2


IN_GROUPS = (("dq", True, False, 0, QK_SCALE2), ("dk", True, True, 0, 1.0), ("dv", False, True, N_DIFF_HEADS, 1.0),
             ("sq", False, False, 0, QK_SCALE2), ("sk", False, True, 0, 1.0), ("sv", False, True, 0, 1.0))


def _inproj(x, w_in_b, pos_rows, tm):
    t = x.shape[0]
    xb = x.astype(BF)
    tabs = _rope_tables(pos_rows)
    row = lambda i: (i, 0)
    tab = pl.BlockSpec((tm, HEAD_DIM), row)
    bf, f32 = {}, {}
    for j, (name, rope, want_f32, tile_rows, scale) in enumerate(IN_GROUPS):
        assert not (want_f32 and scale != 1.0)
        outs = pl.pallas_call(
            functools.partial(_inproj_kernel, rope=rope, want_f32=want_f32, tile_rows=tile_rows, scale=scale),
            grid=(t // tm,),
            in_specs=[pl.BlockSpec((tm, D_MODEL), row),
                      pl.BlockSpec((D_MODEL, SEG), functools.partial(lambda i, j: (0, j), j=j))]
                     + ([tab, tab, tab] if rope else []),
            out_specs=[pl.BlockSpec((1, tm, SEG), lambda i: (0, i, 0))]
                      + ([pl.BlockSpec((tm * SLOTS, HEAD_DIM), row)] if want_f32 else []),
            out_shape=[jax.ShapeDtypeStruct((1, t, SEG), BF)]
                      + ([jax.ShapeDtypeStruct((t * SLOTS, HEAD_DIM), F32)] if want_f32 else []),
            compiler_params=_cparams(1),
            name="inproj_" + name,
        )(xb, w_in_b, *([tb * scale for tb in tabs] if rope else ()))
        bf[name] = outs[0]
        if want_f32:
            f32[name] = outs[1]
    return bf, f32


def _matmul_kernel(x_ref, w_ref, *o_refs):
    acc = jnp.dot(x_ref[...], w_ref[...], preferred_element_type=F32)
    for o in o_refs:
        o[...] = acc.astype(o.dtype)


def _matmul(x_b, w_b, out_dtypes, tm, tn, name):
    t, k = x_b.shape
    n = w_b.shape[1]
    return pl.pallas_call(
        _matmul_kernel,
        grid=(t // tm, n // tn),
        in_specs=[pl.BlockSpec((tm, k), lambda i, j: (i, 0)),
                  pl.BlockSpec((k, tn), lambda i, j: (0, j))],
        out_specs=[pl.BlockSpec((tm, tn), lambda i, j: (i, j)) for _ in out_dtypes],
        out_shape=[jax.ShapeDtypeStruct((t, n), d) for d in out_dtypes],
        compiler_params=_cparams(2),
        name=name,
    )(x_b, w_b)


def _proj_ln_kernel(*refs, n_in):
    a_refs = refs[:n_in]
    w_refs = refs[n_in:2 * n_in]
    res_ref, g_ref, b_ref, of_ref, ob_ref = refs[2 * n_in:]
    tm = res_ref.shape[0]
    slices = [slice(r, r + LN_ROWS) for r in range(0, tm, LN_ROWS)]
    accs = []
    for rows in slices:
        acc = None
        for a, w in zip(a_refs, w_refs):
            d = jnp.dot(a[rows, :], w[...], preferred_element_type=F32)
            acc = d if acc is None else acc + d
        accs.append(acc)
    for rows, acc in zip(slices, accs):
        out = _layer_norm(DN_ALPHA * res_ref[rows, :] + acc, g_ref[...], b_ref[...])
        of_ref[rows, :] = out
        ob_ref[rows, :] = out.astype(BF)


def _proj_ln(a_list, w_b, res, g, b, tm, name):
    t = res.shape[0]
    n_in = len(a_list)
    ka = a_list[0].shape[1]
    row = lambda i: (i, 0)
    in_specs = ([pl.BlockSpec((tm, ka), row) for _ in a_list]
                + [pl.BlockSpec((ka, D_MODEL), functools.partial(lambda i, r: (r, 0), r=r)) for r in range(n_in)]
                + [pl.BlockSpec((tm, D_MODEL), row),
                   pl.BlockSpec((1, D_MODEL), lambda i: (0, 0)),
                   pl.BlockSpec((1, D_MODEL), lambda i: (0, 0))])
    return pl.pallas_call(
        functools.partial(_proj_ln_kernel, n_in=n_in),
        grid=(t // tm,),
        in_specs=in_specs,
        out_specs=[pl.BlockSpec((tm, D_MODEL), row), pl.BlockSpec((tm, D_MODEL), row)],
        out_shape=[jax.ShapeDtypeStruct((t, D_MODEL), F32), jax.ShapeDtypeStruct((t, D_MODEL), BF)],
        compiler_params=_cparams(1),
        name=name,
    )(*a_list, *([w_b] * n_in), res, g.reshape(1, D_MODEL), b.reshape(1, D_MODEL))


def _chunk_visible(q0, k0, nq, nk):
    qpos = q0 + lax.broadcasted_iota(jnp.int32, (nq, nk), 0)
    kpos = k0 + lax.broadcasted_iota(jnp.int32, (nq, nk), 1)
    return lax.shift_right_logical(kpos, CHUNK_SHIFT) <= lax.shift_right_logical(qpos, CHUNK_SHIFT)


def _subln(o0, o1, lam, g, lam_init):
    of = o0 - lam * o1
    return of * lax.rsqrt(jnp.mean(of * of, axis=-1, keepdims=True) + LN_EPS) * g * (1.0 - lam_init)


def _diff_prompt_kernel(lam_ref, q_ref, k_ref, v_ref, g_ref, o_ref, m_ref, l_ref, acc_ref, *, tq, sub, lam_init):
    qi = pl.program_id(2)
    m_ref[...] = jnp.full(m_ref.shape, NEG_BIG, F32)
    l_ref[...] = jnp.zeros(l_ref.shape, F32)
    acc_ref[...] = jnp.zeros(acc_ref.shape, F32)

    def run(k0, items):
        scores = []
        for c, rows, nk, vis in items:
            cs = slice(c * HEAD_DIM, (c + 1) * HEAD_DIM)
            s = lax.dot_general(q_ref[0, rows, cs], k_ref[0, pl.ds(k0, nk), cs], _NT,
                                preferred_element_type=F32)
            if vis is not None:
                tail = jnp.where(vis, s[:, nk - sub:], -jnp.inf)
                s = tail if nk == sub else jnp.concatenate([s[:, :nk - sub], tail], axis=1)
            scores.append(s)
        probs = []
        for s, (c, rows, nk, _) in zip(scores, items):
            m_old = m_ref[c, rows]
            m_new = jnp.maximum(m_old, jnp.max(s, axis=-1, keepdims=True))
            alpha = jnp.exp2(m_old - m_new)
            p = jnp.exp2(s - m_new)
            l_ref[c, rows] = alpha * l_ref[c, rows] + jnp.sum(p, axis=-1, keepdims=True)
            m_ref[c, rows] = m_new
            probs.append((alpha, p.astype(BF)))
        for (alpha, p), (c, rows, nk, _) in zip(probs, items):
            acc_ref[c, rows] = alpha * acc_ref[c, rows] + jnp.dot(p, v_ref[0, pl.ds(k0, nk), :],
                                                                  preferred_element_type=F32)

    full_items = [(c, slice(r * sub, (r + 1) * sub), tq, None) for r in range(tq // sub) for c in range(2)]

    def body(g, carry):
        run(pl.multiple_of(g * tq, tq), full_items)
        return carry

    lax.fori_loop(0, qi, body, 0)

    vis = _chunk_visible(0, 0, sub, sub)
    diag_items = [(c, slice(r * sub, (r + 1) * sub), (r + 1) * sub, vis)
                  for r in range(tq // sub) for c in range(2)]
    run(pl.multiple_of(qi * tq, tq), diag_items)

    o = _subln(acc_ref[0] / l_ref[0], acc_ref[1] / l_ref[1], lam_ref[0], g_ref[...], lam_init)
    o_ref[...] = o.astype(BF)


def _diff_attn_prompt(lam, q, k, v, subln_g, n_batch, t, lam_init, tq=1024, sub=256):
    assert t % tq == 0 and tq % sub == 0 and sub % CHUNK == 0
    nq = t // tq
    kern = functools.partial(_diff_prompt_kernel, tq=tq, sub=sub, lam_init=lam_init)
    return pl.pallas_call(
        kern,
        grid=(n_batch, N_DIFF_HEADS, nq),
        in_specs=[pl.BlockSpec(memory_space=pltpu.SMEM),
                  pl.BlockSpec((1, tq, DIFF_V_DIM), lambda b, h, i: (0, b * nq + i, h)),
                  pl.BlockSpec((1, t, DIFF_V_DIM), lambda b, h, i: (0, b, h)),
                  pl.BlockSpec((1, t, DIFF_V_DIM), lambda b, h, i: (0, b, h)),
                  pl.BlockSpec((1, DIFF_V_DIM), lambda b, h, i: (0, 0))],
        out_specs=pl.BlockSpec((tq, DIFF_V_DIM), lambda b, h, i: (b * nq + i, h)),
        out_shape=jax.ShapeDtypeStruct((n_batch * t, SEG), BF),
        scratch_shapes=[pltpu.VMEM((2, tq, 1), F32), pltpu.VMEM((2, tq, 1), F32),
                        pltpu.VMEM((2, tq, DIFF_V_DIM), F32)],
        compiler_params=_cparams(3),
        name="diff_attn_prompt",
    )(lam, q, k, v, subln_g.reshape(1, DIFF_V_DIM))


def _diff_sample_kernel(lam_ref, q_ref, kn_ref, vn_ref, kc_ref, vc_ref, g_ref, o_ref, *, t_s, past, lam_init):
    vis = _chunk_visible(past, past, t_s, t_s)
    for h in range(N_DIFF_HEADS):
        vs = slice(h * DIFF_V_DIM, (h + 1) * DIFF_V_DIM)
        v_old = _lane_rows(vc_ref, 0, past, h, DIFF_V_DIM, SEG, tiled=True).astype(BF)
        v_new = vn_ref[0, :, vs]
        o = []
        for c in range(2):
            cs = slice((2 * h + c) * HEAD_DIM, (2 * h + c + 1) * HEAD_DIM)
            q = q_ref[0, :, cs]
            k_old = _lane_rows(kc_ref, 0, past, 2 * h + c, HEAD_DIM, SEG).astype(BF)
            s_old = lax.dot_general(q, k_old, _NT, preferred_element_type=F32)
            s_new = lax.dot_general(q, kn_ref[0, :, cs], _NT, preferred_element_type=F32)
            s_new = jnp.where(vis, s_new, -jnp.inf)
            m = jnp.maximum(jnp.max(s_old, axis=-1, keepdims=True), jnp.max(s_new, axis=-1, keepdims=True))
            p_old = jnp.exp2(s_old - m)
            p_new = jnp.exp2(s_new - m)
            l = jnp.sum(p_old, axis=-1, keepdims=True) + jnp.sum(p_new, axis=-1, keepdims=True)
            acc = (jnp.dot(p_old.astype(BF), v_old, preferred_element_type=F32)
                   + jnp.dot(p_new.astype(BF), v_new, preferred_element_type=F32))
            o.append(acc / l)
        o_ref[:, vs] = _subln(o[0], o[1], lam_ref[0], g_ref[...], lam_init).astype(BF)


def _new_rows_spec(t_s):
    return pl.BlockSpec((1, t_s, SEG), lambda b: (0, b, 0))


def _as_lane_rows(x, n_batch):
    return x.reshape(n_batch, -1, HEAD_DIM)


def _lane_rows(ref, t0, nt, slot, slot_width, width, tiled=False):
    per_row = width // HEAD_DIM
    per_slot = slot_width // HEAD_DIM
    n_slots = width // slot_width
    offs = [(k * n_slots + slot) if tiled else (slot * per_slot + k) for k in range(per_slot)]
    parts = [ref[0, pl.ds(t0 * per_row + o, nt, stride=per_row), :] for o in offs]
    return parts[0] if per_slot == 1 else jnp.concatenate(parts, axis=1)


def _tiled_lane_rows(x, n_batch, n_slots):
    rows, w = x.shape[1], x.shape[3]
    x = x.reshape(n_batch, rows, n_slots, w // HEAD_DIM, HEAD_DIM)
    return jnp.transpose(x, (0, 1, 3, 2, 4)).reshape(n_batch, -1, HEAD_DIM)


def _cache_spec(cache):
    return pl.BlockSpec((1,) + cache.shape[1:], lambda b: (b, 0, 0))


def _diff_attn_sample(lam, q, k_new, v_new, cache_k, cache_v, subln_g, n_batch, t_s, past, lam_init):
    kern = functools.partial(_diff_sample_kernel, t_s=t_s, past=past, lam_init=lam_init)
    return pl.pallas_call(
        kern,
        grid=(n_batch,),
        in_specs=[pl.BlockSpec(memory_space=pltpu.SMEM), _new_rows_spec(t_s), _new_rows_spec(t_s),
                  _new_rows_spec(t_s), _cache_spec(cache_k), _cache_spec(cache_v),
                  pl.BlockSpec((1, DIFF_V_DIM), lambda b: (0, 0))],
        out_specs=pl.BlockSpec((t_s, SEG), lambda b: (b, 0)),
        out_shape=jax.ShapeDtypeStruct((n_batch * t_s, SEG), BF),
        compiler_params=_cparams(1),
        name="diff_attn_sample",
    )(lam, q, k_new, v_new, cache_k, cache_v, subln_g.reshape(1, DIFF_V_DIM))


def _sb_neg_log_keep(z2):
    mx = jnp.maximum(z2, 0.0)
    mn = jnp.minimum(z2, 0.0)
    return mx + jnp.log(1.0 + jnp.exp2(mn - mx)) * LOG2E


def _split_bf16(x):
    hi = x.astype(BF)
    return hi, (x - hi.astype(F32)).astype(BF)


def _sb_group(items, carry, acc, key_axis):
    if key_axis == 0:
        zs = [lax.dot_general(k, q, _NT, preferred_element_type=F32) for q, k, _, _, _, _ in items]
    else:
        zs = [lax.dot_general(q, k, _NT, preferred_element_type=F32) for q, k, _, _, _, _ in items]
    splits = []
    for z, (_, _, _, tri, before, _) in zip(zs, items):
        nlk = _sb_neg_log_keep(z)
        if before is not None:
            nlk = jnp.where(before, nlk, 0.0)
        hi, lo = _split_bf16(nlk)
        fused = tri.shape[1 - key_axis] == 2 * tri.shape[key_axis]
        splits.append(jnp.concatenate([hi, lo], axis=key_axis) if fused else (hi, lo))
    sufs = []
    for hl, (_, _, _, tri, _, _) in zip(splits, items):
        if key_axis == 0:
            mm = lambda x: jnp.dot(tri, x, preferred_element_type=F32)
        else:
            mm = lambda x: jnp.dot(x, tri, preferred_element_type=F32)
        sufs.append(mm(hl[0]) + mm(hl[1]) if isinstance(hl, tuple) else mm(hl))
    probs = []
    for z, suf, (_, _, _, _, before, ch) in zip(zs, sufs, items):
        a = jnp.exp2(z - suf - carry[ch])
        if before is not None:
            a = jnp.where(before, a, 0.0)
        probs.append(a.astype(BF))
        carry[ch] = carry[ch] + (suf[0:1, :] if key_axis == 0 else suf[:, 0:1])
    for a, (_, _, v_op, _, _, ch) in zip(probs, items):
        if key_axis == 0:
            acc[ch] = acc[ch] + jnp.dot(v_op, a, preferred_element_type=F32)
        else:
            acc[ch] = acc[ch] + jnp.dot(a, v_op, preferred_element_type=F32)


def _tri(n, upper):
    r = lax.broadcasted_iota(jnp.int32, (n, n), 0)
    c = lax.broadcasted_iota(jnp.int32, (n, n), 1)
    return jnp.where((c >= r) if upper else (r >= c), 1.0, 0.0).astype(BF)


def _sb_prompt_kernel(q_ref, k_ref, v_ref, o_ref, vt_ref, carry_ref, acc_ref, *, tq, sub, t, unroll):
    qi = pl.program_id(2)
    n_sub = tq // sub

    @pl.when(qi == 0)
    def _():
        for kb in range(t // sub):
            vt_ref[kb] = v_ref[0, kb * sub:(kb + 1) * sub, :].astype(F32).T.astype(BF)

    u = _tri(sub, True)
    tri2 = jnp.concatenate([u, u], axis=1)
    r_i = lax.broadcasted_iota(jnp.int32, (sub, sub), 0)
    c_i = lax.broadcasted_iota(jnp.int32, (sub, sub), 1)
    before = r_i < c_i
    qs = [q_ref[0, c * sub:(c + 1) * sub, :] for c in range(n_sub)]
    carry = [jnp.zeros((1, sub), F32) for _ in range(n_sub)]
    acc = [jnp.zeros((HEAD_DIM, sub), F32) for _ in range(n_sub)]

    base = qi * n_sub
    items = []
    for j in reversed(range(n_sub)):
        kblk = k_ref[0, pl.ds(pl.multiple_of((base + j) * sub, sub), sub), :]
        vt_blk = vt_ref[base + j]
        items += [(qs[c], kblk, vt_blk, tri2, before if c == j else None, c) for c in range(j, n_sub)]
    _sb_group(items, carry, acc, 0)
    for c in range(n_sub):
        carry_ref[c] = carry[c]
        acc_ref[c] = acc[c]

    def body(i, x):
        cr = [carry_ref[c] for c in range(n_sub)]
        ac = [acc_ref[c] for c in range(n_sub)]
        items = []
        for uu in range(unroll):
            kb = base - 1 - (i * unroll + uu)
            kblk = k_ref[0, pl.ds(pl.multiple_of(kb * sub, sub), sub), :]
            vt_blk = vt_ref[kb]
            items += [(qs[c], kblk, vt_blk, tri2, None, c) for c in range(n_sub)]
        _sb_group(items, cr, ac, 0)
        for c in range(n_sub):
            carry_ref[c] = cr[c]
            acc_ref[c] = ac[c]
        return x

    lax.fori_loop(0, (qi * n_sub) // unroll, body, 0)
    for c in range(n_sub):
        o_ref[c * sub:(c + 1) * sub, :] = acc_ref[c].T.astype(BF)


def _sb_attn_prompt(q, k, v, n_batch, t, tq=1024, sub=256, unroll=2):
    assert t % tq == 0 and tq % sub == 0 and (tq // sub) % unroll == 0
    nq = t // tq
    n_sub = tq // sub
    kern = functools.partial(_sb_prompt_kernel, tq=tq, sub=sub, t=t, unroll=unroll)
    return pl.pallas_call(
        kern,
        grid=(n_batch, N_SB_HEADS, nq),
        in_specs=[pl.BlockSpec((1, tq, HEAD_DIM), lambda b, h, i: (0, b * nq + i, h)),
                  pl.BlockSpec((1, t, HEAD_DIM), lambda b, h, i: (0, b, h)),
                  pl.BlockSpec((1, t, HEAD_DIM), lambda b, h, i: (0, b, h))],
        out_specs=pl.BlockSpec((tq, HEAD_DIM), lambda b, h, i: (b * nq + i, h)),
        out_shape=jax.ShapeDtypeStruct((n_batch * t, SEG), BF),
        scratch_shapes=[pltpu.VMEM((t // sub, HEAD_DIM, sub), BF), pltpu.VMEM((n_sub, 1, sub), F32),
                        pltpu.VMEM((n_sub, HEAD_DIM, sub), F32)],
        compiler_params=_cparams(3),
        name="sb_attn_prompt",
    )(q, k, v)


def _sb_sample_kernel(q_ref, kn_ref, vn_ref, kc_ref, vc_ref, o_ref, *, t_s, past, tk):
    tri_new = _tri(t_s, False)
    l_old = _tri(tk, False)
    tri_old = jnp.concatenate([l_old, l_old], axis=0)
    before = (lax.broadcasted_iota(jnp.int32, (t_s, t_s), 1) < lax.broadcasted_iota(jnp.int32, (t_s, t_s), 0))
    heads = range(N_SB_HEADS)
    cols = [slice(h * HEAD_DIM, (h + 1) * HEAD_DIM) for h in heads]
    qs = [q_ref[0, :, cs] for cs in cols]
    carry = [jnp.zeros((t_s, 1), F32) for _ in heads]
    acc = [jnp.zeros((t_s, HEAD_DIM), F32) for _ in heads]
    _sb_group([(qs[h], kn_ref[0, :, cols[h]], vn_ref[0, :, cols[h]], tri_new, before, h) for h in heads],
              carry, acc, 1)
    for kb in reversed(range(past // tk)):
        rows = slice(kb * tk, (kb + 1) * tk)
        _sb_group([(qs[h], _lane_rows(kc_ref, kb * tk, tk, h, HEAD_DIM, SEG).astype(BF),
                    _lane_rows(vc_ref, kb * tk, tk, h, HEAD_DIM, SEG).astype(BF), tri_old, None, h)
                   for h in heads], carry, acc, 1)
    for h in heads:
        o_ref[:, cols[h]] = acc[h].astype(BF)


def _sb_attn_sample(q, k_new, v_new, cache_k, cache_v, n_batch, t_s, past, tk=256):
    assert past % tk == 0
    kern = functools.partial(_sb_sample_kernel, t_s=t_s, past=past, tk=tk)
    return pl.pallas_call(
        kern,
        grid=(n_batch,),
        in_specs=[_new_rows_spec(t_s), _new_rows_spec(t_s), _new_rows_spec(t_s),
                  _cache_spec(cache_k), _cache_spec(cache_v)],
        out_specs=pl.BlockSpec((t_s, SEG), lambda b: (b, 0)),
        out_shape=jax.ShapeDtypeStruct((n_batch * t_s, SEG), BF),
        compiler_params=_cparams(1),
        name="sb_attn_sample",
    )(q, k_new, v_new, cache_k, cache_v)


def _cross_kernel(q_ref, k_ref, v_ref, o_ref):
    scale = MEM_HEAD_DIM ** -0.5
    n_mem = k_ref.shape[1] * HEAD_DIM // D_MODEL
    for h in range(N_MEM_HEADS):
        sl = slice(h * MEM_HEAD_DIM, (h + 1) * MEM_HEAD_DIM)
        k = _lane_rows(k_ref, 0, n_mem, h, MEM_HEAD_DIM, D_MODEL, tiled=True).astype(BF)
        v = _lane_rows(v_ref, 0, n_mem, h, MEM_HEAD_DIM, D_MODEL, tiled=True).astype(BF)
        s = lax.dot_general(q_ref[:, sl], k, _NT, preferred_element_type=F32) * scale
        p = jnp.exp(s - jnp.max(s, axis=-1, keepdims=True))
        l = jnp.sum(p, axis=-1, keepdims=True)
        o = jnp.dot(p.astype(BF), v, preferred_element_type=F32) / l
        o_ref[:, sl] = o.astype(BF)


def _cross_attn(q_b, memk, memv, n_batch, t_b, tm):
    nt = t_b // tm
    mem_spec = pl.BlockSpec((1,) + memk.shape[1:], lambda b, i: (b, 0, 0))
    return pl.pallas_call(
        _cross_kernel,
        grid=(n_batch, nt),
        in_specs=[pl.BlockSpec((tm, D_MODEL), lambda b, i: (b * nt + i, 0)), mem_spec, mem_spec],
        out_specs=pl.BlockSpec((tm, D_MODEL), lambda b, i: (b * nt + i, 0)),
        out_shape=jax.ShapeDtypeStruct((n_batch * t_b, D_MODEL), BF),
        compiler_params=_cparams(2),
        name="cross_attn",
    )(q_b, memk, memv)


def _gelu_tanh(x):
    return 0.5 * x * (1.0 + jnp.tanh(math.sqrt(2.0 / math.pi) * (x + 0.044715 * (x * x * x))))


def _ffn_kernel(h_ref, wg_ref, wu_ref, wd_ref, wc_ref, bc_ref, st_ref, g_ref, b_ref,
                o_ref, last_ref, hb_ref, carry_ref, gs_ref, *, tm, fc, seq, n_ff):
    i = pl.program_id(0)
    c = pl.program_id(1)

    @pl.when(c == 0)
    def _():
        hb_ref[...] = h_ref[...].astype(BF)
        o_ref[...] = jnp.zeros(o_ref.shape, F32)

    @pl.when((i == 0) & (c == 0))
    def _():
        carry_ref[...] = jnp.zeros(carry_ref.shape, F32)

    hb = hb_ref[...]
    halves = [slice(k, k + FFN_COLS) for k in range(0, fc, FFN_COLS)]
    gs = [jnp.dot(hb, wg_ref[:, cols], preferred_element_type=F32) for cols in halves]
    us = [jnp.dot(hb, wu_ref[:, cols], preferred_element_type=F32) for cols in halves]
    row = lax.broadcasted_iota(jnp.int32, (tm, FFN_COLS), 0)
    acts = []
    for g, u, cols in zip(gs, us, halves):
        if seq >= tm:
            tiles_per_seq = seq // tm
            at_start = (i % tiles_per_seq) == 0
            prev = jnp.where(at_start, st_ref[0, :, cols], carry_ref[c, :, cols])
            pos = row
            ex0 = prev[0:1]
            ex1 = prev[1:2]
            carry_ref[c, :, cols] = g[tm - 2:tm, :]
            last_ref[0, :, cols] = g[tm - 2:tm, :]
        else:
            ns = tm // seq
            pos = jnp.bitwise_and(row, seq - 1)
            ex0 = jnp.broadcast_to(st_ref[0, :, cols][:, None, :], (ns, seq, FFN_COLS)).reshape(tm, FFN_COLS)
            ex1 = jnp.broadcast_to(st_ref[1, :, cols][:, None, :], (ns, seq, FFN_COLS)).reshape(tm, FFN_COLS)
            for k in range(FFN_COLS // 128):
                lanes = slice(cols.start + k * 128, cols.start + (k + 1) * 128)
                gs_ref[k] = g[:, k * 128:(k + 1) * 128]
                last_ref[0, :, lanes] = gs_ref[k, pl.ds(seq - 2, ns, stride=seq), :]
                last_ref[1, :, lanes] = gs_ref[k, pl.ds(seq - 1, ns, stride=seq), :]
        g_m1 = jnp.where(pos == 0, ex1, pltpu.roll(g, 1, 0))
        g_m2 = jnp.where(pos == 0, ex0, jnp.where(pos == 1, ex1, pltpu.roll(g, 2, 0)))
        conv = (bc_ref[:, cols] + g_m2 * wc_ref[0:1, cols] + g_m1 * wc_ref[1:2, cols] + g * wc_ref[2:3, cols])
        acts.append((_gelu_tanh(conv) * u).astype(BF))
    d = None
    for a, cols in zip(acts, halves):
        dd = jnp.dot(a, wd_ref[cols, :], preferred_element_type=F32)
        d = dd if d is None else d + dd
    o_ref[...] += d

    @pl.when(c == n_ff - 1)
    def _():
        o_ref[...] = _layer_norm(DN_ALPHA * h_ref[...] + o_ref[...], g_ref[...], b_ref[...])


def _ffn(h, w_gate_b, w_up_b, w_down_b, w_conv, b_conv, state, ln_g, ln_b, seq, tm, fc):
    t = h.shape[0]
    n_ff = D_FF // fc
    nt = t // tm
    if seq >= tm:
        tiles_per_seq = seq // tm
        st_spec = pl.BlockSpec((1, 2, fc), lambda i, c: (i // tiles_per_seq, 0, c))
        last_spec = pl.BlockSpec((1, 2, fc), lambda i, c: (i, 0, c))
        last_shape = jax.ShapeDtypeStruct((nt, 2, D_FF), F32)
    else:
        ns = tm // seq
        st_spec = pl.BlockSpec((2, ns, fc), lambda i, c: (0, i, c))
        last_spec = pl.BlockSpec((2, ns, fc), lambda i, c: (0, i, c))
        last_shape = jax.ShapeDtypeStruct((2, t // seq, D_FF), F32)
    kern = functools.partial(_ffn_kernel, tm=tm, fc=fc, seq=seq, n_ff=n_ff)
    return pl.pallas_call(
        kern,
        grid=(nt, n_ff),
        in_specs=[pl.BlockSpec((tm, D_MODEL), lambda i, c: (i, 0), pipeline_mode=pl.Buffered(1)),
                  pl.BlockSpec((D_MODEL, fc), lambda i, c: (0, c)),
                  pl.BlockSpec((D_MODEL, fc), lambda i, c: (0, c)),
                  pl.BlockSpec((fc, D_MODEL), lambda i, c: (c, 0)),
                  pl.BlockSpec((CONV_W, fc), lambda i, c: (0, c)),
                  pl.BlockSpec((1, fc), lambda i, c: (0, c)),
                  st_spec,
                  pl.BlockSpec((1, D_MODEL), lambda i, c: (0, 0)),
                  pl.BlockSpec((1, D_MODEL), lambda i, c: (0, 0))],
        out_specs=[pl.BlockSpec((tm, D_MODEL), lambda i, c: (i, 0)), last_spec],
        out_shape=[jax.ShapeDtypeStruct((t, D_MODEL), F32), last_shape],
        scratch_shapes=[pltpu.VMEM((tm, D_MODEL), BF), pltpu.VMEM((n_ff, 2, fc), F32),
                        pltpu.VMEM((fc // 128, tm, 128), F32)],
        compiler_params=_cparams(2),
        name="conv_ffn",
    )(h, w_gate_b, w_up_b, w_down_b, w_conv, b_conv.reshape(1, D_FF), state,
      ln_g.reshape(1, D_MODEL), ln_b.reshape(1, D_MODEL))


def _after_mixers(x, od, os_, memk, memv, conv_state, n_batch, t_b, w):
    tm = 512
    h1, h1b = _proj_ln([od, os_], w['w_o'], x, w['ln1_g'], w['ln1_b'], tm, "out_proj_ln1")
    (qc,) = _matmul(h1b, w['w_mq'], [BF], tm, 1024, "cross_q")
    oc = _cross_attn(qc, memk, memv, n_batch, t_b, min(t_b, tm))
    h2, _ = _proj_ln([oc], w['w_mo'], h1, w['ln2_g'], w['ln2_b'], tm, "cross_out_ln2")
    return _ffn(h2, w['w_gate'], w['w_up'], w['w_down'], w['w_conv'], w['b_conv'], conv_state,
                w['ln3_g'], w['ln3_b'], t_b, FFN_ROWS, FFN_CHUNK)


def kernel(x_prompt, x_sample, mem_prompt, cache_diff_k, cache_diff_v, cache_sb_k, cache_sb_v, cache_mem_k, cache_mem_v, state_ffn_conv, w_in, lambda_q1, lambda_k1, lambda_q2, lambda_k2, subln_g, w_o, ln1_g, ln1_b, w_mq, w_mk, w_mv, w_mo, ln2_g, ln2_b, w_gate, w_up, w_conv, b_conv, w_down, ln3_g, ln3_b):
    assert w_in.shape[0] == DEPTH == 1
    n_p, t_p, _ = x_prompt.shape
    n_s, t_s, _ = x_sample.shape
    past = cache_diff_k.shape[2]
    n_mem = mem_prompt.shape[1]
    l = 0
    lam_init = 0.8 - 0.6 * math.exp(-0.3 * l)
    lam = (jnp.exp(jnp.sum(lambda_q1[l] * lambda_k1[l])) - jnp.exp(jnp.sum(lambda_q2[l] * lambda_k2[l]))
           + lam_init).reshape(1).astype(F32)
    w = {'w_in': w_in[l].astype(BF), 'w_o': w_o[l].astype(BF), 'w_mq': w_mq[l].astype(BF),
         'w_mo': w_mo[l].astype(BF), 'w_gate': w_gate[l].astype(BF), 'w_up': w_up[l].astype(BF),
         'w_down': w_down[l].astype(BF), 'w_conv': w_conv[l], 'b_conv': b_conv[l],
         'subln_g': subln_g[l], 'ln1_g': ln1_g[l], 'ln1_b': ln1_b[l], 'ln2_g': ln2_g[l], 'ln2_b': ln2_b[l],
         'ln3_g': ln3_g[l], 'ln3_b': ln3_b[l]}
    tm = 512

    w_mkv = jnp.concatenate([w_mk[l], w_mv[l]], axis=1).astype(BF)
    mem_b = mem_prompt.reshape(n_p * n_mem, D_MODEL).astype(BF)
    (mkv_f,) = _matmul(mem_b, w_mkv, [F32], n_p * n_mem, 1024, "memory_kv")
    xp = x_prompt.reshape(n_p * t_p, D_MODEL)
    pb, pf = _inproj(xp, w['w_in'], jnp.tile(jnp.arange(t_p), n_p), tm)
    odp = _diff_attn_prompt(lam, pb['dq'], pb['dk'], pb['dv'], w['subln_g'], n_p, t_p, lam_init)
    osp = _sb_attn_prompt(pb['sq'], pb['sk'], pb['sv'], n_p, t_p)
    zero_state = jnp.zeros((n_p, CONV_W - 1, D_FF), F32)
    mem_heads = (n_p, n_mem, N_MEM_HEADS, MEM_HEAD_DIM)
    yp, lastp = _after_mixers(xp, odp, osp, _tiled_lane_rows(mkv_f[:, :D_MODEL].reshape(mem_heads), n_p, N_MEM_HEADS),
                              _tiled_lane_rows(mkv_f[:, D_MODEL:].reshape(mem_heads), n_p, N_MEM_HEADS),
                              zero_state, n_p, t_p, w)

    xs = x_sample.reshape(n_s * t_s, D_MODEL)
    sb, sf = _inproj(xs, w['w_in'], jnp.tile(past + jnp.arange(t_s), n_s), tm)
    rows = lambda c: _as_lane_rows(c[l], n_s)
    ods = _diff_attn_sample(lam, sb['dq'], sb['dk'], sb['dv'], rows(cache_diff_k),
                            _tiled_lane_rows(cache_diff_v[l], n_s, N_DIFF_HEADS), w['subln_g'],
                            n_s, t_s, past, lam_init)
    oss = _sb_attn_sample(sb['sq'], sb['sk'], sb['sv'], rows(cache_sb_k), rows(cache_sb_v), n_s, t_s, past)
    state_s = jnp.transpose(state_ffn_conv[l], (1, 0, 2))
    ys, lasts = _after_mixers(xs, ods, oss, _tiled_lane_rows(cache_mem_k[l], n_s, N_MEM_HEADS),
                              _tiled_lane_rows(cache_mem_v[l], n_s, N_MEM_HEADS), state_s, n_s, t_s, w)

    tiles_per_seq = t_p // FFN_ROWS
    new_conv_p = lastp[tiles_per_seq - 1::tiles_per_seq]
    new_conv_s = jnp.transpose(lasts, (1, 0, 2))

    def new_rows(f, n, t):
        dv = jnp.transpose(f['dv'].reshape(n, t, DIFF_V_DIM // HEAD_DIM, N_DIFF_HEADS, HEAD_DIM), (0, 1, 3, 2, 4))
        return (f['dk'].reshape(1, n, t, N_DIFF_HEADS, 2, HEAD_DIM), dv.reshape(1, n, t, N_DIFF_HEADS, DIFF_V_DIM),
                f['sk'].reshape(1, n, t, N_SB_HEADS, HEAD_DIM), f['sv'].reshape(1, n, t, N_SB_HEADS, HEAD_DIM))

    return ((yp.reshape(n_p, t_p, D_MODEL), ys.reshape(n_s, t_s, D_MODEL))
            + new_rows(pf, n_p, t_p)
            + (mkv_f[:, :D_MODEL].reshape((1,) + mem_heads), mkv_f[:, D_MODEL:].reshape((1,) + mem_heads),
               new_conv_p[None])
            + new_rows(sf, n_s, t_s)
            + (new_conv_s[None],))
```

```python
import functools
import math

import jax
import jax.numpy as jnp
from jax import lax
from jax.experimental import pallas as pl
from jax.experimental.pallas import tpu as pltpu

BF = jnp.bfloat16
F32 = jnp.float32

D_MODEL = 2048
CHUNK = 64
CHUNK_SHIFT = 6
HEAD_DIM = 128
N_DIFF_HEADS = 4
N_SB_HEADS = 8
DIFF_V_DIM = 2 * HEAD_DIM
ROPE_DIM = HEAD_DIM // 4
ROPE_THETA = 500000.0
N_MEM_HEADS = 4
MEM_HEAD_DIM = D_MODEL // N_MEM_HEADS
D_FF = 5632
CONV_W = 3
LN_EPS = 1e-5
DEPTH = 1
DN_ALPHA = (2 * DEPTH) ** 0.25
SEG = 1024
N_SEG = 6
VMEM_LIMIT = 56 * 1024 * 1024
LOG2E = math.log2(math.e)
QK_SCALE2 = HEAD_DIM ** -0.5 * LOG2E
NEG_BIG = -1e30
LN_ROWS = 128
FFN_COLS = 256
FFN_ROWS = 1024
FFN_CHUNK = 512
INPROJ_ROWS_PROMPT = 1024
MID_ROWS = 256

_NT = (((1,), (1,)), ((), ()))


def _cparams(n_axes):
    return pltpu.CompilerParams(dimension_semantics=("arbitrary",) * n_axes,
                                vmem_limit_bytes=VMEM_LIMIT)


def _layer_norm(y, g, b):
    mu = jnp.mean(y, axis=-1, keepdims=True)
    yc = y - mu
    var = jnp.mean(yc * yc, axis=-1, keepdims=True)
    return yc * lax.rsqrt(var + LN_EPS) * g + b


SLOTS = SEG // HEAD_DIM


def _inproj_kernel(*refs, rope, want_f32, tile_rows, scale):
    xb_ref, w_ref = refs[:2]
    tabs = refs[2:5] if rope else None
    outs = refs[5:] if rope else refs[2:]
    pb_ref = outs[0]
    acc = jnp.dot(xb_ref[...], w_ref[...], preferred_element_type=F32)
    tm = acc.shape[0]
    for ch in range(SLOTS):
        lanes = slice(ch * HEAD_DIM, (ch + 1) * HEAD_DIM)
        r = acc[:, lanes]
        if rope:
            c_ref, s1_ref, s2_ref = tabs
            r = (r * c_ref[...] + pltpu.roll(r, HEAD_DIM - ROPE_DIM // 2, 1) * s1_ref[...]
                 + pltpu.roll(r, ROPE_DIM // 2, 1) * s2_ref[...])
        elif scale != 1.0:
            r = r * scale
        pb_ref[0, :, lanes] = r.astype(BF)
        if want_f32:
            head, col_tile = divmod(ch, SLOTS // tile_rows) if tile_rows else (0, 0)
            off = col_tile * tile_rows + head if tile_rows else ch
            outs[1][pl.ds(off, tm, stride=SLOTS), :] = r


def _rope_tables(pos):
    half = ROPE_DIM // 2
    inv_freq = jnp.power(ROPE_THETA, -jnp.arange(half, dtype=F32) * (2.0 / ROPE_DIM))
    ang = pos.astype(F32)[:, None] * inv_freq[None, :]
    cos, sin = jnp.cos(ang), jnp.sin(ang)
    n = pos.shape[0]
    c = jnp.concatenate([cos, cos, jnp.ones((n, HEAD_DIM - ROPE_DIM), F32)], axis=1)
    s1 = jnp.concatenate([-sin, jnp.zeros((n, HEAD_DIM - half), F32)], axis=1)
    s2 = jnp.concatenate([jnp.zeros((n, half), F32), sin, jnp.zeros((n, HEAD_DIM - ROPE_DIM), F32)], axis=1)
    return c, s1, s2


IN_GROUPS = (("dq", True, False, 0, QK_SCALE2), ("dk", True, True, 0, 1.0), ("dv", False, True, N_DIFF_HEADS, 1.0),
             ("sq", False, False, 0, QK_SCALE2), ("sk", False, True, 0, 1.0), ("sv", False, True, 0, 1.0))


def _inproj(x, w_in_b, pos_rows, tm):
    t = x.shape[0]
    xb = x.astype(BF)
    tabs = _rope_tables(pos_rows)
    row = lambda i: (i, 0)
    tab = pl.BlockSpec((tm, HEAD_DIM), row)
    bf, f32 = {}, {}
    for j, (name, rope, want_f32, tile_rows, scale) in enumerate(IN_GROUPS):
        assert not (want_f32 and scale != 1.0)
        outs = pl.pallas_call(
            functools.partial(_inproj_kernel, rope=rope, want_f32=want_f32, tile_rows=tile_rows, scale=scale),
            grid=(t // tm,),
            in_specs=[pl.BlockSpec((tm, D_MODEL), row),
                      pl.BlockSpec((D_MODEL, SEG), functools.partial(lambda i, j: (0, j), j=j))]
                     + ([tab, tab, tab] if rope else []),
            out_specs=[pl.BlockSpec((1, tm, SEG), lambda i: (0, i, 0))]
                      + ([pl.BlockSpec((tm * SLOTS, HEAD_DIM), row)] if want_f32 else []),
            out_shape=[jax.ShapeDtypeStruct((1, t, SEG), BF)]
                      + ([jax.ShapeDtypeStruct((t * SLOTS, HEAD_DIM), F32)] if want_f32 else []),
            compiler_params=_cparams(1),
            name="inproj_" + name,
        )(xb, w_in_b, *([tb * scale for tb in tabs] if rope else ()))
        bf[name] = outs[0]
        if want_f32:
            f32[name] = outs[1]
    return bf, f32


def _matmul_kernel(x_ref, w_ref, *o_refs):
    acc = jnp.dot(x_ref[...], w_ref[...], preferred_element_type=F32)
    for o in o_refs:
        o[...] = acc.astype(o.dtype)


def _matmul(x_b, w_b, out_dtypes, tm, tn, name):
    t, k = x_b.shape
    n = w_b.shape[1]
    return pl.pallas_call(
        _matmul_kernel,
        grid=(t // tm, n // tn),
        in_specs=[pl.BlockSpec((tm, k), lambda i, j: (i, 0)),
                  pl.BlockSpec((k, tn), lambda i, j: (0, j))],
        out_specs=[pl.BlockSpec((tm, tn), lambda i, j: (i, j)) for _ in out_dtypes],
        out_shape=[jax.ShapeDtypeStruct((t, n), d) for d in out_dtypes],
        compiler_params=_cparams(2),
        name=name,
    )(x_b, w_b)


def _chunk_visible(q0, k0, nq, nk):
    qpos = q0 + lax.broadcasted_iota(jnp.int32, (nq, nk), 0)
    kpos = k0 + lax.broadcasted_iota(jnp.int32, (nq, nk), 1)
    return lax.shift_right_logical(kpos, CHUNK_SHIFT) <= lax.shift_right_logical(qpos, CHUNK_SHIFT)


def _subln(o0, o1, lam, g, lam_init):
    of = o0 - lam * o1
    return of * lax.rsqrt(jnp.mean(of * of, axis=-1, keepdims=True) + LN_EPS) * g * (1.0 - lam_init)


def _diff_prompt_kernel(*refs, tq, sub, lam_init, n_cast):
    lam_ref, q_ref, k_ref, v_ref, g_ref = refs[:5]
    o_ref = refs[5 + n_cast]
    m_ref, l_ref, acc_ref = refs[6 + 2 * n_cast:]
    for w_ref, wb_ref in zip(refs[5:5 + n_cast], refs[6 + n_cast:6 + 2 * n_cast]):
        wb_ref[...] = w_ref[...].astype(BF)
    qi = pl.program_id(2)
    m_ref[...] = jnp.full(m_ref.shape, NEG_BIG, F32)
    l_ref[...] = jnp.zeros(l_ref.shape, F32)
    acc_ref[...] = jnp.zeros(acc_ref.shape, F32)

    def run(k0, items):
        scores = []
        for c, rows, nk, vis in items:
            cs = slice(c * HEAD_DIM, (c + 1) * HEAD_DIM)
            s = lax.dot_general(q_ref[0, rows, cs], k_ref[0, pl.ds(k0, nk), cs], _NT,
                                preferred_element_type=F32)
            if vis is not None:
                tail = jnp.where(vis, s[:, nk - sub:], -jnp.inf)
                s = tail if nk == sub else jnp.concatenate([s[:, :nk - sub], tail], axis=1)
            scores.append(s)
        probs = []
        for s, (c, rows, nk, _) in zip(scores, items):
            m_old = m_ref[c, rows]
            m_new = jnp.maximum(m_old, jnp.max(s, axis=-1, keepdims=True))
            alpha = jnp.exp2(m_old - m_new)
            p = jnp.exp2(s - m_new)
            l_ref[c, rows] = alpha * l_ref[c, rows] + jnp.sum(p, axis=-1, keepdims=True)
            m_ref[c, rows] = m_new
            probs.append((alpha, p.astype(BF)))
        for (alpha, p), (c, rows, nk, _) in zip(probs, items):
            acc_ref[c, rows] = alpha * acc_ref[c, rows] + jnp.dot(p, v_ref[0, pl.ds(k0, nk), :],
                                                                  preferred_element_type=F32)

    full_items = [(c, slice(r * sub, (r + 1) * sub), tq, None) for r in range(tq // sub) for c in range(2)]

    def body(g, carry):
        run(pl.multiple_of(g * tq, tq), full_items)
        return carry

    lax.fori_loop(0, qi, body, 0)

    vis = _chunk_visible(0, 0, sub, sub)
    diag_items = [(c, slice(r * sub, (r + 1) * sub), (r + 1) * sub, vis)
                  for r in range(tq // sub) for c in range(2)]
    run(pl.multiple_of(qi * tq, tq), diag_items)

    o = _subln(acc_ref[0] / l_ref[0], acc_ref[1] / l_ref[1], lam_ref[0], g_ref[...], lam_init)
    o_ref[...] = o.astype(BF)


def _diff_attn_prompt(lam, q, k, v, subln_g, n_batch, t, lam_init, cast_weights=(), tq=1024, sub=256):
    assert t % tq == 0 and tq % sub == 0 and sub % CHUNK == 0
    nq = t // tq
    n_steps = n_batch * N_DIFF_HEADS * nq
    step = lambda b, h, i: ((b * N_DIFF_HEADS + h) * nq + i, 0)
    cast_specs = []
    for wt in cast_weights:
        assert wt.shape[0] % (16 * n_steps) == 0
        cast_specs.append(pl.BlockSpec((wt.shape[0] // n_steps, wt.shape[1]), step))
    kern = functools.partial(_diff_prompt_kernel, tq=tq, sub=sub, lam_init=lam_init, n_cast=len(cast_weights))
    outs = pl.pallas_call(
        kern,
        grid=(n_batch, N_DIFF_HEADS, nq),
        in_specs=[pl.BlockSpec(memory_space=pltpu.SMEM),
                  pl.BlockSpec((1, tq, DIFF_V_DIM), lambda b, h, i: (0, b * nq + i, h)),
                  pl.BlockSpec((1, t, DIFF_V_DIM), lambda b, h, i: (0, b, h)),
                  pl.BlockSpec((1, t, DIFF_V_DIM), lambda b, h, i: (0, b, h)),
                  pl.BlockSpec((1, DIFF_V_DIM), lambda b, h, i: (0, 0))] + cast_specs,
        out_specs=[pl.BlockSpec((tq, DIFF_V_DIM), lambda b, h, i: (b * nq + i, h))] + cast_specs,
        out_shape=[jax.ShapeDtypeStruct((n_batch * t, SEG), BF)]
                  + [jax.ShapeDtypeStruct(wt.shape, BF) for wt in cast_weights],
        scratch_shapes=[pltpu.VMEM((2, tq, 1), F32), pltpu.VMEM((2, tq, 1), F32),
                        pltpu.VMEM((2, tq, DIFF_V_DIM), F32)],
        compiler_params=_cparams(3),
        name="diff_attn_prompt",
    )(lam, q, k, v, subln_g.reshape(1, DIFF_V_DIM), *cast_weights)
    return outs[0], outs[1:]


def _diff_sample_kernel(lam_ref, q_ref, kn_ref, vn_ref, kc_ref, vc_ref, g_ref, o_ref, *, t_s, past, lam_init):
    vis = _chunk_visible(past, past, t_s, t_s)
    for h in range(N_DIFF_HEADS):
        vs = slice(h * DIFF_V_DIM, (h + 1) * DIFF_V_DIM)
        v_old = _lane_rows(vc_ref, 0, past, h, DIFF_V_DIM, SEG, tiled=True).astype(BF)
        v_new = vn_ref[0, :, vs]
        o = []
        for c in range(2):
            cs = slice((2 * h + c) * HEAD_DIM, (2 * h + c + 1) * HEAD_DIM)
            q = q_ref[0, :, cs]
            k_old = _lane_rows(kc_ref, 0, past, 2 * h + c, HEAD_DIM, SEG).astype(BF)
            s_old = lax.dot_general(q, k_old, _NT, preferred_element_type=F32)
            s_new = lax.dot_general(q, kn_ref[0, :, cs], _NT, preferred_element_type=F32)
            s_new = jnp.where(vis, s_new, -jnp.inf)
            m = jnp.maximum(jnp.max(s_old, axis=-1, keepdims=True), jnp.max(s_new, axis=-1, keepdims=True))
            p_old = jnp.exp2(s_old - m)
            p_new = jnp.exp2(s_new - m)
            l = jnp.sum(p_old, axis=-1, keepdims=True) + jnp.sum(p_new, axis=-1, keepdims=True)
            acc = (jnp.dot(p_old.astype(BF), v_old, preferred_element_type=F32)
                   + jnp.dot(p_new.astype(BF), v_new, preferred_element_type=F32))
            o.append(acc / l)
        o_ref[:, vs] = _subln(o[0], o[1], lam_ref[0], g_ref[...], lam_init).astype(BF)


def _new_rows_spec(t_s):
    return pl.BlockSpec((1, t_s, SEG), lambda b: (0, b, 0))


def _as_lane_rows(x, n_batch):
    return x.reshape(n_batch, -1, HEAD_DIM)


def _lane_rows(ref, t0, nt, slot, slot_width, width, tiled=False):
    per_row = width // HEAD_DIM
    per_slot = slot_width // HEAD_DIM
    n_slots = width // slot_width
    offs = [(k * n_slots + slot) if tiled else (slot * per_slot + k) for k in range(per_slot)]
    parts = [ref[0, pl.ds(t0 * per_row + o, nt, stride=per_row), :] for o in offs]
    return parts[0] if per_slot == 1 else jnp.concatenate(parts, axis=1)


def _tiled_lane_rows(x, n_batch, n_slots):
    rows, w = x.shape[1], x.shape[3]
    x = x.reshape(n_batch, rows, n_slots, w // HEAD_DIM, HEAD_DIM)
    return jnp.transpose(x, (0, 1, 3, 2, 4)).reshape(n_batch, -1, HEAD_DIM)


def _cache_spec(cache):
    return pl.BlockSpec((1,) + cache.shape[1:], lambda b: (b, 0, 0))


def _diff_attn_sample(lam, q, k_new, v_new, cache_k, cache_v, subln_g, n_batch, t_s, past, lam_init):
    kern = functools.partial(_diff_sample_kernel, t_s=t_s, past=past, lam_init=lam_init)
    return pl.pallas_call(
        kern,
        grid=(n_batch,),
        in_specs=[pl.BlockSpec(memory_space=pltpu.SMEM), _new_rows_spec(t_s), _new_rows_spec(t_s),
                  _new_rows_spec(t_s), _cache_spec(cache_k), _cache_spec(cache_v),
                  pl.BlockSpec((1, DIFF_V_DIM), lambda b: (0, 0))],
        out_specs=pl.BlockSpec((t_s, SEG), lambda b: (b, 0)),
        out_shape=jax.ShapeDtypeStruct((n_batch * t_s, SEG), BF),
        compiler_params=_cparams(1),
        name="diff_attn_sample",
    )(lam, q, k_new, v_new, cache_k, cache_v, subln_g.reshape(1, DIFF_V_DIM))


def _sb_neg_log_keep(z2):
    mx = jnp.maximum(z2, 0.0)
    mn = jnp.minimum(z2, 0.0)
    return mx + jnp.log(1.0 + jnp.exp2(mn - mx)) * LOG2E


def _split_bf16(x):
    hi = x.astype(BF)
    return hi, (x - hi.astype(F32)).astype(BF)


def _sb_group(items, carry, acc, key_axis):
    if key_axis == 0:
        zs = [lax.dot_general(k, q, _NT, preferred_element_type=F32) for q, k, _, _, _, _ in items]
    else:
        zs = [lax.dot_general(q, k, _NT, preferred_element_type=F32) for q, k, _, _, _, _ in items]
    splits = []
    for z, (_, _, _, tri, before, _) in zip(zs, items):
        nlk = _sb_neg_log_keep(z)
        if before is not None:
            nlk = jnp.where(before, nlk, 0.0)
        hi, lo = _split_bf16(nlk)
        fused = tri.shape[1 - key_axis] == 2 * tri.shape[key_axis]
        splits.append(jnp.concatenate([hi, lo], axis=key_axis) if fused else (hi, lo))
    sufs = []
    for hl, (_, _, _, tri, _, _) in zip(splits, items):
        if key_axis == 0:
            mm = lambda x: jnp.dot(tri, x, preferred_element_type=F32)
        else:
            mm = lambda x: jnp.dot(x, tri, preferred_element_type=F32)
        sufs.append(mm(hl[0]) + mm(hl[1]) if isinstance(hl, tuple) else mm(hl))
    probs = []
    for z, suf, (_, _, _, _, before, ch) in zip(zs, sufs, items):
        a = jnp.exp2(z - suf - carry[ch])
        if before is not None:
            a = jnp.where(before, a, 0.0)
        probs.append(a.astype(BF))
        carry[ch] = carry[ch] + (suf[0:1, :] if key_axis == 0 else suf[:, 0:1])
    for a, (_, _, v_op, _, _, ch) in zip(probs, items):
        if key_axis == 0:
            acc[ch] = acc[ch] + jnp.dot(v_op, a, preferred_element_type=F32)
        else:
            acc[ch] = acc[ch] + jnp.dot(a, v_op, preferred_element_type=F32)


def _tri(n, upper):
    r = lax.broadcasted_iota(jnp.int32, (n, n), 0)
    c = lax.broadcasted_iota(jnp.int32, (n, n), 1)
    return jnp.where((c >= r) if upper else (r >= c), 1.0, 0.0).astype(BF)


def _sb_prompt_kernel(q_ref, k_ref, v_ref, o_ref, vt_ref, carry_ref, acc_ref, *, tq, sub, t, unroll):
    qi = pl.program_id(2)
    n_sub = tq // sub

    @pl.when(qi == 0)
    def _():
        for kb in range(t // sub):
            vt_ref[kb] = v_ref[0, kb * sub:(kb + 1) * sub, :].astype(F32).T.astype(BF)

    u = _tri(sub, True)
    tri2 = jnp.concatenate([u, u], axis=1)
    r_i = lax.broadcasted_iota(jnp.int32, (sub, sub), 0)
    c_i = lax.broadcasted_iota(jnp.int32, (sub, sub), 1)
    before = r_i < c_i
    qs = [q_ref[0, c * sub:(c + 1) * sub, :] for c in range(n_sub)]
    carry = [jnp.zeros((1, sub), F32) for _ in range(n_sub)]
    acc = [jnp.zeros((HEAD_DIM, sub), F32) for _ in range(n_sub)]

    base = qi * n_sub
    items = []
    for j in reversed(range(n_sub)):
        kblk = k_ref[0, pl.ds(pl.multiple_of((base + j) * sub, sub), sub), :]
        vt_blk = vt_ref[base + j]
        items += [(qs[c], kblk, vt_blk, tri2, before if c == j else None, c) for c in range(j, n_sub)]
    _sb_group(items, carry, acc, 0)
    for c in range(n_sub):
        carry_ref[c] = carry[c]
        acc_ref[c] = acc[c]

    def body(i, x):
        cr = [carry_ref[c] for c in range(n_sub)]
        ac = [acc_ref[c] for c in range(n_sub)]
        items = []
        for uu in range(unroll):
            kb = base - 1 - (i * unroll + uu)
            kblk = k_ref[0, pl.ds(pl.multiple_of(kb * sub, sub), sub), :]
            vt_blk = vt_ref[kb]
            items += [(qs[c], kblk, vt_blk, tri2, None, c) for c in range(n_sub)]
        _sb_group(items, cr, ac, 0)
        for c in range(n_sub):
            carry_ref[c] = cr[c]
            acc_ref[c] = ac[c]
        return x

    lax.fori_loop(0, (qi * n_sub) // unroll, body, 0)
    for c in range(n_sub):
        o_ref[c * sub:(c + 1) * sub, :] = acc_ref[c].T.astype(BF)


def _sb_attn_prompt(q, k, v, n_batch, t, tq=1024, sub=256, unroll=2):
    assert t % tq == 0 and tq % sub == 0 and (tq // sub) % unroll == 0
    nq = t // tq
    n_sub = tq // sub
    kern = functools.partial(_sb_prompt_kernel, tq=tq, sub=sub, t=t, unroll=unroll)
    return pl.pallas_call(
        kern,
        grid=(n_batch, N_SB_HEADS, nq),
        in_specs=[pl.BlockSpec((1, tq, HEAD_DIM), lambda b, h, i: (0, b * nq + i, h)),
                  pl.BlockSpec((1, t, HEAD_DIM), lambda b, h, i: (0, b, h)),
                  pl.BlockSpec((1, t, HEAD_DIM), lambda b, h, i: (0, b, h))],
        out_specs=pl.BlockSpec((tq, HEAD_DIM), lambda b, h, i: (b * nq + i, h)),
        out_shape=jax.ShapeDtypeStruct((n_batch * t, SEG), BF),
        scratch_shapes=[pltpu.VMEM((t // sub, HEAD_DIM, sub), BF), pltpu.VMEM((n_sub, 1, sub), F32),
                        pltpu.VMEM((n_sub, HEAD_DIM, sub), F32)],
        compiler_params=_cparams(3),
        name="sb_attn_prompt",
    )(q, k, v)


def _sb_sample_kernel(q_ref, kn_ref, vn_ref, kc_ref, vc_ref, o_ref, *, t_s, past, tk):
    tri_new = _tri(t_s, False)
    l_old = _tri(tk, False)
    tri_old = jnp.concatenate([l_old, l_old], axis=0)
    before = (lax.broadcasted_iota(jnp.int32, (t_s, t_s), 1) < lax.broadcasted_iota(jnp.int32, (t_s, t_s), 0))
    heads = range(N_SB_HEADS)
    cols = [slice(h * HEAD_DIM, (h + 1) * HEAD_DIM) for h in heads]
    qs = [q_ref[0, :, cs] for cs in cols]
    carry = [jnp.zeros((t_s, 1), F32) for _ in heads]
    acc = [jnp.zeros((t_s, HEAD_DIM), F32) for _ in heads]
    _sb_group([(qs[h], kn_ref[0, :, cols[h]], vn_ref[0, :, cols[h]], tri_new, before, h) for h in heads],
              carry, acc, 1)
    for kb in reversed(range(past // tk)):
        rows = slice(kb * tk, (kb + 1) * tk)
        _sb_group([(qs[h], _lane_rows(kc_ref, kb * tk, tk, h, HEAD_DIM, SEG).astype(BF),
                    _lane_rows(vc_ref, kb * tk, tk, h, HEAD_DIM, SEG).astype(BF), tri_old, None, h)
                   for h in heads], carry, acc, 1)
    for h in heads:
        o_ref[:, cols[h]] = acc[h].astype(BF)


def _sb_attn_sample(q, k_new, v_new, cache_k, cache_v, n_batch, t_s, past, tk=256):
    assert past % tk == 0
    kern = functools.partial(_sb_sample_kernel, t_s=t_s, past=past, tk=tk)
    return pl.pallas_call(
        kern,
        grid=(n_batch,),
        in_specs=[_new_rows_spec(t_s), _new_rows_spec(t_s), _new_rows_spec(t_s),
                  _cache_spec(cache_k), _cache_spec(cache_v)],
        out_specs=pl.BlockSpec((t_s, SEG), lambda b: (b, 0)),
        out_shape=jax.ShapeDtypeStruct((n_batch * t_s, SEG), BF),
        compiler_params=_cparams(1),
        name="sb_attn_sample",
    )(q, k_new, v_new, cache_k, cache_v)


MEM_SCALE2 = MEM_HEAD_DIM ** -0.5 * LOG2E


def _mid_kernel(od_ref, os_ref, x_ref, wo0_ref, wo1_ref, g1_ref, b1_ref, wq_ref, mk_ref, mv_ref, wmo_ref,
                g2_ref, b2_ref, o_ref):
    tm = x_ref.shape[0]
    n_mem = mk_ref.shape[1] * HEAD_DIM // D_MODEL
    slices = [slice(r, r + min(LN_ROWS, tm)) for r in range(0, tm, LN_ROWS)]
    heads = [slice(h * MEM_HEAD_DIM, (h + 1) * MEM_HEAD_DIM) for h in range(N_MEM_HEADS)]
    acc1 = [jnp.dot(od_ref[rows, :], wo0_ref[...], preferred_element_type=F32)
            + jnp.dot(os_ref[rows, :], wo1_ref[...], preferred_element_type=F32) for rows in slices]
    h1 = [_layer_norm(DN_ALPHA * x_ref[rows, :] + a, g1_ref[...], b1_ref[...]) for rows, a in zip(slices, acc1)]
    qs = [(jnp.dot(h.astype(BF), wq_ref[...], preferred_element_type=F32) * MEM_SCALE2).astype(BF) for h in h1]
    ks = [_lane_rows(mk_ref, 0, n_mem, h, MEM_HEAD_DIM, D_MODEL, tiled=True).astype(BF) for h in range(N_MEM_HEADS)]
    vs = [_lane_rows(mv_ref, 0, n_mem, h, MEM_HEAD_DIM, D_MODEL, tiled=True).astype(BF) for h in range(N_MEM_HEADS)]
    scores = [[lax.dot_general(q[:, hs], k, _NT, preferred_element_type=F32) for hs, k in zip(heads, ks)] for q in qs]
    ocs = []
    for per_head in scores:
        outs = []
        for s, v in zip(per_head, vs):
            p = jnp.exp2(s - jnp.max(s, axis=-1, keepdims=True))
            l = jnp.sum(p, axis=-1, keepdims=True)
            outs.append((jnp.dot(p.astype(BF), v, preferred_element_type=F32) / l).astype(BF))
        ocs.append(jnp.concatenate(outs, axis=1))
    acc2 = [jnp.dot(oc, wmo_ref[...], preferred_element_type=F32) for oc in ocs]
    for rows, h, a in zip(slices, h1, acc2):
        o_ref[rows, :] = _layer_norm(DN_ALPHA * h + a, g2_ref[...], b2_ref[...])


def _mid(od, os_, x, memk, memv, n_batch, t_b, tm, w):
    nt = t_b // tm
    row = lambda b, i: (b * nt + i, 0)
    const = lambda b, i: (0, 0)
    once = dict(pipeline_mode=pl.Buffered(1))
    vec = pl.BlockSpec((1, D_MODEL), const)
    w_sq = pl.BlockSpec((D_MODEL, D_MODEL), const, **once)
    mem_spec = pl.BlockSpec((1,) + memk.shape[1:], lambda b, i: (b, 0, 0), **once)
    return pl.pallas_call(
        _mid_kernel,
        grid=(n_batch, nt),
        in_specs=[pl.BlockSpec((tm, SEG), row), pl.BlockSpec((tm, SEG), row), pl.BlockSpec((tm, D_MODEL), row),
                  pl.BlockSpec((SEG, D_MODEL), const, **once), pl.BlockSpec((SEG, D_MODEL), lambda b, i: (1, 0), **once),
                  vec, vec, w_sq, mem_spec, mem_spec, w_sq, vec, vec],
        out_specs=pl.BlockSpec((tm, D_MODEL), row),
        out_shape=jax.ShapeDtypeStruct((n_batch * t_b, D_MODEL), F32),
        compiler_params=_cparams(2),
        name="proj_cross_ln",
    )(od, os_, x, w['w_o'], w['w_o'], w['ln1_g'].reshape(1, D_MODEL), w['ln1_b'].reshape(1, D_MODEL), w['w_mq'],
      memk, memv, w['w_mo'], w['ln2_g'].reshape(1, D_MODEL), w['ln2_b'].reshape(1, D_MODEL))


def _gelu_tanh(x):
    return 0.5 * x * (1.0 + jnp.tanh(math.sqrt(2.0 / math.pi) * (x + 0.044715 * (x * x * x))))


def _ffn_kernel(h_ref, wg_ref, wu_ref, wd_ref, wc_ref, bc_ref, st_ref, g_ref, b_ref,
                o_ref, last_ref, hb_ref, carry_ref, gs_ref, *, tm, fc, seq, n_ff):
    i = pl.program_id(0)
    c = pl.program_id(1)

    @pl.when(c == 0)
    def _():
        hb_ref[...] = h_ref[...].astype(BF)
        o_ref[...] = jnp.zeros(o_ref.shape, F32)

    @pl.when((i == 0) & (c == 0))
    def _():
        carry_ref[...] = jnp.zeros(carry_ref.shape, F32)

    hb = hb_ref[...]
    halves = [slice(k, k + FFN_COLS) for k in range(0, fc, FFN_COLS)]
    gs = [jnp.dot(hb, wg_ref[:, cols], preferred_element_type=F32) for cols in halves]
    us = [jnp.dot(hb, wu_ref[:, cols], preferred_element_type=F32) for cols in halves]
    row = lax.broadcasted_iota(jnp.int32, (tm, FFN_COLS), 0)
    acts = []
    for g, u, cols in zip(gs, us, halves):
        if seq >= tm:
            tiles_per_seq = seq // tm
            at_start = (i % tiles_per_seq) == 0
            prev = jnp.where(at_start, st_ref[0, :, cols], carry_ref[c, :, cols])
            pos = row
            ex0 = prev[0:1]
            ex1 = prev[1:2]
            carry_ref[c, :, cols] = g[tm - 2:tm, :]
            last_ref[0, :, cols] = g[tm - 2:tm, :]
        else:
            ns = tm // seq
            pos = jnp.bitwise_and(row, seq - 1)
            ex0 = jnp.broadcast_to(st_ref[0, :, cols][:, None, :], (ns, seq, FFN_COLS)).reshape(tm, FFN_COLS)
            ex1 = jnp.broadcast_to(st_ref[1, :, cols][:, None, :], (ns, seq, FFN_COLS)).reshape(tm, FFN_COLS)
            for k in range(FFN_COLS // 128):
                lanes = slice(cols.start + k * 128, cols.start + (k + 1) * 128)
                gs_ref[k] = g[:, k * 128:(k + 1) * 128]
                last_ref[0, :, lanes] = gs_ref[k, pl.ds(seq - 2, ns, stride=seq), :]
                last_ref[1, :, lanes] = gs_ref[k, pl.ds(seq - 1, ns, stride=seq), :]
        g_m1 = jnp.where(pos == 0, ex1, pltpu.roll(g, 1, 0))
        g_m2 = jnp.where(pos == 0, ex0, jnp.where(pos == 1, ex1, pltpu.roll(g, 2, 0)))
        conv = (bc_ref[:, cols] + g_m2 * wc_ref[0:1, cols] + g_m1 * wc_ref[1:2, cols] + g * wc_ref[2:3, cols])
        acts.append((_gelu_tanh(conv) * u).astype(BF))
    d = None
    for a, cols in zip(acts, halves):
        dd = jnp.dot(a, wd_ref[cols, :], preferred_element_type=F32)
        d = dd if d is None else d + dd
    o_ref[...] += d

    @pl.when(c == n_ff - 1)
    def _():
        o_ref[...] = _layer_norm(DN_ALPHA * h_ref[...] + o_ref[...], g_ref[...], b_ref[...])


def _ffn(h, w_gate_b, w_up_b, w_down_b, w_conv, b_conv, state, ln_g, ln_b, seq, tm, fc):
    t = h.shape[0]
    n_ff = D_FF // fc
    nt = t // tm
    if seq >= tm:
        tiles_per_seq = seq // tm
        st_spec = pl.BlockSpec((1, 2, fc), lambda i, c: (i // tiles_per_seq, 0, c))
        last_spec = pl.BlockSpec((1, 2, fc), lambda i, c: (i, 0, c))
        last_shape = jax.ShapeDtypeStruct((nt, 2, D_FF), F32)
    else:
        ns = tm // seq
        st_spec = pl.BlockSpec((2, ns, fc), lambda i, c: (0, i, c))
        last_spec = pl.BlockSpec((2, ns, fc), lambda i, c: (0, i, c))
        last_shape = jax.ShapeDtypeStruct((2, t // seq, D_FF), F32)
    kern = functools.partial(_ffn_kernel, tm=tm, fc=fc, seq=seq, n_ff=n_ff)
    return pl.pallas_call(
        kern,
        grid=(nt, n_ff),
        in_specs=[pl.BlockSpec((tm, D_MODEL), lambda i, c: (i, 0), pipeline_mode=pl.Buffered(1)),
                  pl.BlockSpec((D_MODEL, fc), lambda i, c: (0, c)),
                  pl.BlockSpec((D_MODEL, fc), lambda i, c: (0, c)),
                  pl.BlockSpec((fc, D_MODEL), lambda i, c: (c, 0)),
                  pl.BlockSpec((CONV_W, fc), lambda i, c: (0, c)),
                  pl.BlockSpec((1, fc), lambda i, c: (0, c)),
                  st_spec,
                  pl.BlockSpec((1, D_MODEL), lambda i, c: (0, 0)),
                  pl.BlockSpec((1, D_MODEL), lambda i, c: (0, 0))],
        out_specs=[pl.BlockSpec((tm, D_MODEL), lambda i, c: (i, 0)), last_spec],
        out_shape=[jax.ShapeDtypeStruct((t, D_MODEL), F32), last_shape],
        scratch_shapes=[pltpu.VMEM((tm, D_MODEL), BF), pltpu.VMEM((n_ff, 2, fc), F32),
                        pltpu.VMEM((fc // 128, tm, 128), F32)],
        compiler_params=_cparams(2),
        name="conv_ffn",
    )(h, w_gate_b, w_up_b, w_down_b, w_conv, b_conv.reshape(1, D_FF), state,
      ln_g.reshape(1, D_MODEL), ln_b.reshape(1, D_MODEL))


def _after_mixers(x, od, os_, memk, memv, conv_state, n_batch, t_b, w):
    h2 = _mid(od, os_, x, memk, memv, n_batch, t_b, min(t_b, MID_ROWS), w)
    return _ffn(h2, w['w_gate'], w['w_up'], w['w_down'], w['w_conv'], w['b_conv'], conv_state,
                w['ln3_g'], w['ln3_b'], t_b, FFN_ROWS, FFN_CHUNK)


def kernel(x_prompt, x_sample, mem_prompt, cache_diff_k, cache_diff_v, cache_sb_k, cache_sb_v, cache_mem_k, cache_mem_v, state_ffn_conv, w_in, lambda_q1, lambda_k1, lambda_q2, lambda_k2, subln_g, w_o, ln1_g, ln1_b, w_mq, w_mk, w_mv, w_mo, ln2_g, ln2_b, w_gate, w_up, w_conv, b_conv, w_down, ln3_g, ln3_b):
    assert w_in.shape[0] == DEPTH == 1
    n_p, t_p, _ = x_prompt.shape
    n_s, t_s, _ = x_sample.shape
    past = cache_diff_k.shape[2]
    n_mem = mem_prompt.shape[1]
    l = 0
    lam_init = 0.8 - 0.6 * math.exp(-0.3 * l)
    lam = (jnp.exp(jnp.sum(lambda_q1[l] * lambda_k1[l])) - jnp.exp(jnp.sum(lambda_q2[l] * lambda_k2[l]))
           + lam_init).reshape(1).astype(F32)
    w = {'w_in': w_in[l].astype(BF), 'w_conv': w_conv[l], 'b_conv': b_conv[l],
         'subln_g': subln_g[l], 'ln1_g': ln1_g[l], 'ln1_b': ln1_b[l], 'ln2_g': ln2_g[l], 'ln2_b': ln2_b[l],
         'ln3_g': ln3_g[l], 'ln3_b': ln3_b[l]}
    late_weights = {'w_o': w_o[l], 'w_mq': w_mq[l], 'w_mo': w_mo[l], 'w_gate': w_gate[l], 'w_up': w_up[l],
                    'w_down': w_down[l]}
    tm = 512

    w_mkv = jnp.concatenate([w_mk[l], w_mv[l]], axis=1).astype(BF)
    mem_b = mem_prompt.reshape(n_p * n_mem, D_MODEL).astype(BF)
    (mkv_f,) = _matmul(mem_b, w_mkv, [F32], n_p * n_mem, 1024, "memory_kv")
    xp = x_prompt.reshape(n_p * t_p, D_MODEL)
    pb, pf = _inproj(xp, w['w_in'], jnp.tile(jnp.arange(t_p), n_p), INPROJ_ROWS_PROMPT)
    odp, late_b = _diff_attn_prompt(lam, pb['dq'], pb['dk'], pb['dv'], w['subln_g'], n_p, t_p, lam_init,
                                    cast_weights=tuple(late_weights.values()))
    w.update(zip(late_weights, late_b))
    osp = _sb_attn_prompt(pb['sq'], pb['sk'], pb['sv'], n_p, t_p)
    zero_state = jnp.zeros((n_p, CONV_W - 1, D_FF), F32)
    mem_heads = (n_p, n_mem, N_MEM_HEADS, MEM_HEAD_DIM)
    yp, lastp = _after_mixers(xp, odp, osp, _tiled_lane_rows(mkv_f[:, :D_MODEL].reshape(mem_heads), n_p, N_MEM_HEADS),
                              _tiled_lane_rows(mkv_f[:, D_MODEL:].reshape(mem_heads), n_p, N_MEM_HEADS),
                              zero_state, n_p, t_p, w)

    xs = x_sample.reshape(n_s * t_s, D_MODEL)
    sb, sf = _inproj(xs, w['w_in'], jnp.tile(past + jnp.arange(t_s), n_s), tm)
    rows = lambda c: _as_lane_rows(c[l], n_s)
    ods = _diff_attn_sample(lam, sb['dq'], sb['dk'], sb['dv'], rows(cache_diff_k),
                            _tiled_lane_rows(cache_diff_v[l], n_s, N_DIFF_HEADS), w['subln_g'],
                            n_s, t_s, past, lam_init)
    oss = _sb_attn_sample(sb['sq'], sb['sk'], sb['sv'], rows(cache_sb_k), rows(cache_sb_v), n_s, t_s, past)
    state_s = jnp.transpose(state_ffn_conv[l], (1, 0, 2))
    ys, lasts = _after_mixers(xs, ods, oss, _tiled_lane_rows(cache_mem_k[l], n_s, N_MEM_HEADS),
                              _tiled_lane_rows(cache_mem_v[l], n_s, N_MEM_HEADS), state_s, n_s, t_s, w)

    tiles_per_seq = t_p // FFN_ROWS
    new_conv_p = lastp[tiles_per_seq - 1::tiles_per_seq]
    new_conv_s = jnp.transpose(lasts, (1, 0, 2))

    def new_rows(f, n, t):
        dv = jnp.transpose(f['dv'].reshape(n, t, DIFF_V_DIM // HEAD_DIM, N_DIFF_HEADS, HEAD_DIM), (0, 1, 3, 2, 4))
        return (f['dk'].reshape(1, n, t, N_DIFF_HEADS, 2, HEAD_DIM), dv.reshape(1, n, t, N_DIFF_HEADS, DIFF_V_DIM),
                f['sk'].reshape(1, n, t, N_SB_HEADS, HEAD_DIM), f['sv'].reshape(1, n, t, N_SB_HEADS, HEAD_DIM))

    return ((yp.reshape(n_p, t_p, D_MODEL), ys.reshape(n_s, t_s, D_MODEL))
            + new_rows(pf, n_p, t_p)
            + (mkv_f[:, :D_MODEL].reshape((1,) + mem_heads), mkv_f[:, D_MODEL:].reshape((1,) + mem_heads),
               new_conv_p[None])
            + new_rows(sf, n_s, t_s)
            + (new_conv_s[None],))
```

```python
import functools
import math

import jax
import jax.numpy as jnp
import numpy as np
from jax import lax
from jax.experimental import pallas as pl
from jax.experimental.pallas import tpu as pltpu

BF = jnp.bfloat16
F32 = jnp.float32

D_MODEL = 2048
CHUNK = 64
CHUNK_SHIFT = 6
HEAD_DIM = 128
N_DIFF_HEADS = 4
N_SB_HEADS = 8
DIFF_V_DIM = 2 * HEAD_DIM
ROPE_DIM = HEAD_DIM // 4
ROPE_THETA = 500000.0
N_MEM_HEADS = 4
MEM_HEAD_DIM = D_MODEL // N_MEM_HEADS
D_FF = 5632
CONV_W = 3
LN_EPS = 1e-5
DEPTH = 1
DN_ALPHA = (2 * DEPTH) ** 0.25
SEG = 1024
N_SEG = 6
VMEM_LIMIT = 56 * 1024 * 1024
LOG2E = math.log2(math.e)
QK_SCALE2 = HEAD_DIM ** -0.5 * LOG2E
NEG_BIG = -1e30
LN_ROWS = 128
FFN_COLS = 256
FFN_ROWS = 1024
FFN_CHUNK = 512
INPROJ_ROWS_PROMPT = 1024
MID_ROWS = 512

_NT = (((1,), (1,)), ((), ()))


def _cparams(n_axes):
    return pltpu.CompilerParams(dimension_semantics=("arbitrary",) * n_axes,
                                vmem_limit_bytes=VMEM_LIMIT)


def _layer_norm(y, g, b):
    mu = jnp.mean(y, axis=-1, keepdims=True)
    yc = y - mu
    var = jnp.mean(yc * yc, axis=-1, keepdims=True)
    return yc * lax.rsqrt(var + LN_EPS) * g + b


SLOTS = SEG // HEAD_DIM


def _inproj_kernel(*refs, rope, want_f32, tile_rows, scale, emit_xb):
    x_ref, w_ref = refs[:2]
    tabs = refs[2:5] if rope else None
    outs = refs[5:] if rope else refs[2:]
    pb_ref = outs[0]
    xb = x_ref[...].astype(BF)
    if emit_xb:
        outs[-1][...] = xb
    acc = jnp.dot(xb, w_ref[...], preferred_element_type=F32)
    tm = acc.shape[0]
    for ch in range(SLOTS):
        lanes = slice(ch * HEAD_DIM, (ch + 1) * HEAD_DIM)
        r = acc[:, lanes]
        if rope:
            c_ref, s1_ref, s2_ref = tabs
            r = (r * c_ref[...] + pltpu.roll(r, HEAD_DIM - ROPE_DIM // 2, 1) * s1_ref[...]
                 + pltpu.roll(r, ROPE_DIM // 2, 1) * s2_ref[...])
        elif scale != 1.0:
            r = r * scale
        pb_ref[0, :, lanes] = r.astype(BF)
        if want_f32:
            head, col_tile = divmod(ch, SLOTS // tile_rows) if tile_rows else (0, 0)
            off = col_tile * tile_rows + head if tile_rows else ch
            outs[1][pl.ds(off, tm, stride=SLOTS), :] = r


def _rope_tables(pos, n_rep):
    half = ROPE_DIM // 2
    inv_freq = np.power(ROPE_THETA, -np.arange(half, dtype=np.float64) * (2.0 / ROPE_DIM))
    ang = pos.astype(np.float64)[:, None] * inv_freq[None, :]
    cos, sin = jnp.asarray(np.cos(ang), F32), jnp.asarray(np.sin(ang), F32)
    n = pos.shape[0]
    c = jnp.concatenate([cos, cos, jnp.ones((n, HEAD_DIM - ROPE_DIM), F32)], axis=1)
    s1 = jnp.concatenate([-sin, jnp.zeros((n, HEAD_DIM - half), F32)], axis=1)
    s2 = jnp.concatenate([jnp.zeros((n, half), F32), sin, jnp.zeros((n, HEAD_DIM - ROPE_DIM), F32)], axis=1)
    return tuple(jnp.tile(tb, (n_rep, 1)) for tb in (c, s1, s2))


IN_GROUPS = (("dq", True, False, 0, QK_SCALE2), ("dk", True, True, 0, 1.0), ("dv", False, True, N_DIFF_HEADS, 1.0),
             ("sq", False, False, 0, QK_SCALE2), ("sk", False, True, 0, 1.0), ("sv", False, True, 0, 1.0))


def _inproj(x, w_in_b, pos, n_seq, tm):
    t = x.shape[0]
    tabs = _rope_tables(pos, n_seq)
    row = lambda i: (i, 0)
    tab = pl.BlockSpec((tm, HEAD_DIM), row)
    bf, f32 = {}, {}
    xb = None
    for j, (name, rope, want_f32, tile_rows, scale) in enumerate(IN_GROUPS):
        assert not (want_f32 and scale != 1.0)
        emit_xb = xb is None
        outs = pl.pallas_call(
            functools.partial(_inproj_kernel, rope=rope, want_f32=want_f32, tile_rows=tile_rows, scale=scale,
                              emit_xb=emit_xb),
            grid=(t // tm,),
            in_specs=[pl.BlockSpec((tm, D_MODEL), row),
                      pl.BlockSpec((D_MODEL, SEG), functools.partial(lambda i, j: (0, j), j=j))]
                     + ([tab, tab, tab] if rope else []),
            out_specs=[pl.BlockSpec((1, tm, SEG), lambda i: (0, i, 0))]
                      + ([pl.BlockSpec((tm * SLOTS, HEAD_DIM), row)] if want_f32 else [])
                      + ([pl.BlockSpec((tm, D_MODEL), row)] if emit_xb else []),
            out_shape=[jax.ShapeDtypeStruct((1, t, SEG), BF)]
                      + ([jax.ShapeDtypeStruct((t * SLOTS, HEAD_DIM), F32)] if want_f32 else [])
                      + ([jax.ShapeDtypeStruct((t, D_MODEL), BF)] if emit_xb else []),
            compiler_params=_cparams(1),
            name="inproj_" + name,
        )(x if emit_xb else xb, w_in_b, *([tb * scale for tb in tabs] if rope else ()))
        bf[name] = outs[0]
        if want_f32:
            f32[name] = outs[1]
        if emit_xb:
            xb = outs[-1]
    return bf, f32


def _matmul_kernel(x_ref, w_ref, *o_refs):
    acc = jnp.dot(x_ref[...], w_ref[...], preferred_element_type=F32)
    for o in o_refs:
        o[...] = acc.astype(o.dtype)


def _matmul(x_b, w_b, out_dtypes, tm, tn, name):
    t, k = x_b.shape
    n = w_b.shape[1]
    return pl.pallas_call(
        _matmul_kernel,
        grid=(t // tm, n // tn),
        in_specs=[pl.BlockSpec((tm, k), lambda i, j: (i, 0)),
                  pl.BlockSpec((k, tn), lambda i, j: (0, j))],
        out_specs=[pl.BlockSpec((tm, tn), lambda i, j: (i, j)) for _ in out_dtypes],
        out_shape=[jax.ShapeDtypeStruct((t, n), d) for d in out_dtypes],
        compiler_params=_cparams(2),
        name=name,
    )(x_b, w_b)


def _chunk_visible(q0, k0, nq, nk):
    qpos = q0 + lax.broadcasted_iota(jnp.int32, (nq, nk), 0)
    kpos = k0 + lax.broadcasted_iota(jnp.int32, (nq, nk), 1)
    return lax.shift_right_logical(kpos, CHUNK_SHIFT) <= lax.shift_right_logical(qpos, CHUNK_SHIFT)


def _subln(o0, o1, lam, g, lam_init):
    of = o0 - lam * o1
    return of * lax.rsqrt(jnp.mean(of * of, axis=-1, keepdims=True) + LN_EPS) * g * (1.0 - lam_init)


def _diff_prompt_kernel(*refs, tq, sub, lam_init, n_cast):
    lam_ref, q_ref, k_ref, v_ref, g_ref = refs[:5]
    o_ref = refs[5 + n_cast]
    m_ref, l_ref, acc_ref = refs[6 + 2 * n_cast:]
    for w_ref, wb_ref in zip(refs[5:5 + n_cast], refs[6 + n_cast:6 + 2 * n_cast]):
        wb_ref[...] = w_ref[...].astype(BF)
    qi = pl.program_id(2)
    m_ref[...] = jnp.full(m_ref.shape, NEG_BIG, F32)
    l_ref[...] = jnp.zeros(l_ref.shape, F32)
    acc_ref[...] = jnp.zeros(acc_ref.shape, F32)

    def run(k0, items):
        scores = []
        for c, rows, nk, vis in items:
            cs = slice(c * HEAD_DIM, (c + 1) * HEAD_DIM)
            s = lax.dot_general(q_ref[0, rows, cs], k_ref[0, pl.ds(k0, nk), cs], _NT,
                                preferred_element_type=F32)
            if vis is not None:
                tail = jnp.where(vis, s[:, nk - sub:], -jnp.inf)
                s = tail if nk == sub else jnp.concatenate([s[:, :nk - sub], tail], axis=1)
            scores.append(s)
        probs = []
        for s, (c, rows, nk, _) in zip(scores, items):
            m_old = m_ref[c, rows]
            m_new = jnp.maximum(m_old, jnp.max(s, axis=-1, keepdims=True))
            alpha = jnp.exp2(m_old - m_new)
            p = jnp.exp2(s - m_new)
            l_ref[c, rows] = alpha * l_ref[c, rows] + jnp.sum(p, axis=-1, keepdims=True)
            m_ref[c, rows] = m_new
            probs.append((alpha, p.astype(BF)))
        for (alpha, p), (c, rows, nk, _) in zip(probs, items):
            acc_ref[c, rows] = alpha * acc_ref[c, rows] + jnp.dot(p, v_ref[0, pl.ds(k0, nk), :],
                                                                  preferred_element_type=F32)

    full_items = [(c, slice(r * sub, (r + 1) * sub), tq, None) for r in range(tq // sub) for c in range(2)]

    def body(g, carry):
        run(pl.multiple_of(g * tq, tq), full_items)
        return carry

    lax.fori_loop(0, qi, body, 0)

    vis = _chunk_visible(0, 0, sub, sub)
    diag_items = [(c, slice(r * sub, (r + 1) * sub), (r + 1) * sub, vis)
                  for r in range(tq // sub) for c in range(2)]
    run(pl.multiple_of(qi * tq, tq), diag_items)

    o = _subln(acc_ref[0] / l_ref[0], acc_ref[1] / l_ref[1], lam_ref[0], g_ref[...], lam_init)
    o_ref[...] = o.astype(BF)


def _diff_attn_prompt(lam, q, k, v, subln_g, n_batch, t, lam_init, cast_weights=(), tq=1024, sub=256):
    assert t % tq == 0 and tq % sub == 0 and sub % CHUNK == 0
    nq = t // tq
    n_steps = n_batch * N_DIFF_HEADS * nq
    step = lambda b, h, i: ((b * N_DIFF_HEADS + h) * nq + i, 0)
    cast_specs = []
    for wt in cast_weights:
        assert wt.shape[0] % (16 * n_steps) == 0
        cast_specs.append(pl.BlockSpec((wt.shape[0] // n_steps, wt.shape[1]), step))
    kern = functools.partial(_diff_prompt_kernel, tq=tq, sub=sub, lam_init=lam_init, n_cast=len(cast_weights))
    outs = pl.pallas_call(
        kern,
        grid=(n_batch, N_DIFF_HEADS, nq),
        in_specs=[pl.BlockSpec(memory_space=pltpu.SMEM),
                  pl.BlockSpec((1, tq, DIFF_V_DIM), lambda b, h, i: (0, b * nq + i, h)),
                  pl.BlockSpec((1, t, DIFF_V_DIM), lambda b, h, i: (0, b, h)),
                  pl.BlockSpec((1, t, DIFF_V_DIM), lambda b, h, i: (0, b, h)),
                  pl.BlockSpec((1, DIFF_V_DIM), lambda b, h, i: (0, 0))] + cast_specs,
        out_specs=[pl.BlockSpec((tq, DIFF_V_DIM), lambda b, h, i: (b * nq + i, h))] + cast_specs,
        out_shape=[jax.ShapeDtypeStruct((n_batch * t, SEG), BF)]
                  + [jax.ShapeDtypeStruct(wt.shape, BF) for wt in cast_weights],
        scratch_shapes=[pltpu.VMEM((2, tq, 1), F32), pltpu.VMEM((2, tq, 1), F32),
                        pltpu.VMEM((2, tq, DIFF_V_DIM), F32)],
        compiler_params=_cparams(3),
        name="diff_attn_prompt",
    )(lam, q, k, v, subln_g.reshape(1, DIFF_V_DIM), *cast_weights)
    return outs[0], outs[1:]


def _diff_sample_kernel(lam_ref, q_ref, kn_ref, vn_ref, kc_ref, vc_ref, g_ref, o_ref, *, t_s, past, lam_init):
    vis = _chunk_visible(past, past, t_s, t_s)
    for h in range(N_DIFF_HEADS):
        vs = slice(h * DIFF_V_DIM, (h + 1) * DIFF_V_DIM)
        v_old = _lane_rows(vc_ref, 0, past, h, DIFF_V_DIM, SEG, tiled=True).astype(BF)
        v_new = vn_ref[0, :, vs]
        o = []
        for c in range(2):
            cs = slice((2 * h + c) * HEAD_DIM, (2 * h + c + 1) * HEAD_DIM)
            q = q_ref[0, :, cs]
            k_old = _lane_rows(kc_ref, 0, past, 2 * h + c, HEAD_DIM, SEG).astype(BF)
            s_old = lax.dot_general(q, k_old, _NT, preferred_element_type=F32)
            s_new = lax.dot_general(q, kn_ref[0, :, cs], _NT, preferred_element_type=F32)
            s_new = jnp.where(vis, s_new, -jnp.inf)
            m = jnp.maximum(jnp.max(s_old, axis=-1, keepdims=True), jnp.max(s_new, axis=-1, keepdims=True))
            p_old = jnp.exp2(s_old - m)
            p_new = jnp.exp2(s_new - m)
            l = jnp.sum(p_old, axis=-1, keepdims=True) + jnp.sum(p_new, axis=-1, keepdims=True)
            acc = (jnp.dot(p_old.astype(BF), v_old, preferred_element_type=F32)
                   + jnp.dot(p_new.astype(BF), v_new, preferred_element_type=F32))
            o.append(acc / l)
        o_ref[:, vs] = _subln(o[0], o[1], lam_ref[0], g_ref[...], lam_init).astype(BF)


def _new_rows_spec(t_s):
    return pl.BlockSpec((1, t_s, SEG), lambda b: (0, b, 0))


def _as_lane_rows(x, n_batch):
    return x.reshape(n_batch, -1, HEAD_DIM)


def _lane_rows(ref, t0, nt, slot, slot_width, width, tiled=False):
    per_row = width // HEAD_DIM
    per_slot = slot_width // HEAD_DIM
    n_slots = width // slot_width
    offs = [(k * n_slots + slot) if tiled else (slot * per_slot + k) for k in range(per_slot)]
    parts = [ref[0, pl.ds(t0 * per_row + o, nt, stride=per_row), :] for o in offs]
    return parts[0] if per_slot == 1 else jnp.concatenate(parts, axis=1)


def _tiled_lane_rows(x, n_batch, n_slots):
    rows, w = x.shape[1], x.shape[3]
    x = x.reshape(n_batch, rows, n_slots, w // HEAD_DIM, HEAD_DIM)
    return jnp.transpose(x, (0, 1, 3, 2, 4)).reshape(n_batch, -1, HEAD_DIM)


def _cache_spec(cache):
    return pl.BlockSpec((1,) + cache.shape[1:], lambda b: (b, 0, 0))


def _diff_attn_sample(lam, q, k_new, v_new, cache_k, cache_v, subln_g, n_batch, t_s, past, lam_init):
    kern = functools.partial(_diff_sample_kernel, t_s=t_s, past=past, lam_init=lam_init)
    return pl.pallas_call(
        kern,
        grid=(n_batch,),
        in_specs=[pl.BlockSpec(memory_space=pltpu.SMEM), _new_rows_spec(t_s), _new_rows_spec(t_s),
                  _new_rows_spec(t_s), _cache_spec(cache_k), _cache_spec(cache_v),
                  pl.BlockSpec((1, DIFF_V_DIM), lambda b: (0, 0))],
        out_specs=pl.BlockSpec((t_s, SEG), lambda b: (b, 0)),
        out_shape=jax.ShapeDtypeStruct((n_batch * t_s, SEG), BF),
        compiler_params=_cparams(1),
        name="diff_attn_sample",
    )(lam, q, k_new, v_new, cache_k, cache_v, subln_g.reshape(1, DIFF_V_DIM))


def _sb_neg_log_keep(z2):
    mx = jnp.maximum(z2, 0.0)
    mn = jnp.minimum(z2, 0.0)
    return mx + jnp.log(1.0 + jnp.exp2(mn - mx)) * LOG2E


def _split_bf16(x):
    hi = x.astype(BF)
    return hi, (x - hi.astype(F32)).astype(BF)


def _sb_group(items, carry, acc, key_axis):
    if key_axis == 0:
        zs = [lax.dot_general(k, q, _NT, preferred_element_type=F32) for q, k, _, _, _, _ in items]
    else:
        zs = [lax.dot_general(q, k, _NT, preferred_element_type=F32) for q, k, _, _, _, _ in items]
    splits = []
    for z, (_, _, _, tri, before, _) in zip(zs, items):
        nlk = _sb_neg_log_keep(z)
        if before is not None:
            nlk = jnp.where(before, nlk, 0.0)
        hi, lo = _split_bf16(nlk)
        fused = tri.shape[1 - key_axis] == 2 * tri.shape[key_axis]
        splits.append(jnp.concatenate([hi, lo], axis=key_axis) if fused else (hi, lo))
    sufs = []
    for hl, (_, _, _, tri, _, _) in zip(splits, items):
        if key_axis == 0:
            mm = lambda x: jnp.dot(tri, x, preferred_element_type=F32)
        else:
            mm = lambda x: jnp.dot(x, tri, preferred_element_type=F32)
        sufs.append(mm(hl[0]) + mm(hl[1]) if isinstance(hl, tuple) else mm(hl))
    probs = []
    for z, suf, (_, _, _, _, before, ch) in zip(zs, sufs, items):
        a = jnp.exp2(z - suf - carry[ch])
        if before is not None:
            a = jnp.where(before, a, 0.0)
        probs.append(a.astype(BF))
        carry[ch] = carry[ch] + (suf[0:1, :] if key_axis == 0 else suf[:, 0:1])
    for a, (_, _, v_op, _, _, ch) in zip(probs, items):
        if key_axis == 0:
            acc[ch] = acc[ch] + jnp.dot(v_op, a, preferred_element_type=F32)
        else:
            acc[ch] = acc[ch] + jnp.dot(a, v_op, preferred_element_type=F32)


def _tri(n, upper):
    r = lax.broadcasted_iota(jnp.int32, (n, n), 0)
    c = lax.broadcasted_iota(jnp.int32, (n, n), 1)
    return jnp.where((c >= r) if upper else (r >= c), 1.0, 0.0).astype(BF)


def _sb_prompt_kernel(q_ref, k_ref, v_ref, o_ref, vt_ref, carry_ref, acc_ref, *, tq, sub, t, unroll):
    qi = pl.program_id(2)
    n_sub = tq // sub

    @pl.when(qi == 0)
    def _():
        for kb in range(t // sub):
            vt_ref[kb] = v_ref[0, kb * sub:(kb + 1) * sub, :].astype(F32).T.astype(BF)

    u = _tri(sub, True)
    tri2 = jnp.concatenate([u, u], axis=1)
    r_i = lax.broadcasted_iota(jnp.int32, (sub, sub), 0)
    c_i = lax.broadcasted_iota(jnp.int32, (sub, sub), 1)
    before = r_i < c_i
    qs = [q_ref[0, c * sub:(c + 1) * sub, :] for c in range(n_sub)]
    carry = [jnp.zeros((1, sub), F32) for _ in range(n_sub)]
    acc = [jnp.zeros((HEAD_DIM, sub), F32) for _ in range(n_sub)]

    base = qi * n_sub
    items = []
    for j in reversed(range(n_sub)):
        kblk = k_ref[0, pl.ds(pl.multiple_of((base + j) * sub, sub), sub), :]
        vt_blk = vt_ref[base + j]
        items += [(qs[c], kblk, vt_blk, tri2, before if c == j else None, c) for c in range(j, n_sub)]
    _sb_group(items, carry, acc, 0)
    for c in range(n_sub):
        carry_ref[c] = carry[c]
        acc_ref[c] = acc[c]

    def body(i, x):
        cr = [carry_ref[c] for c in range(n_sub)]
        ac = [acc_ref[c] for c in range(n_sub)]
        items = []
        for uu in range(unroll):
            kb = base - 1 - (i * unroll + uu)
            kblk = k_ref[0, pl.ds(pl.multiple_of(kb * sub, sub), sub), :]
            vt_blk = vt_ref[kb]
            items += [(qs[c], kblk, vt_blk, tri2, None, c) for c in range(n_sub)]
        _sb_group(items, cr, ac, 0)
        for c in range(n_sub):
            carry_ref[c] = cr[c]
            acc_ref[c] = ac[c]
        return x

    lax.fori_loop(0, (qi * n_sub) // unroll, body, 0)
    for c in range(n_sub):
        o_ref[c * sub:(c + 1) * sub, :] = acc_ref[c].T.astype(BF)


def _sb_attn_prompt(q, k, v, n_batch, t, tq=1024, sub=256, unroll=2):
    assert t % tq == 0 and tq % sub == 0 and (tq // sub) % unroll == 0
    nq = t // tq
    n_sub = tq // sub
    kern = functools.partial(_sb_prompt_kernel, tq=tq, sub=sub, t=t, unroll=unroll)
    return pl.pallas_call(
        kern,
        grid=(n_batch, N_SB_HEADS, nq),
        in_specs=[pl.BlockSpec((1, tq, HEAD_DIM), lambda b, h, i: (0, b * nq + i, h)),
                  pl.BlockSpec((1, t, HEAD_DIM), lambda b, h, i: (0, b, h)),
                  pl.BlockSpec((1, t, HEAD_DIM), lambda b, h, i: (0, b, h))],
        out_specs=pl.BlockSpec((tq, HEAD_DIM), lambda b, h, i: (b * nq + i, h)),
        out_shape=jax.ShapeDtypeStruct((n_batch * t, SEG), BF),
        scratch_shapes=[pltpu.VMEM((t // sub, HEAD_DIM, sub), BF), pltpu.VMEM((n_sub, 1, sub), F32),
                        pltpu.VMEM((n_sub, HEAD_DIM, sub), F32)],
        compiler_params=_cparams(3),
        name="sb_attn_prompt",
    )(q, k, v)


def _sb_sample_kernel(q_ref, kn_ref, vn_ref, kc_ref, vc_ref, o_ref, *, t_s, past, tk):
    tri_new = _tri(t_s, False)
    l_old = _tri(tk, False)
    tri_old = jnp.concatenate([l_old, l_old], axis=0)
    before = (lax.broadcasted_iota(jnp.int32, (t_s, t_s), 1) < lax.broadcasted_iota(jnp.int32, (t_s, t_s), 0))
    heads = range(N_SB_HEADS)
    cols = [slice(h * HEAD_DIM, (h + 1) * HEAD_DIM) for h in heads]
    qs = [q_ref[0, :, cs] for cs in cols]
    carry = [jnp.zeros((t_s, 1), F32) for _ in heads]
    acc = [jnp.zeros((t_s, HEAD_DIM), F32) for _ in heads]
    _sb_group([(qs[h], kn_ref[0, :, cols[h]], vn_ref[0, :, cols[h]], tri_new, before, h) for h in heads],
              carry, acc, 1)
    for kb in reversed(range(past // tk)):
        rows = slice(kb * tk, (kb + 1) * tk)
        _sb_group([(qs[h], _lane_rows(kc_ref, kb * tk, tk, h, HEAD_DIM, SEG).astype(BF),
                    _lane_rows(vc_ref, kb * tk, tk, h, HEAD_DIM, SEG).astype(BF), tri_old, None, h)
                   for h in heads], carry, acc, 1)
    for h in heads:
        o_ref[:, cols[h]] = acc[h].astype(BF)


def _sb_attn_sample(q, k_new, v_new, cache_k, cache_v, n_batch, t_s, past, tk=256):
    assert past % tk == 0
    kern = functools.partial(_sb_sample_kernel, t_s=t_s, past=past, tk=tk)
    return pl.pallas_call(
        kern,
        grid=(n_batch,),
        in_specs=[_new_rows_spec(t_s), _new_rows_spec(t_s), _new_rows_spec(t_s),
                  _cache_spec(cache_k), _cache_spec(cache_v)],
        out_specs=pl.BlockSpec((t_s, SEG), lambda b: (b, 0)),
        out_shape=jax.ShapeDtypeStruct((n_batch * t_s, SEG), BF),
        compiler_params=_cparams(1),
        name="sb_attn_sample",
    )(q, k_new, v_new, cache_k, cache_v)


MEM_SCALE2 = MEM_HEAD_DIM ** -0.5 * LOG2E


def _proj_ln_kernel(*refs, n_in, with_q):
    a_refs = refs[:n_in]
    w_refs = refs[n_in:2 * n_in]
    res_ref, g_ref, b_ref = refs[2 * n_in:2 * n_in + 3]
    rest = refs[2 * n_in + 3:]
    tm = res_ref.shape[0]
    slices = [slice(r, r + LN_ROWS) for r in range(0, tm, LN_ROWS)]
    accs = []
    for rows in slices:
        acc = None
        for a, w in zip(a_refs, w_refs):
            d = jnp.dot(a[rows, :], w[...], preferred_element_type=F32)
            acc = d if acc is None else acc + d
        accs.append(acc)
    outs = [_layer_norm(DN_ALPHA * res_ref[rows, :] + acc, g_ref[...], b_ref[...]) for rows, acc in zip(slices, accs)]
    if with_q:
        wq_ref, of_ref, oq_ref = rest
        for rows, out in zip(slices, outs):
            of_ref[rows, :] = out
        for rows, out in zip(slices, outs):
            q = jnp.dot(out.astype(BF), wq_ref[...], preferred_element_type=F32)
            oq_ref[rows, :] = (q * MEM_SCALE2).astype(BF)
    else:
        (of_ref,) = rest
        for rows, out in zip(slices, outs):
            of_ref[rows, :] = out


def _proj_ln(a_list, w_b, res, g, b, tm, name, wq_b=None):
    t = res.shape[0]
    n_in = len(a_list)
    ka = a_list[0].shape[1]
    row = lambda i: (i, 0)
    once = dict(pipeline_mode=pl.Buffered(1))
    full = pl.BlockSpec((tm, D_MODEL), row)
    vec = pl.BlockSpec((1, D_MODEL), lambda i: (0, 0))
    in_specs = ([pl.BlockSpec((tm, ka), row) for _ in a_list]
                + [pl.BlockSpec((ka, D_MODEL), functools.partial(lambda i, r: (r, 0), r=r), **once) for r in range(n_in)]
                + [full, vec, vec])
    args = [*a_list, *([w_b] * n_in), res, g.reshape(1, D_MODEL), b.reshape(1, D_MODEL)]
    out_specs = [full]
    out_shape = [jax.ShapeDtypeStruct((t, D_MODEL), F32)]
    if wq_b is not None:
        in_specs.append(pl.BlockSpec((D_MODEL, D_MODEL), lambda i: (0, 0), **once))
        args.append(wq_b)
        out_specs.append(full)
        out_shape.append(jax.ShapeDtypeStruct((t, D_MODEL), BF))
    return pl.pallas_call(
        functools.partial(_proj_ln_kernel, n_in=n_in, with_q=wq_b is not None),
        grid=(t // tm,),
        in_specs=in_specs,
        out_specs=out_specs,
        out_shape=out_shape,
        compiler_params=_cparams(1),
        name=name,
    )(*args)


def _cross_kernel(q_ref, k_ref, v_ref, o_ref):
    n_mem = k_ref.shape[1] * HEAD_DIM // D_MODEL
    for h in range(N_MEM_HEADS):
        sl = slice(h * MEM_HEAD_DIM, (h + 1) * MEM_HEAD_DIM)
        k = _lane_rows(k_ref, 0, n_mem, h, MEM_HEAD_DIM, D_MODEL, tiled=True).astype(BF)
        v = _lane_rows(v_ref, 0, n_mem, h, MEM_HEAD_DIM, D_MODEL, tiled=True).astype(BF)
        s = lax.dot_general(q_ref[:, sl], k, _NT, preferred_element_type=F32)
        p = jnp.exp2(s - jnp.max(s, axis=-1, keepdims=True))
        l = jnp.sum(p, axis=-1, keepdims=True)
        o = jnp.dot(p.astype(BF), v, preferred_element_type=F32) / l
        o_ref[:, sl] = o.astype(BF)


def _cross_attn(q_b, memk, memv, n_batch, t_b, tm):
    nt = t_b // tm
    mem_spec = pl.BlockSpec((1,) + memk.shape[1:], lambda b, i: (b, 0, 0))
    return pl.pallas_call(
        _cross_kernel,
        grid=(n_batch, nt),
        in_specs=[pl.BlockSpec((tm, D_MODEL), lambda b, i: (b * nt + i, 0)), mem_spec, mem_spec],
        out_specs=pl.BlockSpec((tm, D_MODEL), lambda b, i: (b * nt + i, 0)),
        out_shape=jax.ShapeDtypeStruct((n_batch * t_b, D_MODEL), BF),
        compiler_params=_cparams(2),
        name="cross_attn",
    )(q_b, memk, memv)


def _gelu_tanh(x):
    return 0.5 * x * (1.0 + jnp.tanh(math.sqrt(2.0 / math.pi) * (x + 0.044715 * (x * x * x))))


def _ffn_kernel(h_ref, wg_ref, wu_ref, wd_ref, wc_ref, bc_ref, st_ref, g_ref, b_ref,
                o_ref, last_ref, hb_ref, carry_ref, gs_ref, *, tm, fc, seq, n_ff):
    i = pl.program_id(0)
    c = pl.program_id(1)

    @pl.when(c == 0)
    def _():
        hb_ref[...] = h_ref[...].astype(BF)
        o_ref[...] = jnp.zeros(o_ref.shape, F32)

    @pl.when((i == 0) & (c == 0))
    def _():
        carry_ref[...] = jnp.zeros(carry_ref.shape, F32)

    hb = hb_ref[...]
    halves = [slice(k, k + FFN_COLS) for k in range(0, fc, FFN_COLS)]
    gs = [jnp.dot(hb, wg_ref[:, cols], preferred_element_type=F32) for cols in halves]
    us = [jnp.dot(hb, wu_ref[:, cols], preferred_element_type=F32) for cols in halves]
    row = lax.broadcasted_iota(jnp.int32, (tm, FFN_COLS), 0)
    acts = []
    for g, u, cols in zip(gs, us, halves):
        if seq >= tm:
            tiles_per_seq = seq // tm
            at_start = (i % tiles_per_seq) == 0
            prev = jnp.where(at_start, st_ref[0, :, cols], carry_ref[c, :, cols])
            pos = row
            ex0 = prev[0:1]
            ex1 = prev[1:2]
            carry_ref[c, :, cols] = g[tm - 2:tm, :]
            last_ref[0, :, cols] = g[tm - 2:tm, :]
        else:
            ns = tm // seq
            pos = jnp.bitwise_and(row, seq - 1)
            ex0 = jnp.broadcast_to(st_ref[0, :, cols][:, None, :], (ns, seq, FFN_COLS)).reshape(tm, FFN_COLS)
            ex1 = jnp.broadcast_to(st_ref[1, :, cols][:, None, :], (ns, seq, FFN_COLS)).reshape(tm, FFN_COLS)
            for k in range(FFN_COLS // 128):
                lanes = slice(cols.start + k * 128, cols.start + (k + 1) * 128)
                gs_ref[k] = g[:, k * 128:(k + 1) * 128]
                last_ref[0, :, lanes] = gs_ref[k, pl.ds(seq - 2, ns, stride=seq), :]
                last_ref[1, :, lanes] = gs_ref[k, pl.ds(seq - 1, ns, stride=seq), :]
        g_m1 = jnp.where(pos == 0, ex1, pltpu.roll(g, 1, 0))
        g_m2 = jnp.where(pos == 0, ex0, jnp.where(pos == 1, ex1, pltpu.roll(g, 2, 0)))
        conv = (bc_ref[:, cols] + g_m2 * wc_ref[0:1, cols] + g_m1 * wc_ref[1:2, cols] + g * wc_ref[2:3, cols])
        acts.append((_gelu_tanh(conv) * u).astype(BF))
    d = None
    for a, cols in zip(acts, halves):
        dd = jnp.dot(a, wd_ref[cols, :], preferred_element_type=F32)
        d = dd if d is None else d + dd
    o_ref[...] += d

    @pl.when(c == n_ff - 1)
    def _():
        o_ref[...] = _layer_norm(DN_ALPHA * h_ref[...] + o_ref[...], g_ref[...], b_ref[...])


def _ffn(h, w_gate_b, w_up_b, w_down_b, w_conv, b_conv, state, ln_g, ln_b, seq, tm, fc):
    t = h.shape[0]
    n_ff = D_FF // fc
    nt = t // tm
    if seq >= tm:
        tiles_per_seq = seq // tm
        st_spec = pl.BlockSpec((1, 2, fc), lambda i, c: (i // tiles_per_seq, 0, c))
        last_spec = pl.BlockSpec((1, 2, fc), lambda i, c: (i, 0, c))
        last_shape = jax.ShapeDtypeStruct((nt, 2, D_FF), F32)
    else:
        ns = tm // seq
        st_spec = pl.BlockSpec((2, ns, fc), lambda i, c: (0, i, c))
        last_spec = pl.BlockSpec((2, ns, fc), lambda i, c: (0, i, c))
        last_shape = jax.ShapeDtypeStruct((2, t // seq, D_FF), F32)
    kern = functools.partial(_ffn_kernel, tm=tm, fc=fc, seq=seq, n_ff=n_ff)
    return pl.pallas_call(
        kern,
        grid=(nt, n_ff),
        in_specs=[pl.BlockSpec((tm, D_MODEL), lambda i, c: (i, 0), pipeline_mode=pl.Buffered(1)),
                  pl.BlockSpec((D_MODEL, fc), lambda i, c: (0, c)),
                  pl.BlockSpec((D_MODEL, fc), lambda i, c: (0, c)),
                  pl.BlockSpec((fc, D_MODEL), lambda i, c: (c, 0)),
                  pl.BlockSpec((CONV_W, fc), lambda i, c: (0, c)),
                  pl.BlockSpec((1, fc), lambda i, c: (0, c)),
                  st_spec,
                  pl.BlockSpec((1, D_MODEL), lambda i, c: (0, 0)),
                  pl.BlockSpec((1, D_MODEL), lambda i, c: (0, 0))],
        out_specs=[pl.BlockSpec((tm, D_MODEL), lambda i, c: (i, 0)), last_spec],
        out_shape=[jax.ShapeDtypeStruct((t, D_MODEL), F32), last_shape],
        scratch_shapes=[pltpu.VMEM((tm, D_MODEL), BF), pltpu.VMEM((n_ff, 2, fc), F32),
                        pltpu.VMEM((fc // 128, tm, 128), F32)],
        compiler_params=_cparams(2),
        name="conv_ffn",
    )(h, w_gate_b, w_up_b, w_down_b, w_conv, b_conv.reshape(1, D_FF), state,
      ln_g.reshape(1, D_MODEL), ln_b.reshape(1, D_MODEL))


def _after_mixers(x, od, os_, memk, memv, conv_state, n_batch, t_b, w):
    h1, qc = _proj_ln([od, os_], w['w_o'], x, w['ln1_g'], w['ln1_b'], MID_ROWS, "out_proj_ln1_q", wq_b=w['w_mq'])
    oc = _cross_attn(qc, memk, memv, n_batch, t_b, min(t_b, MID_ROWS))
    (h2,) = _proj_ln([oc], w['w_mo'], h1, w['ln2_g'], w['ln2_b'], MID_ROWS, "cross_out_ln2")
    return _ffn(h2, w['w_gate'], w['w_up'], w['w_down'], w['w_conv'], w['b_conv'], conv_state,
                w['ln3_g'], w['ln3_b'], t_b, FFN_ROWS, FFN_CHUNK)


def kernel(x_prompt, x_sample, mem_prompt, cache_diff_k, cache_diff_v, cache_sb_k, cache_sb_v, cache_mem_k, cache_mem_v, state_ffn_conv, w_in, lambda_q1, lambda_k1, lambda_q2, lambda_k2, subln_g, w_o, ln1_g, ln1_b, w_mq, w_mk, w_mv, w_mo, ln2_g, ln2_b, w_gate, w_up, w_conv, b_conv, w_down, ln3_g, ln3_b):
    assert w_in.shape[0] == DEPTH == 1
    n_p, t_p, _ = x_prompt.shape
    n_s, t_s, _ = x_sample.shape
    past = cache_diff_k.shape[2]
    n_mem = mem_prompt.shape[1]
    l = 0
    lam_init = 0.8 - 0.6 * math.exp(-0.3 * l)
    lam = (jnp.exp(jnp.sum(lambda_q1[l] * lambda_k1[l])) - jnp.exp(jnp.sum(lambda_q2[l] * lambda_k2[l]))
           + lam_init).reshape(1).astype(F32)
    w = {'w_in': w_in[l].astype(BF), 'w_conv': w_conv[l], 'b_conv': b_conv[l],
         'subln_g': subln_g[l], 'ln1_g': ln1_g[l], 'ln1_b': ln1_b[l], 'ln2_g': ln2_g[l], 'ln2_b': ln2_b[l],
         'ln3_g': ln3_g[l], 'ln3_b': ln3_b[l]}
    late_weights = {'w_o': w_o[l], 'w_mq': w_mq[l], 'w_mo': w_mo[l], 'w_gate': w_gate[l], 'w_up': w_up[l],
                    'w_down': w_down[l]}
    tm = 512

    w_mkv = jnp.concatenate([w_mk[l], w_mv[l]], axis=1).astype(BF)
    mem_b = mem_prompt.reshape(n_p * n_mem, D_MODEL).astype(BF)
    (mkv_f,) = _matmul(mem_b, w_mkv, [F32], n_p * n_mem, 1024, "memory_kv")
    xp = x_prompt.reshape(n_p * t_p, D_MODEL)
    pb, pf = _inproj(xp, w['w_in'], np.arange(t_p), n_p, INPROJ_ROWS_PROMPT)
    odp, late_b = _diff_attn_prompt(lam, pb['dq'], pb['dk'], pb['dv'], w['subln_g'], n_p, t_p, lam_init,
                                    cast_weights=tuple(late_weights.values()))
    w.update(zip(late_weights, late_b))
    osp = _sb_attn_prompt(pb['sq'], pb['sk'], pb['sv'], n_p, t_p)
    zero_state = jnp.zeros((n_p, CONV_W - 1, D_FF), F32)
    mem_heads = (n_p, n_mem, N_MEM_HEADS, MEM_HEAD_DIM)
    yp, lastp = _after_mixers(xp, odp, osp, _tiled_lane_rows(mkv_f[:, :D_MODEL].reshape(mem_heads), n_p, N_MEM_HEADS),
                              _tiled_lane_rows(mkv_f[:, D_MODEL:].reshape(mem_heads), n_p, N_MEM_HEADS),
                              zero_state, n_p, t_p, w)

    xs = x_sample.reshape(n_s * t_s, D_MODEL)
    sb, sf = _inproj(xs, w['w_in'], past + np.arange(t_s), n_s, tm)
    rows = lambda c: _as_lane_rows(c[l], n_s)
    ods = _diff_attn_sample(lam, sb['dq'], sb['dk'], sb['dv'], rows(cache_diff_k),
                            _tiled_lane_rows(cache_diff_v[l], n_s, N_DIFF_HEADS), w['subln_g'],
                            n_s, t_s, past, lam_init)
    oss = _sb_attn_sample(sb['sq'], sb['sk'], sb['sv'], rows(cache_sb_k), rows(cache_sb_v), n_s, t_s, past)
    state_s = jnp.transpose(state_ffn_conv[l], (1, 0, 2))
    ys, lasts = _after_mixers(xs, ods, oss, _tiled_lane_rows(cache_mem_k[l], n_s, N_MEM_HEADS),
                              _tiled_lane_rows(cache_mem_v[l], n_s, N_MEM_HEADS), state_s, n_s, t_s, w)

    tiles_per_seq = t_p // FFN_ROWS
    new_conv_p = lastp[tiles_per_seq - 1::tiles_per_seq]
    new_conv_s = jnp.transpose(lasts, (1, 0, 2))

    def new_rows(f, n, t):
        dv = jnp.transpose(f['dv'].reshape(n, t, DIFF_V_DIM // HEAD_DIM, N_DIFF_HEADS, HEAD_DIM), (0, 1, 3, 2, 4))
        return (f['dk'].reshape(1, n, t, N_DIFF_HEADS, 2, HEAD_DIM), dv.reshape(1, n, t, N_DIFF_HEADS, DIFF_V_DIM),
                f['sk'].reshape(1, n, t, N_SB_HEADS, HEAD_DIM), f['sv'].reshape(1, n, t, N_SB_HEADS, HEAD_DIM))

    return ((yp.reshape(n_p, t_p, D_MODEL), ys.reshape(n_s, t_s, D_MODEL))
            + new_rows(pf, n_p, t_p)
            + (mkv_f[:, :D_MODEL].reshape((1,) + mem_heads), mkv_f[:, D_MODEL:].reshape((1,) + mem_heads),
               new_conv_p[None])
            + new_rows(sf, n_s, t_s)
            + (new_conv_s[None],))
```

```python
import functools
import math

import jax
import jax.numpy as jnp
import numpy as np
from jax import lax
from jax.experimental import pallas as pl
from jax.experimental.pallas import tpu as pltpu

BF = jnp.bfloat16
F32 = jnp.float32

D_MODEL = 2048
CHUNK = 64
CHUNK_SHIFT = 6
HEAD_DIM = 128
N_DIFF_HEADS = 4
N_SB_HEADS = 8
DIFF_V_DIM = 2 * HEAD_DIM
ROPE_DIM = HEAD_DIM // 4
ROPE_THETA = 500000.0
N_MEM_HEADS = 4
MEM_HEAD_DIM = D_MODEL // N_MEM_HEADS
D_FF = 5632
CONV_W = 3
LN_EPS = 1e-5
DEPTH = 1
DN_ALPHA = (2 * DEPTH) ** 0.25
SEG = 1024
N_SEG = 6
VMEM_LIMIT = 56 * 1024 * 1024
LOG2E = math.log2(math.e)
QK_SCALE2 = HEAD_DIM ** -0.5 * LOG2E
NEG_BIG = -1e30
LN_ROWS = 128
FFN_COLS = 256
FFN_ROWS = 1024
FFN_CHUNK = 512
INPROJ_ROWS_PROMPT = 1024
MID_ROWS = 512
SB_BLOCK = 256

_NT = (((1,), (1,)), ((), ()))


def _cparams(n_axes):
    return pltpu.CompilerParams(dimension_semantics=("arbitrary",) * n_axes,
                                vmem_limit_bytes=VMEM_LIMIT)


def _layer_norm(y, g, b):
    mu = jnp.mean(y, axis=-1, keepdims=True)
    yc = y - mu
    var = jnp.mean(yc * yc, axis=-1, keepdims=True)
    return yc * lax.rsqrt(var + LN_EPS) * g + b


SLOTS = SEG // HEAD_DIM


def _inproj_kernel(*refs, rope, want_f32, tile_rows, scale, emit_xb, interleave):
    x_ref, w_ref = refs[:2]
    tabs = refs[2:5] if rope else None
    outs = refs[5:] if rope else refs[2:]
    if interleave:
        outs, perm_ref = outs[:-1], outs[-1]
    pb_ref = outs[0]
    xb = x_ref[...].astype(BF)
    if emit_xb:
        outs[-1][...] = xb
    acc = jnp.dot(xb, w_ref[...], preferred_element_type=F32)
    tm = acc.shape[0]
    for ch in range(SLOTS):
        lanes = slice(ch * HEAD_DIM, (ch + 1) * HEAD_DIM)
        r = acc[:, lanes]
        if rope:
            c_ref, s1_ref, s2_ref = tabs
            r = (r * c_ref[...] + pltpu.roll(r, HEAD_DIM - ROPE_DIM // 2, 1) * s1_ref[...]
                 + pltpu.roll(r, ROPE_DIM // 2, 1) * s2_ref[...])
        elif scale != 1.0:
            r = r * scale
        if interleave:
            run = interleave // SUBLANES
            for b0 in range(0, tm, interleave):
                for s in range(SUBLANES):
                    perm_ref[ch, pl.ds(b0 + s, run, stride=SUBLANES), :] = r[b0 + s * run:b0 + (s + 1) * run, :]
            pb_ref[0, :, lanes] = perm_ref[ch].astype(BF)
        else:
            pb_ref[0, :, lanes] = r.astype(BF)
        if want_f32:
            head, col_tile = divmod(ch, SLOTS // tile_rows) if tile_rows else (0, 0)
            off = col_tile * tile_rows + head if tile_rows else ch
            outs[1][pl.ds(off, tm, stride=SLOTS), :] = r


def _rope_tables(pos, n_rep):
    half = ROPE_DIM // 2
    inv_freq = np.power(ROPE_THETA, -np.arange(half, dtype=np.float64) * (2.0 / ROPE_DIM))
    ang = pos.astype(np.float64)[:, None] * inv_freq[None, :]
    cos, sin = jnp.asarray(np.cos(ang), F32), jnp.asarray(np.sin(ang), F32)
    n = pos.shape[0]
    c = jnp.concatenate([cos, cos, jnp.ones((n, HEAD_DIM - ROPE_DIM), F32)], axis=1)
    s1 = jnp.concatenate([-sin, jnp.zeros((n, HEAD_DIM - half), F32)], axis=1)
    s2 = jnp.concatenate([jnp.zeros((n, half), F32), sin, jnp.zeros((n, HEAD_DIM - ROPE_DIM), F32)], axis=1)
    return tuple(jnp.tile(tb, (n_rep, 1)) for tb in (c, s1, s2))


IN_GROUPS = (("dq", True, False, 0, QK_SCALE2), ("dk", True, True, 0, 1.0), ("dv", False, True, N_DIFF_HEADS, 1.0),
             ("sq", False, False, 0, QK_SCALE2), ("sk", False, True, 0, 1.0), ("sv", False, True, 0, 1.0))


def _inproj(x, w_in_b, pos, n_seq, tm, sb_interleave=0):
    t = x.shape[0]
    tabs = _rope_tables(pos, n_seq)
    row = lambda i: (i, 0)
    tab = pl.BlockSpec((tm, HEAD_DIM), row)
    bf, f32 = {}, {}
    xb = None
    for j, (name, rope, want_f32, tile_rows, scale) in enumerate(IN_GROUPS):
        assert not (want_f32 and scale != 1.0)
        emit_xb = xb is None
        interleave = sb_interleave if name in ("sk", "sv") else 0
        outs = pl.pallas_call(
            functools.partial(_inproj_kernel, rope=rope, want_f32=want_f32, tile_rows=tile_rows, scale=scale,
                              emit_xb=emit_xb, interleave=interleave),
            scratch_shapes=[pltpu.VMEM((SLOTS, tm, HEAD_DIM), F32)] if interleave else [],
            grid=(t // tm,),
            in_specs=[pl.BlockSpec((tm, D_MODEL), row),
                      pl.BlockSpec((D_MODEL, SEG), functools.partial(lambda i, j: (0, j), j=j))]
                     + ([tab, tab, tab] if rope else []),
            out_specs=[pl.BlockSpec((1, tm, SEG), lambda i: (0, i, 0))]
                      + ([pl.BlockSpec((tm * SLOTS, HEAD_DIM), row)] if want_f32 else [])
                      + ([pl.BlockSpec((tm, D_MODEL), row)] if emit_xb else []),
            out_shape=[jax.ShapeDtypeStruct((1, t, SEG), BF)]
                      + ([jax.ShapeDtypeStruct((t * SLOTS, HEAD_DIM), F32)] if want_f32 else [])
                      + ([jax.ShapeDtypeStruct((t, D_MODEL), BF)] if emit_xb else []),
            compiler_params=_cparams(1),
            name="inproj_" + name,
        )(x if emit_xb else xb, w_in_b, *([tb * scale for tb in tabs] if rope else ()))
        bf[name] = outs[0]
        if want_f32:
            f32[name] = outs[1]
        if emit_xb:
            xb = outs[-1]
    return bf, f32


def _matmul_kernel(x_ref, w_ref, *o_refs):
    acc = jnp.dot(x_ref[...], w_ref[...], preferred_element_type=F32)
    for o in o_refs:
        o[...] = acc.astype(o.dtype)


def _matmul(x_b, w_b, out_dtypes, tm, tn, name):
    t, k = x_b.shape
    n = w_b.shape[1]
    return pl.pallas_call(
        _matmul_kernel,
        grid=(t // tm, n // tn),
        in_specs=[pl.BlockSpec((tm, k), lambda i, j: (i, 0)),
                  pl.BlockSpec((k, tn), lambda i, j: (0, j))],
        out_specs=[pl.BlockSpec((tm, tn), lambda i, j: (i, j)) for _ in out_dtypes],
        out_shape=[jax.ShapeDtypeStruct((t, n), d) for d in out_dtypes],
        compiler_params=_cparams(2),
        name=name,
    )(x_b, w_b)


def _chunk_visible(q0, k0, nq, nk):
    qpos = q0 + lax.broadcasted_iota(jnp.int32, (nq, nk), 0)
    kpos = k0 + lax.broadcasted_iota(jnp.int32, (nq, nk), 1)
    return lax.shift_right_logical(kpos, CHUNK_SHIFT) <= lax.shift_right_logical(qpos, CHUNK_SHIFT)


def _subln(o0, o1, lam, g, lam_init):
    of = o0 - lam * o1
    return of * lax.rsqrt(jnp.mean(of * of, axis=-1, keepdims=True) + LN_EPS) * g * (1.0 - lam_init)


def _diff_prompt_kernel(*refs, tq, sub, lam_init, n_cast):
    lam_ref, q_ref, k_ref, v_ref, g_ref = refs[:5]
    o_ref = refs[5 + n_cast]
    vt_ref, m_ref, l_ref, acc_ref = refs[6 + 2 * n_cast:]
    for w_ref, wb_ref in zip(refs[5:5 + n_cast], refs[6 + n_cast:6 + 2 * n_cast]):
        wb_ref[...] = w_ref[...].astype(BF)
    qi = pl.program_id(2)
    n_sub = tq // sub

    @pl.when(qi == 0)
    def _():
        for g in range(vt_ref.shape[0]):
            for j in range(n_sub):
                blk = v_ref[0, g * tq + j * sub:g * tq + (j + 1) * sub, :]
                vt_ref[g, :, j * sub:(j + 1) * sub] = blk.astype(F32).T.astype(BF)

    m_ref[...] = jnp.full(m_ref.shape, NEG_BIG, F32)
    l_ref[...] = jnp.zeros(l_ref.shape, F32)
    acc_ref[...] = jnp.zeros(acc_ref.shape, F32)

    def run(g, items):
        k0 = pl.multiple_of(g * tq, tq)
        scores = []
        for c, r, nk, vis in items:
            cs = slice(c * HEAD_DIM, (c + 1) * HEAD_DIM)
            s = lax.dot_general(k_ref[0, pl.ds(k0, nk), cs], q_ref[0, r * sub:(r + 1) * sub, cs], _NT,
                                preferred_element_type=F32)
            if vis is not None:
                tail = jnp.where(vis, s[nk - sub:, :], -jnp.inf)
                s = tail if nk == sub else jnp.concatenate([s[:nk - sub, :], tail], axis=0)
            scores.append(s)
        probs = []
        for s, (c, r, nk, _) in zip(scores, items):
            i = c * n_sub + r
            m_old = m_ref[i]
            m_new = jnp.maximum(m_old, jnp.max(s, axis=0, keepdims=True))
            alpha = jnp.exp2(m_old - m_new)
            p = jnp.exp2(s - m_new)
            l_ref[i] = alpha * l_ref[i] + jnp.sum(p, axis=0, keepdims=True)
            m_ref[i] = m_new
            probs.append((alpha, p.astype(BF)))
        for (alpha, p), (c, r, nk, _) in zip(probs, items):
            i = c * n_sub + r
            acc_ref[i] = alpha * acc_ref[i] + jnp.dot(vt_ref[g, :, :nk], p, preferred_element_type=F32)

    full_items = [(c, r, tq, None) for r in range(n_sub) for c in range(2)]

    def body(g, carry):
        run(g, full_items)
        return carry

    lax.fori_loop(0, qi, body, 0)

    k_i = lax.broadcasted_iota(jnp.int32, (sub, sub), 0)
    q_i = lax.broadcasted_iota(jnp.int32, (sub, sub), 1)
    vis = lax.shift_right_logical(k_i, CHUNK_SHIFT) <= lax.shift_right_logical(q_i, CHUNK_SHIFT)
    run(qi, [(c, r, (r + 1) * sub, vis) for r in range(n_sub) for c in range(2)])

    for r in range(n_sub):
        o0 = (acc_ref[r] / l_ref[r]).T
        o1 = (acc_ref[n_sub + r] / l_ref[n_sub + r]).T
        o_ref[r * sub:(r + 1) * sub, :] = _subln(o0, o1, lam_ref[0], g_ref[...], lam_init).astype(BF)


def _diff_attn_prompt(lam, q, k, v, subln_g, n_batch, t, lam_init, cast_weights=(), tq=1024, sub=256):
    assert t % tq == 0 and tq % sub == 0 and sub % CHUNK == 0
    nq = t // tq
    n_steps = n_batch * N_DIFF_HEADS * nq
    step = lambda b, h, i: ((b * N_DIFF_HEADS + h) * nq + i, 0)
    cast_specs = []
    for wt in cast_weights:
        assert wt.shape[0] % (16 * n_steps) == 0
        cast_specs.append(pl.BlockSpec((wt.shape[0] // n_steps, wt.shape[1]), step))
    kern = functools.partial(_diff_prompt_kernel, tq=tq, sub=sub, lam_init=lam_init, n_cast=len(cast_weights))
    outs = pl.pallas_call(
        kern,
        grid=(n_batch, N_DIFF_HEADS, nq),
        in_specs=[pl.BlockSpec(memory_space=pltpu.SMEM),
                  pl.BlockSpec((1, tq, DIFF_V_DIM), lambda b, h, i: (0, b * nq + i, h)),
                  pl.BlockSpec((1, t, DIFF_V_DIM), lambda b, h, i: (0, b, h)),
                  pl.BlockSpec((1, t, DIFF_V_DIM), lambda b, h, i: (0, b, h)),
                  pl.BlockSpec((1, DIFF_V_DIM), lambda b, h, i: (0, 0))] + cast_specs,
        out_specs=[pl.BlockSpec((tq, DIFF_V_DIM), lambda b, h, i: (b * nq + i, h))] + cast_specs,
        out_shape=[jax.ShapeDtypeStruct((n_batch * t, SEG), BF)]
                  + [jax.ShapeDtypeStruct(wt.shape, BF) for wt in cast_weights],
        scratch_shapes=[pltpu.VMEM((nq, DIFF_V_DIM, tq), BF),
                        pltpu.VMEM((2 * tq // sub, 1, sub), F32), pltpu.VMEM((2 * tq // sub, 1, sub), F32),
                        pltpu.VMEM((2 * tq // sub, DIFF_V_DIM, sub), F32)],
        compiler_params=_cparams(3),
        name="diff_attn_prompt",
    )(lam, q, k, v, subln_g.reshape(1, DIFF_V_DIM), *cast_weights)
    return outs[0], outs[1:]


def _diff_sample_kernel(lam_ref, q_ref, kn_ref, vn_ref, kc_ref, vc_ref, g_ref, o_ref, *, t_s, past, lam_init):
    vis = _chunk_visible(past, past, t_s, t_s)
    for h in range(N_DIFF_HEADS):
        vs = slice(h * DIFF_V_DIM, (h + 1) * DIFF_V_DIM)
        v_old = _lane_rows(vc_ref, 0, past, h, DIFF_V_DIM, SEG, tiled=True).astype(BF)
        v_new = vn_ref[0, :, vs]
        o = []
        for c in range(2):
            cs = slice((2 * h + c) * HEAD_DIM, (2 * h + c + 1) * HEAD_DIM)
            q = q_ref[0, :, cs]
            k_old = _lane_rows(kc_ref, 0, past, 2 * h + c, HEAD_DIM, SEG).astype(BF)
            s_old = lax.dot_general(q, k_old, _NT, preferred_element_type=F32)
            s_new = lax.dot_general(q, kn_ref[0, :, cs], _NT, preferred_element_type=F32)
            s_new = jnp.where(vis, s_new, -jnp.inf)
            m = jnp.maximum(jnp.max(s_old, axis=-1, keepdims=True), jnp.max(s_new, axis=-1, keepdims=True))
            p_old = jnp.exp2(s_old - m)
            p_new = jnp.exp2(s_new - m)
            l = jnp.sum(p_old, axis=-1, keepdims=True) + jnp.sum(p_new, axis=-1, keepdims=True)
            acc = (jnp.dot(p_old.astype(BF), v_old, preferred_element_type=F32)
                   + jnp.dot(p_new.astype(BF), v_new, preferred_element_type=F32))
            o.append(acc / l)
        o_ref[:, vs] = _subln(o[0], o[1], lam_ref[0], g_ref[...], lam_init).astype(BF)


def _new_rows_spec(t_s):
    return pl.BlockSpec((1, t_s, SEG), lambda b: (0, b, 0))


def _as_lane_rows(x, n_batch):
    return x.reshape(n_batch, -1, HEAD_DIM)


def _lane_rows(ref, t0, nt, slot, slot_width, width, tiled=False):
    per_row = width // HEAD_DIM
    per_slot = slot_width // HEAD_DIM
    n_slots = width // slot_width
    offs = [(k * n_slots + slot) if tiled else (slot * per_slot + k) for k in range(per_slot)]
    parts = [ref[0, pl.ds(t0 * per_row + o, nt, stride=per_row), :] for o in offs]
    return parts[0] if per_slot == 1 else jnp.concatenate(parts, axis=1)


def _tiled_lane_rows(x, n_batch, n_slots):
    rows, w = x.shape[1], x.shape[3]
    x = x.reshape(n_batch, rows, n_slots, w // HEAD_DIM, HEAD_DIM)
    return jnp.transpose(x, (0, 1, 3, 2, 4)).reshape(n_batch, -1, HEAD_DIM)


def _cache_spec(cache):
    return pl.BlockSpec((1,) + cache.shape[1:], lambda b: (b, 0, 0))


def _diff_attn_sample(lam, q, k_new, v_new, cache_k, cache_v, subln_g, n_batch, t_s, past, lam_init):
    kern = functools.partial(_diff_sample_kernel, t_s=t_s, past=past, lam_init=lam_init)
    return pl.pallas_call(
        kern,
        grid=(n_batch,),
        in_specs=[pl.BlockSpec(memory_space=pltpu.SMEM), _new_rows_spec(t_s), _new_rows_spec(t_s),
                  _new_rows_spec(t_s), _cache_spec(cache_k), _cache_spec(cache_v),
                  pl.BlockSpec((1, DIFF_V_DIM), lambda b: (0, 0))],
        out_specs=pl.BlockSpec((t_s, SEG), lambda b: (b, 0)),
        out_shape=jax.ShapeDtypeStruct((n_batch * t_s, SEG), BF),
        compiler_params=_cparams(1),
        name="diff_attn_sample",
    )(lam, q, k_new, v_new, cache_k, cache_v, subln_g.reshape(1, DIFF_V_DIM))


def _sb_neg_log_keep(z2):
    mx = jnp.maximum(z2, 0.0)
    mn = jnp.minimum(z2, 0.0)
    return mx + jnp.log(1.0 + jnp.exp2(mn - mx)) * LOG2E


def _split_bf16(x):
    hi = x.astype(BF)
    return hi, (x - hi.astype(F32)).astype(BF)


SUBLANES = 8


def _interleaved_key(n):
    r = lax.broadcasted_iota(jnp.int32, (n, 1), 0)
    return jnp.bitwise_and(r, SUBLANES - 1) * (n // SUBLANES) + lax.shift_right_logical(r, 3)


def _suffix_interleaved(x, carry):
    nj = x.shape[0] // SUBLANES
    run = [None] * nj
    acc = None
    for j in reversed(range(nj)):
        part = x[SUBLANES * j:SUBLANES * (j + 1), :]
        acc = part if acc is None else acc + part
        run[j] = acc
    g = run[0]
    row = lax.broadcasted_iota(jnp.int32, g.shape, 0)
    incl = g
    for d in (1, 2, 4):
        incl = incl + jnp.where(row < SUBLANES - d, pltpu.roll(incl, SUBLANES - d, 0), 0.0)
    later = incl - g + carry
    return jnp.concatenate([r + later for r in run], axis=0), incl[0:1, :]


def _sb_group(items, carry, acc, key_axis):
    if key_axis == 0:
        zs = [lax.dot_general(k, q, _NT, preferred_element_type=F32) for q, k, _, _, _, _ in items]
    else:
        zs = [lax.dot_general(q, k, _NT, preferred_element_type=F32) for q, k, _, _, _, _ in items]
    splits = []
    for z, (_, _, _, tri, before, _) in zip(zs, items):
        nlk = _sb_neg_log_keep(z)
        if before is not None:
            nlk = jnp.where(before, nlk, 0.0)
        if tri is None:
            splits.append(nlk)
            continue
        hi, lo = _split_bf16(nlk)
        fused = tri.shape[1 - key_axis] == 2 * tri.shape[key_axis]
        splits.append(jnp.concatenate([hi, lo], axis=key_axis) if fused else (hi, lo))
    sufs = []
    for hl, (_, _, _, tri, _, ch) in zip(splits, items):
        if tri is None:
            suf, total = _suffix_interleaved(hl, carry[ch])
        else:
            if key_axis == 0:
                mm = lambda x: jnp.dot(tri, x, preferred_element_type=F32)
            else:
                mm = lambda x: jnp.dot(x, tri, preferred_element_type=F32)
            suf = mm(hl[0]) + mm(hl[1]) if isinstance(hl, tuple) else mm(hl)
            total = suf[0:1, :] if key_axis == 0 else suf[:, 0:1]
            suf = suf + carry[ch]
        sufs.append(suf)
        carry[ch] = carry[ch] + total
    probs = []
    for z, suf, (_, _, _, _, before, ch) in zip(zs, sufs, items):
        a = jnp.exp2(z - suf)
        if before is not None:
            a = jnp.where(before, a, 0.0)
        probs.append(a.astype(BF))
    for a, (_, _, v_op, _, _, ch) in zip(probs, items):
        if key_axis == 0:
            acc[ch] = acc[ch] + jnp.dot(v_op, a, preferred_element_type=F32)
        else:
            acc[ch] = acc[ch] + jnp.dot(a, v_op, preferred_element_type=F32)


def _tri(n, upper):
    r = lax.broadcasted_iota(jnp.int32, (n, n), 0)
    c = lax.broadcasted_iota(jnp.int32, (n, n), 1)
    return jnp.where((c >= r) if upper else (r >= c), 1.0, 0.0).astype(BF)


def _sb_prompt_kernel(q_ref, k_ref, v_ref, o_ref, vt_ref, carry_ref, acc_ref, *, tq, sub, t, unroll):
    qi = pl.program_id(2)
    n_sub = tq // sub

    @pl.when(qi == 0)
    def _():
        for kb in range(t // sub):
            vt_ref[kb] = v_ref[0, kb * sub:(kb + 1) * sub, :].astype(F32).T.astype(BF)

    tri2 = None
    before = _interleaved_key(sub) < lax.broadcasted_iota(jnp.int32, (sub, sub), 1)
    qs = [q_ref[0, c * sub:(c + 1) * sub, :] for c in range(n_sub)]
    carry = [jnp.zeros((1, sub), F32) for _ in range(n_sub)]
    acc = [jnp.zeros((HEAD_DIM, sub), F32) for _ in range(n_sub)]

    base = qi * n_sub
    items = []
    for j in reversed(range(n_sub)):
        kblk = k_ref[0, pl.ds(pl.multiple_of((base + j) * sub, sub), sub), :]
        vt_blk = vt_ref[base + j]
        items += [(qs[c], kblk, vt_blk, tri2, before if c == j else None, c) for c in range(j, n_sub)]
    _sb_group(items, carry, acc, 0)
    for c in range(n_sub):
        carry_ref[c] = carry[c]
        acc_ref[c] = acc[c]

    def body(i, x):
        cr = [carry_ref[c] for c in range(n_sub)]
        ac = [acc_ref[c] for c in range(n_sub)]
        items = []
        for uu in range(unroll):
            kb = base - 1 - (i * unroll + uu)
            kblk = k_ref[0, pl.ds(pl.multiple_of(kb * sub, sub), sub), :]
            vt_blk = vt_ref[kb]
            items += [(qs[c], kblk, vt_blk, tri2, None, c) for c in range(n_sub)]
        _sb_group(items, cr, ac, 0)
        for c in range(n_sub):
            carry_ref[c] = cr[c]
            acc_ref[c] = ac[c]
        return x

    lax.fori_loop(0, (qi * n_sub) // unroll, body, 0)
    for c in range(n_sub):
        o_ref[c * sub:(c + 1) * sub, :] = acc_ref[c].T.astype(BF)


def _sb_attn_prompt(q, k, v, n_batch, t, tq=1024, sub=SB_BLOCK, unroll=2):
    assert t % tq == 0 and tq % sub == 0 and (tq // sub) % unroll == 0
    nq = t // tq
    n_sub = tq // sub
    kern = functools.partial(_sb_prompt_kernel, tq=tq, sub=sub, t=t, unroll=unroll)
    return pl.pallas_call(
        kern,
        grid=(n_batch, N_SB_HEADS, nq),
        in_specs=[pl.BlockSpec((1, tq, HEAD_DIM), lambda b, h, i: (0, b * nq + i, h)),
                  pl.BlockSpec((1, t, HEAD_DIM), lambda b, h, i: (0, b, h)),
                  pl.BlockSpec((1, t, HEAD_DIM), lambda b, h, i: (0, b, h))],
        out_specs=pl.BlockSpec((tq, HEAD_DIM), lambda b, h, i: (b * nq + i, h)),
        out_shape=jax.ShapeDtypeStruct((n_batch * t, SEG), BF),
        scratch_shapes=[pltpu.VMEM((t // sub, HEAD_DIM, sub), BF), pltpu.VMEM((n_sub, 1, sub), F32),
                        pltpu.VMEM((n_sub, HEAD_DIM, sub), F32)],
        compiler_params=_cparams(3),
        name="sb_attn_prompt",
    )(q, k, v)


def _sb_sample_kernel(q_ref, kn_ref, vn_ref, kc_ref, vc_ref, o_ref, *, t_s, past, tk):
    tri_new = _tri(t_s, False)
    l_old = _tri(tk, False)
    tri_old = jnp.concatenate([l_old, l_old], axis=0)
    before = (lax.broadcasted_iota(jnp.int32, (t_s, t_s), 1) < lax.broadcasted_iota(jnp.int32, (t_s, t_s), 0))
    heads = range(N_SB_HEADS)
    cols = [slice(h * HEAD_DIM, (h + 1) * HEAD_DIM) for h in heads]
    qs = [q_ref[0, :, cs] for cs in cols]
    carry = [jnp.zeros((t_s, 1), F32) for _ in heads]
    acc = [jnp.zeros((t_s, HEAD_DIM), F32) for _ in heads]
    _sb_group([(qs[h], kn_ref[0, :, cols[h]], vn_ref[0, :, cols[h]], tri_new, before, h) for h in heads],
              carry, acc, 1)
    for kb in reversed(range(past // tk)):
        rows = slice(kb * tk, (kb + 1) * tk)
        _sb_group([(qs[h], _lane_rows(kc_ref, kb * tk, tk, h, HEAD_DIM, SEG).astype(BF),
                    _lane_rows(vc_ref, kb * tk, tk, h, HEAD_DIM, SEG).astype(BF), tri_old, None, h)
                   for h in heads], carry, acc, 1)
    for h in heads:
        o_ref[:, cols[h]] = acc[h].astype(BF)


def _sb_attn_sample(q, k_new, v_new, cache_k, cache_v, n_batch, t_s, past, tk=256):
    assert past % tk == 0
    kern = functools.partial(_sb_sample_kernel, t_s=t_s, past=past, tk=tk)
    return pl.pallas_call(
        kern,
        grid=(n_batch,),
        in_specs=[_new_rows_spec(t_s), _new_rows_spec(t_s), _new_rows_spec(t_s),
                  _cache_spec(cache_k), _cache_spec(cache_v)],
        out_specs=pl.BlockSpec((t_s, SEG), lambda b: (b, 0)),
        out_shape=jax.ShapeDtypeStruct((n_batch * t_s, SEG), BF),
        compiler_params=_cparams(1),
        name="sb_attn_sample",
    )(q, k_new, v_new, cache_k, cache_v)


MEM_SCALE2 = MEM_HEAD_DIM ** -0.5 * LOG2E


def _proj_ln_kernel(*refs, n_in, with_q):
    a_refs = refs[:n_in]
    w_refs = refs[n_in:2 * n_in]
    res_ref, g_ref, b_ref = refs[2 * n_in:2 * n_in + 3]
    rest = refs[2 * n_in + 3:]
    tm = res_ref.shape[0]
    slices = [slice(r, r + LN_ROWS) for r in range(0, tm, LN_ROWS)]
    accs = []
    for rows in slices:
        acc = None
        for a, w in zip(a_refs, w_refs):
            d = jnp.dot(a[rows, :], w[...], preferred_element_type=F32)
            acc = d if acc is None else acc + d
        accs.append(acc)
    outs = [_layer_norm(DN_ALPHA * res_ref[rows, :] + acc, g_ref[...], b_ref[...]) for rows, acc in zip(slices, accs)]
    if with_q:
        wq_ref, of_ref, oq_ref = rest
        for rows, out in zip(slices, outs):
            of_ref[rows, :] = out
        for rows, out in zip(slices, outs):
            q = jnp.dot(out.astype(BF), wq_ref[...], preferred_element_type=F32)
            oq_ref[rows, :] = (q * MEM_SCALE2).astype(BF)
    else:
        (of_ref,) = rest
        for rows, out in zip(slices, outs):
            of_ref[rows, :] = out


def _proj_ln(a_list, w_b, res, g, b, tm, name, wq_b=None):
    t = res.shape[0]
    n_in = len(a_list)
    ka = a_list[0].shape[1]
    row = lambda i: (i, 0)
    once = dict(pipeline_mode=pl.Buffered(1))
    full = pl.BlockSpec((tm, D_MODEL), row)
    vec = pl.BlockSpec((1, D_MODEL), lambda i: (0, 0))
    in_specs = ([pl.BlockSpec((tm, ka), row) for _ in a_list]
                + [pl.BlockSpec((ka, D_MODEL), functools.partial(lambda i, r: (r, 0), r=r), **once) for r in range(n_in)]
                + [full, vec, vec])
    args = [*a_list, *([w_b] * n_in), res, g.reshape(1, D_MODEL), b.reshape(1, D_MODEL)]
    out_specs = [full]
    out_shape = [jax.ShapeDtypeStruct((t, D_MODEL), F32)]
    if wq_b is not None:
        in_specs.append(pl.BlockSpec((D_MODEL, D_MODEL), lambda i: (0, 0), **once))
        args.append(wq_b)
        out_specs.append(full)
        out_shape.append(jax.ShapeDtypeStruct((t, D_MODEL), BF))
    return pl.pallas_call(
        functools.partial(_proj_ln_kernel, n_in=n_in, with_q=wq_b is not None),
        grid=(t // tm,),
        in_specs=in_specs,
        out_specs=out_specs,
        out_shape=out_shape,
        compiler_params=_cparams(1),
        name=name,
    )(*args)


def _cross_kernel(q_ref, k_ref, v_ref, o_ref):
    n_mem = k_ref.shape[1] * HEAD_DIM // D_MODEL
    for h in range(N_MEM_HEADS):
        sl = slice(h * MEM_HEAD_DIM, (h + 1) * MEM_HEAD_DIM)
        k = _lane_rows(k_ref, 0, n_mem, h, MEM_HEAD_DIM, D_MODEL, tiled=True).astype(BF)
        v = _lane_rows(v_ref, 0, n_mem, h, MEM_HEAD_DIM, D_MODEL, tiled=True).astype(BF)
        s = lax.dot_general(q_ref[:, sl], k, _NT, preferred_element_type=F32)
        p = jnp.exp2(s - jnp.max(s, axis=-1, keepdims=True))
        l = jnp.sum(p, axis=-1, keepdims=True)
        o = jnp.dot(p.astype(BF), v, preferred_element_type=F32) / l
        o_ref[:, sl] = o.astype(BF)


def _cross_attn(q_b, memk, memv, n_batch, t_b, tm):
    nt = t_b // tm
    mem_spec = pl.BlockSpec((1,) + memk.shape[1:], lambda b, i: (b, 0, 0))
    return pl.pallas_call(
        _cross_kernel,
        grid=(n_batch, nt),
        in_specs=[pl.BlockSpec((tm, D_MODEL), lambda b, i: (b * nt + i, 0)), mem_spec, mem_spec],
        out_specs=pl.BlockSpec((tm, D_MODEL), lambda b, i: (b * nt + i, 0)),
        out_shape=jax.ShapeDtypeStruct((n_batch * t_b, D_MODEL), BF),
        compiler_params=_cparams(2),
        name="cross_attn",
    )(q_b, memk, memv)


def _gelu_tanh(x):
    return 0.5 * x * (1.0 + jnp.tanh(math.sqrt(2.0 / math.pi) * (x + 0.044715 * (x * x * x))))


def _ffn_kernel(h_ref, wg_ref, wu_ref, wd_ref, wc_ref, bc_ref, st_ref, g_ref, b_ref,
                o_ref, last_ref, hb_ref, carry_ref, gs_ref, *, tm, fc, seq, n_ff):
    i = pl.program_id(0)
    c = pl.program_id(1)

    @pl.when(c == 0)
    def _():
        hb_ref[...] = h_ref[...].astype(BF)
        o_ref[...] = jnp.zeros(o_ref.shape, F32)

    @pl.when((i == 0) & (c == 0))
    def _():
        carry_ref[...] = jnp.zeros(carry_ref.shape, F32)

    hb = hb_ref[...]
    halves = [slice(k, k + FFN_COLS) for k in range(0, fc, FFN_COLS)]
    gs = [jnp.dot(hb, wg_ref[:, cols], preferred_element_type=F32) for cols in halves]
    us = [jnp.dot(hb, wu_ref[:, cols], preferred_element_type=F32) for cols in halves]
    row = lax.broadcasted_iota(jnp.int32, (tm, FFN_COLS), 0)
    acts = []
    for g, u, cols in zip(gs, us, halves):
        if seq >= tm:
            tiles_per_seq = seq // tm
            at_start = (i % tiles_per_seq) == 0
            prev = jnp.where(at_start, st_ref[0, :, cols], carry_ref[c, :, cols])
            pos = row
            ex0 = prev[0:1]
            ex1 = prev[1:2]
            carry_ref[c, :, cols] = g[tm - 2:tm, :]
            last_ref[0, :, cols] = g[tm - 2:tm, :]
        else:
            ns = tm // seq
            pos = jnp.bitwise_and(row, seq - 1)
            ex0 = jnp.broadcast_to(st_ref[0, :, cols][:, None, :], (ns, seq, FFN_COLS)).reshape(tm, FFN_COLS)
            ex1 = jnp.broadcast_to(st_ref[1, :, cols][:, None, :], (ns, seq, FFN_COLS)).reshape(tm, FFN_COLS)
            for k in range(FFN_COLS // 128):
                lanes = slice(cols.start + k * 128, cols.start + (k + 1) * 128)
                gs_ref[k] = g[:, k * 128:(k + 1) * 128]
                last_ref[0, :, lanes] = gs_ref[k, pl.ds(seq - 2, ns, stride=seq), :]
                last_ref[1, :, lanes] = gs_ref[k, pl.ds(seq - 1, ns, stride=seq), :]
        g_m1 = jnp.where(pos == 0, ex1, pltpu.roll(g, 1, 0))
        g_m2 = jnp.where(pos == 0, ex0, jnp.where(pos == 1, ex1, pltpu.roll(g, 2, 0)))
        conv = (bc_ref[:, cols] + g_m2 * wc_ref[0:1, cols] + g_m1 * wc_ref[1:2, cols] + g * wc_ref[2:3, cols])
        acts.append((_gelu_tanh(conv) * u).astype(BF))
    d = None
    for a, cols in zip(acts, halves):
        dd = jnp.dot(a, wd_ref[cols, :], preferred_element_type=F32)
        d = dd if d is None else d + dd
    o_ref[...] += d

    @pl.when(c == n_ff - 1)
    def _():
        o_ref[...] = _layer_norm(DN_ALPHA * h_ref[...] + o_ref[...], g_ref[...], b_ref[...])


def _ffn(h, w_gate_b, w_up_b, w_down_b, w_conv, b_conv, state, ln_g, ln_b, seq, tm, fc):
    t = h.shape[0]
    n_ff = D_FF // fc
    nt = t // tm
    if seq >= tm:
        tiles_per_seq = seq // tm
        st_spec = pl.BlockSpec((1, 2, fc), lambda i, c: (i // tiles_per_seq, 0, c))
        last_spec = pl.BlockSpec((1, 2, fc), lambda i, c: (i, 0, c))
        last_shape = jax.ShapeDtypeStruct((nt, 2, D_FF), F32)
    else:
        ns = tm // seq
        st_spec = pl.BlockSpec((2, ns, fc), lambda i, c: (0, i, c))
        last_spec = pl.BlockSpec((2, ns, fc), lambda i, c: (0, i, c))
        last_shape = jax.ShapeDtypeStruct((2, t // seq, D_FF), F32)
    kern = functools.partial(_ffn_kernel, tm=tm, fc=fc, seq=seq, n_ff=n_ff)
    return pl.pallas_call(
        kern,
        grid=(nt, n_ff),
        in_specs=[pl.BlockSpec((tm, D_MODEL), lambda i, c: (i, 0), pipeline_mode=pl.Buffered(1)),
                  pl.BlockSpec((D_MODEL, fc), lambda i, c: (0, c)),
                  pl.BlockSpec((D_MODEL, fc), lambda i, c: (0, c)),
                  pl.BlockSpec((fc, D_MODEL), lambda i, c: (c, 0)),
                  pl.BlockSpec((CONV_W, fc), lambda i, c: (0, c)),
                  pl.BlockSpec((1, fc), lambda i, c: (0, c)),
                  st_spec,
                  pl.BlockSpec((1, D_MODEL), lambda i, c: (0, 0)),
                  pl.BlockSpec((1, D_MODEL), lambda i, c: (0, 0))],
        out_specs=[pl.BlockSpec((tm, D_MODEL), lambda i, c: (i, 0)), last_spec],
        out_shape=[jax.ShapeDtypeStruct((t, D_MODEL), F32), last_shape],
        scratch_shapes=[pltpu.VMEM((tm, D_MODEL), BF), pltpu.VMEM((n_ff, 2, fc), F32),
                        pltpu.VMEM((fc // 128, tm, 128), F32)],
        compiler_params=_cparams(2),
        name="conv_ffn",
    )(h, w_gate_b, w_up_b, w_down_b, w_conv, b_conv.reshape(1, D_FF), state,
      ln_g.reshape(1, D_MODEL), ln_b.reshape(1, D_MODEL))


def _after_mixers(x, od, os_, memk, memv, conv_state, n_batch, t_b, w):
    h1, qc = _proj_ln([od, os_], w['w_o'], x, w['ln1_g'], w['ln1_b'], MID_ROWS, "out_proj_ln1_q", wq_b=w['w_mq'])
    oc = _cross_attn(qc, memk, memv, n_batch, t_b, min(t_b, MID_ROWS))
    (h2,) = _proj_ln([oc], w['w_mo'], h1, w['ln2_g'], w['ln2_b'], MID_ROWS, "cross_out_ln2")
    return _ffn(h2, w['w_gate'], w['w_up'], w['w_down'], w['w_conv'], w['b_conv'], conv_state,
                w['ln3_g'], w['ln3_b'], t_b, FFN_ROWS, FFN_CHUNK)


def kernel(x_prompt, x_sample, mem_prompt, cache_diff_k, cache_diff_v, cache_sb_k, cache_sb_v, cache_mem_k, cache_mem_v, state_ffn_conv, w_in, lambda_q1, lambda_k1, lambda_q2, lambda_k2, subln_g, w_o, ln1_g, ln1_b, w_mq, w_mk, w_mv, w_mo, ln2_g, ln2_b, w_gate, w_up, w_conv, b_conv, w_down, ln3_g, ln3_b):
    assert w_in.shape[0] == DEPTH == 1
    n_p, t_p, _ = x_prompt.shape
    n_s, t_s, _ = x_sample.shape
    past = cache_diff_k.shape[2]
    n_mem = mem_prompt.shape[1]
    l = 0
    lam_init = 0.8 - 0.6 * math.exp(-0.3 * l)
    lam = (jnp.exp(jnp.sum(lambda_q1[l] * lambda_k1[l])) - jnp.exp(jnp.sum(lambda_q2[l] * lambda_k2[l]))
           + lam_init).reshape(1).astype(F32)
    w = {'w_in': w_in[l].astype(BF), 'w_conv': w_conv[l], 'b_conv': b_conv[l],
         'subln_g': subln_g[l], 'ln1_g': ln1_g[l], 'ln1_b': ln1_b[l], 'ln2_g': ln2_g[l], 'ln2_b': ln2_b[l],
         'ln3_g': ln3_g[l], 'ln3_b': ln3_b[l]}
    late_weights = {'w_o': w_o[l], 'w_mq': w_mq[l], 'w_mo': w_mo[l], 'w_gate': w_gate[l], 'w_up': w_up[l],
                    'w_down': w_down[l]}
    tm = 512

    w_mkv = jnp.concatenate([w_mk[l], w_mv[l]], axis=1).astype(BF)
    mem_b = mem_prompt.reshape(n_p * n_mem, D_MODEL).astype(BF)
    (mkv_f,) = _matmul(mem_b, w_mkv, [F32], n_p * n_mem, 1024, "memory_kv")
    xp = x_prompt.reshape(n_p * t_p, D_MODEL)
    pb, pf = _inproj(xp, w['w_in'], np.arange(t_p), n_p, INPROJ_ROWS_PROMPT, sb_interleave=SB_BLOCK)
    odp, late_b = _diff_attn_prompt(lam, pb['dq'], pb['dk'], pb['dv'], w['subln_g'], n_p, t_p, lam_init,
                                    cast_weights=tuple(late_weights.values()))
    w.update(zip(late_weights, late_b))
    osp = _sb_attn_prompt(pb['sq'], pb['sk'], pb['sv'], n_p, t_p)
    zero_state = jnp.zeros((n_p, CONV_W - 1, D_FF), F32)
    mem_heads = (n_p, n_mem, N_MEM_HEADS, MEM_HEAD_DIM)
    yp, lastp = _after_mixers(xp, odp, osp, _tiled_lane_rows(mkv_f[:, :D_MODEL].reshape(mem_heads), n_p, N_MEM_HEADS),
                              _tiled_lane_rows(mkv_f[:, D_MODEL:].reshape(mem_heads), n_p, N_MEM_HEADS),
                              zero_state, n_p, t_p, w)

    xs = x_sample.reshape(n_s * t_s, D_MODEL)
    sb, sf = _inproj(xs, w['w_in'], past + np.arange(t_s), n_s, tm)
    rows = lambda c: _as_lane_rows(c[l], n_s)
    ods = _diff_attn_sample(lam, sb['dq'], sb['dk'], sb['dv'], rows(cache_diff_k),
                            _tiled_lane_rows(cache_diff_v[l], n_s, N_DIFF_HEADS), w['subln_g'],
                            n_s, t_s, past, lam_init)
    oss = _sb_attn_sample(sb['sq'], sb['sk'], sb['sv'], rows(cache_sb_k), rows(cache_sb_v), n_s, t_s, past)
    state_s = jnp.transpose(state_ffn_conv[l], (1, 0, 2))
    ys, lasts = _after_mixers(xs, ods, oss, _tiled_lane_rows(cache_mem_k[l], n_s, N_MEM_HEADS),
                              _tiled_lane_rows(cache_mem_v[l], n_s, N_MEM_HEADS), state_s, n_s, t_s, w)

    tiles_per_seq = t_p // FFN_ROWS
    new_conv_p = lastp[tiles_per_seq - 1::tiles_per_seq]
    new_conv_s = jnp.transpose(lasts, (1, 0, 2))

    def new_rows(f, n, t):
        dv = jnp.transpose(f['dv'].reshape(n, t, DIFF_V_DIM // HEAD_DIM, N_DIFF_HEADS, HEAD_DIM), (0, 1, 3, 2, 4))
        return (f['dk'].reshape(1, n, t, N_DIFF_HEADS, 2, HEAD_DIM), dv.reshape(1, n, t, N_DIFF_HEADS, DIFF_V_DIM),
                f['sk'].reshape(1, n, t, N_SB_HEADS, HEAD_DIM), f['sv'].reshape(1, n, t, N_SB_HEADS, HEAD_DIM))

    return ((yp.reshape(n_p, t_p, D_MODEL), ys.reshape(n_s, t_s, D_MODEL))
            + new_rows(pf, n_p, t_p)
            + (mkv_f[:, :D_MODEL].reshape((1,) + mem_heads), mkv_f[:, D_MODEL:].reshape((1,) + mem_heads),
               new_conv_p[None])
            + new_rows(sf, n_s, t_s)
            + (new_conv_s[None],))
```

```python
import functools
import math

import jax
import jax.numpy as jnp
import numpy as np
from jax import lax
from jax.experimental import pallas as pl
from jax.experimental.pallas import tpu as pltpu

BF = jnp.bfloat16
F32 = jnp.float32

D_MODEL = 2048
CHUNK = 64
CHUNK_SHIFT = 6
HEAD_DIM = 128
N_DIFF_HEADS = 4
N_SB_HEADS = 8
DIFF_V_DIM = 2 * HEAD_DIM
ROPE_DIM = HEAD_DIM // 4
ROPE_THETA = 500000.0
N_MEM_HEADS = 4
MEM_HEAD_DIM = D_MODEL // N_MEM_HEADS
D_FF = 5632
CONV_W = 3
LN_EPS = 1e-5
DEPTH = 1
DN_ALPHA = (2 * DEPTH) ** 0.25
SEG = 1024
N_SEG = 6
VMEM_LIMIT = 56 * 1024 * 1024
LOG2E = math.log2(math.e)
QK_SCALE2 = HEAD_DIM ** -0.5 * LOG2E
NEG_BIG = -1e30
LN_ROWS = 128
FFN_COLS = 256
FFN_ROWS = 1024
FFN_CHUNK = 512
INPROJ_ROWS_PROMPT = 1024
INPROJ_SLICES = 4
MID_ROWS = 512
SB_BLOCK = 256

_NT = (((1,), (1,)), ((), ()))


def _cparams(n_axes):
    return pltpu.CompilerParams(dimension_semantics=("arbitrary",) * n_axes,
                                vmem_limit_bytes=VMEM_LIMIT)


def _layer_norm(y, g, b):
    mu = jnp.mean(y, axis=-1, keepdims=True)
    yc = y - mu
    var = jnp.mean(yc * yc, axis=-1, keepdims=True)
    return yc * lax.rsqrt(var + LN_EPS) * g + b


SLOTS = SEG // HEAD_DIM


def _inproj_kernel(*refs, rope, want_f32, tile_rows, scale, emit_xb, interleave):
    x_ref, w_ref = refs[:2]
    tabs = refs[2:5] if rope else None
    outs = refs[5:] if rope else refs[2:]
    if interleave:
        outs, perm_ref = outs[:-1], outs[-1]
    pb_ref = outs[0]
    xb = x_ref[...].astype(BF)
    if emit_xb:
        outs[-1][...] = xb
    tm = xb.shape[0]
    hm = max(tm // INPROJ_SLICES, SB_BLOCK)
    starts = range(0, tm, hm)
    accs = [jnp.dot(xb[r0:r0 + hm, :], w_ref[...], preferred_element_type=F32) for r0 in starts]
    for r0, acc in zip(starts, accs):
        for ch in range(SLOTS):
            lanes = slice(ch * HEAD_DIM, (ch + 1) * HEAD_DIM)
            r = acc[:, lanes]
            if rope:
                c_ref, s1_ref, s2_ref = tabs
                r = (r * c_ref[r0:r0 + hm, :] + pltpu.roll(r, HEAD_DIM - ROPE_DIM // 2, 1) * s1_ref[r0:r0 + hm, :]
                     + pltpu.roll(r, ROPE_DIM // 2, 1) * s2_ref[r0:r0 + hm, :])
            elif scale != 1.0:
                r = r * scale
            if interleave:
                run = interleave // SUBLANES
                for b0 in range(0, hm, interleave):
                    for s in range(SUBLANES):
                        perm_ref[ch, pl.ds(r0 + b0 + s, run, stride=SUBLANES), :] = r[b0 + s * run:b0 + (s + 1) * run, :]
                pb_ref[0, r0:r0 + hm, lanes] = perm_ref[ch, r0:r0 + hm, :].astype(BF)
            else:
                pb_ref[0, r0:r0 + hm, lanes] = r.astype(BF)
            if want_f32:
                head, col_tile = divmod(ch, SLOTS // tile_rows) if tile_rows else (0, 0)
                off = col_tile * tile_rows + head if tile_rows else ch
                outs[1][pl.ds(r0 * SLOTS + off, hm, stride=SLOTS), :] = r


def _rope_tables(pos, n_rep):
    half = ROPE_DIM // 2
    inv_freq = np.power(ROPE_THETA, -np.arange(half, dtype=np.float64) * (2.0 / ROPE_DIM))
    ang = pos.astype(np.float64)[:, None] * inv_freq[None, :]
    cos, sin = jnp.asarray(np.cos(ang), F32), jnp.asarray(np.sin(ang), F32)
    n = pos.shape[0]
    c = jnp.concatenate([cos, cos, jnp.ones((n, HEAD_DIM - ROPE_DIM), F32)], axis=1)
    s1 = jnp.concatenate([-sin, jnp.zeros((n, HEAD_DIM - half), F32)], axis=1)
    s2 = jnp.concatenate([jnp.zeros((n, half), F32), sin, jnp.zeros((n, HEAD_DIM - ROPE_DIM), F32)], axis=1)
    return tuple(jnp.tile(tb, (n_rep, 1)) for tb in (c, s1, s2))


IN_GROUPS = (("dq", True, False, 0, QK_SCALE2), ("dk", True, True, 0, 1.0), ("dv", False, True, N_DIFF_HEADS, 1.0),
             ("sq", False, False, 0, QK_SCALE2), ("sk", False, True, 0, 1.0), ("sv", False, True, 0, 1.0))


def _inproj(x, w_in_b, pos, n_seq, tm, sb_interleave=0):
    t = x.shape[0]
    tabs = _rope_tables(pos, n_seq)
    row = lambda i: (i, 0)
    tab = pl.BlockSpec((tm, HEAD_DIM), row)
    bf, f32 = {}, {}
    xb = None
    for j, (name, rope, want_f32, tile_rows, scale) in enumerate(IN_GROUPS):
        assert not (want_f32 and scale != 1.0)
        emit_xb = xb is None
        interleave = sb_interleave if name in ("sk", "sv") else 0
        outs = pl.pallas_call(
            functools.partial(_inproj_kernel, rope=rope, want_f32=want_f32, tile_rows=tile_rows, scale=scale,
                              emit_xb=emit_xb, interleave=interleave),
            scratch_shapes=[pltpu.VMEM((SLOTS, tm, HEAD_DIM), F32)] if interleave else [],
            grid=(t // tm,),
            in_specs=[pl.BlockSpec((tm, D_MODEL), row),
                      pl.BlockSpec((D_MODEL, SEG), functools.partial(lambda i, j: (0, j), j=j))]
                     + ([tab, tab, tab] if rope else []),
            out_specs=[pl.BlockSpec((1, tm, SEG), lambda i: (0, i, 0))]
                      + ([pl.BlockSpec((tm * SLOTS, HEAD_DIM), row)] if want_f32 else [])
                      + ([pl.BlockSpec((tm, D_MODEL), row)] if emit_xb else []),
            out_shape=[jax.ShapeDtypeStruct((1, t, SEG), BF)]
                      + ([jax.ShapeDtypeStruct((t * SLOTS, HEAD_DIM), F32)] if want_f32 else [])
                      + ([jax.ShapeDtypeStruct((t, D_MODEL), BF)] if emit_xb else []),
            compiler_params=_cparams(1),
            name="inproj_" + name,
        )(x if emit_xb else xb, w_in_b, *([tb * scale for tb in tabs] if rope else ()))
        bf[name] = outs[0]
        if want_f32:
            f32[name] = outs[1]
        if emit_xb:
            xb = outs[-1]
    return bf, f32


def _matmul_kernel(x_ref, w_ref, *o_refs):
    acc = jnp.dot(x_ref[...], w_ref[...], preferred_element_type=F32)
    for o in o_refs:
        o[...] = acc.astype(o.dtype)


def _matmul(x_b, w_b, out_dtypes, tm, tn, name):
    t, k = x_b.shape
    n = w_b.shape[1]
    return pl.pallas_call(
        _matmul_kernel,
        grid=(t // tm, n // tn),
        in_specs=[pl.BlockSpec((tm, k), lambda i, j: (i, 0)),
                  pl.BlockSpec((k, tn), lambda i, j: (0, j))],
        out_specs=[pl.BlockSpec((tm, tn), lambda i, j: (i, j)) for _ in out_dtypes],
        out_shape=[jax.ShapeDtypeStruct((t, n), d) for d in out_dtypes],
        compiler_params=_cparams(2),
        name=name,
    )(x_b, w_b)


def _chunk_visible(q0, k0, nq, nk):
    qpos = q0 + lax.broadcasted_iota(jnp.int32, (nq, nk), 0)
    kpos = k0 + lax.broadcasted_iota(jnp.int32, (nq, nk), 1)
    return lax.shift_right_logical(kpos, CHUNK_SHIFT) <= lax.shift_right_logical(qpos, CHUNK_SHIFT)


def _subln(o0, o1, lam, g, lam_init):
    of = o0 - lam * o1
    return of * lax.rsqrt(jnp.mean(of * of, axis=-1, keepdims=True) + LN_EPS) * g * (1.0 - lam_init)


def _diff_prompt_kernel(*refs, tq, sub, lam_init, n_cast):
    lam_ref, q_ref, k_ref, v_ref, g_ref = refs[:5]
    o_ref = refs[5 + n_cast]
    vt_ref, m_ref, l_ref, acc_ref = refs[6 + 2 * n_cast:]
    for w_ref, wb_ref in zip(refs[5:5 + n_cast], refs[6 + n_cast:6 + 2 * n_cast]):
        wb_ref[...] = w_ref[...].astype(BF)
    qi = pl.program_id(2)
    n_sub = tq // sub

    @pl.when(qi == 0)
    def _():
        for g in range(vt_ref.shape[0]):
            for j in range(n_sub):
                blk = v_ref[0, g * tq + j * sub:g * tq + (j + 1) * sub, :]
                vt_ref[g, :, j * sub:(j + 1) * sub] = blk.astype(F32).T.astype(BF)

    m_ref[...] = jnp.full(m_ref.shape, NEG_BIG, F32)
    l_ref[...] = jnp.zeros(l_ref.shape, F32)
    acc_ref[...] = jnp.zeros(acc_ref.shape, F32)

    def run(g, items):
        k0 = pl.multiple_of(g * tq, tq)
        scores = []
        for c, r, nk, vis in items:
            cs = slice(c * HEAD_DIM, (c + 1) * HEAD_DIM)
            s = lax.dot_general(k_ref[0, pl.ds(k0, nk), cs], q_ref[0, r * sub:(r + 1) * sub, cs], _NT,
                                preferred_element_type=F32)
            if vis is not None:
                tail = jnp.where(vis, s[nk - sub:, :], -jnp.inf)
                s = tail if nk == sub else jnp.concatenate([s[:nk - sub, :], tail], axis=0)
            scores.append(s)
        probs = []
        for s, (c, r, nk, _) in zip(scores, items):
            i = c * n_sub + r
            m_old = m_ref[i]
            m_new = jnp.maximum(m_old, jnp.max(s, axis=0, keepdims=True))
            alpha = jnp.exp2(m_old - m_new)
            p = jnp.exp2(s - m_new)
            l_ref[i] = alpha * l_ref[i] + jnp.sum(p, axis=0, keepdims=True)
            m_ref[i] = m_new
            probs.append((alpha, p.astype(BF)))
        for (alpha, p), (c, r, nk, _) in zip(probs, items):
            i = c * n_sub + r
            acc_ref[i] = alpha * acc_ref[i] + jnp.dot(vt_ref[g, :, :nk], p, preferred_element_type=F32)

    full_items = [(c, r, tq, None) for r in range(n_sub) for c in range(2)]

    def body(g, carry):
        run(g, full_items)
        return carry

    lax.fori_loop(0, qi, body, 0)

    k_i = lax.broadcasted_iota(jnp.int32, (sub, sub), 0)
    q_i = lax.broadcasted_iota(jnp.int32, (sub, sub), 1)
    vis = lax.shift_right_logical(k_i, CHUNK_SHIFT) <= lax.shift_right_logical(q_i, CHUNK_SHIFT)
    run(qi, [(c, r, (r + 1) * sub, vis) for r in range(n_sub) for c in range(2)])

    for r in range(n_sub):
        o0 = (acc_ref[r] / l_ref[r]).T
        o1 = (acc_ref[n_sub + r] / l_ref[n_sub + r]).T
        o_ref[r * sub:(r + 1) * sub, :] = _subln(o0, o1, lam_ref[0], g_ref[...], lam_init).astype(BF)


def _diff_attn_prompt(lam, q, k, v, subln_g, n_batch, t, lam_init, cast_weights=(), tq=1024, sub=256):
    assert t % tq == 0 and tq % sub == 0 and sub % CHUNK == 0
    nq = t // tq
    n_steps = n_batch * N_DIFF_HEADS * nq
    step = lambda b, h, i: ((b * N_DIFF_HEADS + h) * nq + i, 0)
    cast_specs = []
    for wt in cast_weights:
        assert wt.shape[0] % (16 * n_steps) == 0
        cast_specs.append(pl.BlockSpec((wt.shape[0] // n_steps, wt.shape[1]), step))
    kern = functools.partial(_diff_prompt_kernel, tq=tq, sub=sub, lam_init=lam_init, n_cast=len(cast_weights))
    outs = pl.pallas_call(
        kern,
        grid=(n_batch, N_DIFF_HEADS, nq),
        in_specs=[pl.BlockSpec(memory_space=pltpu.SMEM),
                  pl.BlockSpec((1, tq, DIFF_V_DIM), lambda b, h, i: (0, b * nq + i, h)),
                  pl.BlockSpec((1, t, DIFF_V_DIM), lambda b, h, i: (0, b, h)),
                  pl.BlockSpec((1, t, DIFF_V_DIM), lambda b, h, i: (0, b, h)),
                  pl.BlockSpec((1, DIFF_V_DIM), lambda b, h, i: (0, 0))] + cast_specs,
        out_specs=[pl.BlockSpec((tq, DIFF_V_DIM), lambda b, h, i: (b * nq + i, h))] + cast_specs,
        out_shape=[jax.ShapeDtypeStruct((n_batch * t, SEG), BF)]
                  + [jax.ShapeDtypeStruct(wt.shape, BF) for wt in cast_weights],
        scratch_shapes=[pltpu.VMEM((nq, DIFF_V_DIM, tq), BF),
                        pltpu.VMEM((2 * tq // sub, 1, sub), F32), pltpu.VMEM((2 * tq // sub, 1, sub), F32),
                        pltpu.VMEM((2 * tq // sub, DIFF_V_DIM, sub), F32)],
        compiler_params=_cparams(3),
        name="diff_attn_prompt",
    )(lam, q, k, v, subln_g.reshape(1, DIFF_V_DIM), *cast_weights)
    return outs[0], outs[1:]


def _diff_sample_kernel(lam_ref, q_ref, kn_ref, vn_ref, kc_ref, vc_ref, g_ref, o_ref, *, t_s, past, lam_init):
    vis = _chunk_visible(past, past, t_s, t_s)
    for h in range(N_DIFF_HEADS):
        vs = slice(h * DIFF_V_DIM, (h + 1) * DIFF_V_DIM)
        v_old = _lane_rows(vc_ref, 0, past, h, DIFF_V_DIM, SEG, tiled=True).astype(BF)
        v_new = vn_ref[0, :, vs]
        o = []
        for c in range(2):
            cs = slice((2 * h + c) * HEAD_DIM, (2 * h + c + 1) * HEAD_DIM)
            q = q_ref[0, :, cs]
            k_old = _lane_rows(kc_ref, 0, past, 2 * h + c, HEAD_DIM, SEG).astype(BF)
            s_old = lax.dot_general(q, k_old, _NT, preferred_element_type=F32)
            s_new = lax.dot_general(q, kn_ref[0, :, cs], _NT, preferred_element_type=F32)
            s_new = jnp.where(vis, s_new, -jnp.inf)
            m = jnp.maximum(jnp.max(s_old, axis=-1, keepdims=True), jnp.max(s_new, axis=-1, keepdims=True))
            p_old = jnp.exp2(s_old - m)
            p_new = jnp.exp2(s_new - m)
            l = jnp.sum(p_old, axis=-1, keepdims=True) + jnp.sum(p_new, axis=-1, keepdims=True)
            acc = (jnp.dot(p_old.astype(BF), v_old, preferred_element_type=F32)
                   + jnp.dot(p_new.astype(BF), v_new, preferred_element_type=F32))
            o.append(acc / l)
        o_ref[:, vs] = _subln(o[0], o[1], lam_ref[0], g_ref[...], lam_init).astype(BF)


def _new_rows_spec(t_s):
    return pl.BlockSpec((1, t_s, SEG), lambda b: (0, b, 0))


def _as_lane_rows(x, n_batch):
    return x.reshape(n_batch, -1, HEAD_DIM)


def _lane_rows(ref, t0, nt, slot, slot_width, width, tiled=False):
    per_row = width // HEAD_DIM
    per_slot = slot_width // HEAD_DIM
    n_slots = width // slot_width
    offs = [(k * n_slots + slot) if tiled else (slot * per_slot + k) for k in range(per_slot)]
    parts = [ref[0, pl.ds(t0 * per_row + o, nt, stride=per_row), :] for o in offs]
    return parts[0] if per_slot == 1 else jnp.concatenate(parts, axis=1)


def _tiled_lane_rows(x, n_batch, n_slots):
    rows, w = x.shape[1], x.shape[3]
    x = x.reshape(n_batch, rows, n_slots, w // HEAD_DIM, HEAD_DIM)
    return jnp.transpose(x, (0, 1, 3, 2, 4)).reshape(n_batch, -1, HEAD_DIM)


def _cache_spec(cache):
    return pl.BlockSpec((1,) + cache.shape[1:], lambda b: (b, 0, 0))


def _diff_attn_sample(lam, q, k_new, v_new, cache_k, cache_v, subln_g, n_batch, t_s, past, lam_init):
    kern = functools.partial(_diff_sample_kernel, t_s=t_s, past=past, lam_init=lam_init)
    return pl.pallas_call(
        kern,
        grid=(n_batch,),
        in_specs=[pl.BlockSpec(memory_space=pltpu.SMEM), _new_rows_spec(t_s), _new_rows_spec(t_s),
                  _new_rows_spec(t_s), _cache_spec(cache_k), _cache_spec(cache_v),
                  pl.BlockSpec((1, DIFF_V_DIM), lambda b: (0, 0))],
        out_specs=pl.BlockSpec((t_s, SEG), lambda b: (b, 0)),
        out_shape=jax.ShapeDtypeStruct((n_batch * t_s, SEG), BF),
        compiler_params=_cparams(1),
        name="diff_attn_sample",
    )(lam, q, k_new, v_new, cache_k, cache_v, subln_g.reshape(1, DIFF_V_DIM))


def _sb_neg_log_keep(z2):
    neg_abs = jnp.minimum(z2, -z2)
    return jnp.maximum(z2, 0.0) + jnp.log(1.0 + jnp.exp2(neg_abs)) * LOG2E


def _split_bf16(x):
    hi = x.astype(BF)
    return hi, (x - hi.astype(F32)).astype(BF)


SUBLANES = 8


def _interleaved_key(n):
    r = lax.broadcasted_iota(jnp.int32, (n, 1), 0)
    return jnp.bitwise_and(r, SUBLANES - 1) * (n // SUBLANES) + lax.shift_right_logical(r, 3)


def _suffix_interleaved(x, carry):
    nj = x.shape[0] // SUBLANES
    run = [None] * nj
    acc = None
    for j in reversed(range(nj)):
        part = x[SUBLANES * j:SUBLANES * (j + 1), :]
        acc = part if acc is None else acc + part
        run[j] = acc
    g = run[0]
    row = lax.broadcasted_iota(jnp.int32, g.shape, 0)
    incl = g
    for d in (1, 2, 4):
        incl = incl + jnp.where(row < SUBLANES - d, pltpu.roll(incl, SUBLANES - d, 0), 0.0)
    later = incl - g + carry
    return jnp.concatenate([r + later for r in run], axis=0), incl[0:1, :]


def _sb_group(items, carry, acc, key_axis):
    if key_axis == 0:
        zs = [lax.dot_general(k, q, _NT, preferred_element_type=F32) for q, k, _, _, _, _ in items]
    else:
        zs = [lax.dot_general(q, k, _NT, preferred_element_type=F32) for q, k, _, _, _, _ in items]
    splits = []
    for z, (_, _, _, tri, before, _) in zip(zs, items):
        nlk = _sb_neg_log_keep(z)
        if before is not None:
            nlk = jnp.where(before, nlk, 0.0)
        if tri is None:
            splits.append(nlk)
            continue
        hi, lo = _split_bf16(nlk)
        fused = tri.shape[1 - key_axis] == 2 * tri.shape[key_axis]
        splits.append(jnp.concatenate([hi, lo], axis=key_axis) if fused else (hi, lo))
    sufs = []
    for hl, (_, _, _, tri, _, ch) in zip(splits, items):
        if tri is None:
            suf, total = _suffix_interleaved(hl, carry[ch])
        else:
            if key_axis == 0:
                mm = lambda x: jnp.dot(tri, x, preferred_element_type=F32)
            else:
                mm = lambda x: jnp.dot(x, tri, preferred_element_type=F32)
            suf = mm(hl[0]) + mm(hl[1]) if isinstance(hl, tuple) else mm(hl)
            total = suf[0:1, :] if key_axis == 0 else suf[:, 0:1]
            suf = suf + carry[ch]
        sufs.append(suf)
        carry[ch] = carry[ch] + total
    probs = []
    for z, suf, (_, _, _, _, before, ch) in zip(zs, sufs, items):
        a = jnp.exp2(z - suf)
        if before is not None:
            a = jnp.where(before, a, 0.0)
        probs.append(a.astype(BF))
    for a, (_, _, v_op, _, _, ch) in zip(probs, items):
        if key_axis == 0:
            acc[ch] = acc[ch] + jnp.dot(v_op, a, preferred_element_type=F32)
        else:
            acc[ch] = acc[ch] + jnp.dot(a, v_op, preferred_element_type=F32)


def _tri(n, upper):
    r = lax.broadcasted_iota(jnp.int32, (n, n), 0)
    c = lax.broadcasted_iota(jnp.int32, (n, n), 1)
    return jnp.where((c >= r) if upper else (r >= c), 1.0, 0.0).astype(BF)


def _sb_prompt_kernel(q_ref, k_ref, v_ref, o_ref, vt_ref, carry_ref, acc_ref, *, tq, sub, t, unroll):
    qi = pl.program_id(2)
    n_sub = tq // sub

    @pl.when(qi == 0)
    def _():
        for kb in range(t // sub):
            vt_ref[kb] = v_ref[0, kb * sub:(kb + 1) * sub, :].astype(F32).T.astype(BF)

    tri2 = None
    before = _interleaved_key(sub) < lax.broadcasted_iota(jnp.int32, (sub, sub), 1)
    qs = [q_ref[0, c * sub:(c + 1) * sub, :] for c in range(n_sub)]
    carry = [jnp.zeros((1, sub), F32) for _ in range(n_sub)]
    acc = [jnp.zeros((HEAD_DIM, sub), F32) for _ in range(n_sub)]

    base = qi * n_sub
    items = []
    for j in reversed(range(n_sub)):
        kblk = k_ref[0, pl.ds(pl.multiple_of((base + j) * sub, sub), sub), :]
        vt_blk = vt_ref[base + j]
        items += [(qs[c], kblk, vt_blk, tri2, before if c == j else None, c) for c in range(j, n_sub)]
    _sb_group(items, carry, acc, 0)
    for c in range(n_sub):
        carry_ref[c] = carry[c]
        acc_ref[c] = acc[c]

    def body(i, x):
        cr = [carry_ref[c] for c in range(n_sub)]
        ac = [acc_ref[c] for c in range(n_sub)]
        items = []
        for uu in range(unroll):
            kb = base - 1 - (i * unroll + uu)
            kblk = k_ref[0, pl.ds(pl.multiple_of(kb * sub, sub), sub), :]
            vt_blk = vt_ref[kb]
            items += [(qs[c], kblk, vt_blk, tri2, None, c) for c in range(n_sub)]
        _sb_group(items, cr, ac, 0)
        for c in range(n_sub):
            carry_ref[c] = cr[c]
            acc_ref[c] = ac[c]
        return x

    lax.fori_loop(0, (qi * n_sub) // unroll, body, 0)
    for c in range(n_sub):
        o_ref[c * sub:(c + 1) * sub, :] = acc_ref[c].T.astype(BF)


def _sb_attn_prompt(q, k, v, n_batch, t, tq=1024, sub=SB_BLOCK, unroll=2):
    assert t % tq == 0 and tq % sub == 0 and (tq // sub) % unroll == 0
    nq = t // tq
    n_sub = tq // sub
    kern = functools.partial(_sb_prompt_kernel, tq=tq, sub=sub, t=t, unroll=unroll)
    return pl.pallas_call(
        kern,
        grid=(n_batch, N_SB_HEADS, nq),
        in_specs=[pl.BlockSpec((1, tq, HEAD_DIM), lambda b, h, i: (0, b * nq + i, h)),
                  pl.BlockSpec((1, t, HEAD_DIM), lambda b, h, i: (0, b, h)),
                  pl.BlockSpec((1, t, HEAD_DIM), lambda b, h, i: (0, b, h))],
        out_specs=pl.BlockSpec((tq, HEAD_DIM), lambda b, h, i: (b * nq + i, h)),
        out_shape=jax.ShapeDtypeStruct((n_batch * t, SEG), BF),
        scratch_shapes=[pltpu.VMEM((t // sub, HEAD_DIM, sub), BF), pltpu.VMEM((n_sub, 1, sub), F32),
                        pltpu.VMEM((n_sub, HEAD_DIM, sub), F32)],
        compiler_params=_cparams(3),
        name="sb_attn_prompt",
    )(q, k, v)


def _sb_sample_kernel(q_ref, kn_ref, vn_ref, kc_ref, vc_ref, o_ref, *, t_s, past, tk):
    tri_new = _tri(t_s, False)
    l_old = _tri(tk, False)
    tri_old = jnp.concatenate([l_old, l_old], axis=0)
    before = (lax.broadcasted_iota(jnp.int32, (t_s, t_s), 1) < lax.broadcasted_iota(jnp.int32, (t_s, t_s), 0))
    heads = range(N_SB_HEADS)
    cols = [slice(h * HEAD_DIM, (h + 1) * HEAD_DIM) for h in heads]
    qs = [q_ref[0, :, cs] for cs in cols]
    carry = [jnp.zeros((t_s, 1), F32) for _ in heads]
    acc = [jnp.zeros((t_s, HEAD_DIM), F32) for _ in heads]
    _sb_group([(qs[h], kn_ref[0, :, cols[h]], vn_ref[0, :, cols[h]], tri_new, before, h) for h in heads],
              carry, acc, 1)
    for kb in reversed(range(past // tk)):
        rows = slice(kb * tk, (kb + 1) * tk)
        _sb_group([(qs[h], _lane_rows(kc_ref, kb * tk, tk, h, HEAD_DIM, SEG).astype(BF),
                    _lane_rows(vc_ref, kb * tk, tk, h, HEAD_DIM, SEG).astype(BF), tri_old, None, h)
                   for h in heads], carry, acc, 1)
    for h in heads:
        o_ref[:, cols[h]] = acc[h].astype(BF)


def _sb_attn_sample(q, k_new, v_new, cache_k, cache_v, n_batch, t_s, past, tk=256):
    assert past % tk == 0
    kern = functools.partial(_sb_sample_kernel, t_s=t_s, past=past, tk=tk)
    return pl.pallas_call(
        kern,
        grid=(n_batch,),
        in_specs=[_new_rows_spec(t_s), _new_rows_spec(t_s), _new_rows_spec(t_s),
                  _cache_spec(cache_k), _cache_spec(cache_v)],
        out_specs=pl.BlockSpec((t_s, SEG), lambda b: (b, 0)),
        out_shape=jax.ShapeDtypeStruct((n_batch * t_s, SEG), BF),
        compiler_params=_cparams(1),
        name="sb_attn_sample",
    )(q, k_new, v_new, cache_k, cache_v)


MEM_SCALE2 = MEM_HEAD_DIM ** -0.5 * LOG2E


def _proj_ln_kernel(*refs, n_in, with_q):
    a_refs = refs[:n_in]
    w_refs = refs[n_in:2 * n_in]
    res_ref, g_ref, b_ref = refs[2 * n_in:2 * n_in + 3]
    rest = refs[2 * n_in + 3:]
    tm = res_ref.shape[0]
    slices = [slice(r, r + LN_ROWS) for r in range(0, tm, LN_ROWS)]
    accs = []
    for rows in slices:
        acc = None
        for a, w in zip(a_refs, w_refs):
            d = jnp.dot(a[rows, :], w[...], preferred_element_type=F32)
            acc = d if acc is None else acc + d
        accs.append(acc)
    outs = [_layer_norm(DN_ALPHA * res_ref[rows, :] + acc, g_ref[...], b_ref[...]) for rows, acc in zip(slices, accs)]
    if with_q:
        wq_ref, of_ref, oq_ref = rest
        for rows, out in zip(slices, outs):
            of_ref[rows, :] = out
        for rows, out in zip(slices, outs):
            q = jnp.dot(out.astype(BF), wq_ref[...], preferred_element_type=F32)
            oq_ref[rows, :] = (q * MEM_SCALE2).astype(BF)
    else:
        (of_ref,) = rest
        for rows, out in zip(slices, outs):
            of_ref[rows, :] = out


def _proj_ln(a_list, w_b, res, g, b, tm, name, wq_b=None):
    t = res.shape[0]
    n_in = len(a_list)
    ka = a_list[0].shape[1]
    row = lambda i: (i, 0)
    once = dict(pipeline_mode=pl.Buffered(1))
    full = pl.BlockSpec((tm, D_MODEL), row)
    vec = pl.BlockSpec((1, D_MODEL), lambda i: (0, 0))
    in_specs = ([pl.BlockSpec((tm, ka), row) for _ in a_list]
                + [pl.BlockSpec((ka, D_MODEL), functools.partial(lambda i, r: (r, 0), r=r), **once) for r in range(n_in)]
                + [full, vec, vec])
    args = [*a_list, *([w_b] * n_in), res, g.reshape(1, D_MODEL), b.reshape(1, D_MODEL)]
    out_specs = [full]
    out_shape = [jax.ShapeDtypeStruct((t, D_MODEL), F32)]
    if wq_b is not None:
        in_specs.append(pl.BlockSpec((D_MODEL, D_MODEL), lambda i: (0, 0), **once))
        args.append(wq_b)
        out_specs.append(full)
        out_shape.append(jax.ShapeDtypeStruct((t, D_MODEL), BF))
    return pl.pallas_call(
        functools.partial(_proj_ln_kernel, n_in=n_in, with_q=wq_b is not None),
        grid=(t // tm,),
        in_specs=in_specs,
        out_specs=out_specs,
        out_shape=out_shape,
        compiler_params=_cparams(1),
        name=name,
    )(*args)


def _cross_kernel(q_ref, k_ref, v_ref, o_ref):
    n_mem = k_ref.shape[1] * HEAD_DIM // D_MODEL
    for h in range(N_MEM_HEADS):
        sl = slice(h * MEM_HEAD_DIM, (h + 1) * MEM_HEAD_DIM)
        k = _lane_rows(k_ref, 0, n_mem, h, MEM_HEAD_DIM, D_MODEL, tiled=True).astype(BF)
        v = _lane_rows(v_ref, 0, n_mem, h, MEM_HEAD_DIM, D_MODEL, tiled=True).astype(BF)
        s = lax.dot_general(q_ref[:, sl], k, _NT, preferred_element_type=F32)
        p = jnp.exp2(s - jnp.max(s, axis=-1, keepdims=True))
        l = jnp.sum(p, axis=-1, keepdims=True)
        o = jnp.dot(p.astype(BF), v, preferred_element_type=F32) / l
        o_ref[:, sl] = o.astype(BF)


def _cross_attn(q_b, memk, memv, n_batch, t_b, tm):
    nt = t_b // tm
    mem_spec = pl.BlockSpec((1,) + memk.shape[1:], lambda b, i: (b, 0, 0))
    return pl.pallas_call(
        _cross_kernel,
        grid=(n_batch, nt),
        in_specs=[pl.BlockSpec((tm, D_MODEL), lambda b, i: (b * nt + i, 0)), mem_spec, mem_spec],
        out_specs=pl.BlockSpec((tm, D_MODEL), lambda b, i: (b * nt + i, 0)),
        out_shape=jax.ShapeDtypeStruct((n_batch * t_b, D_MODEL), BF),
        compiler_params=_cparams(2),
        name="cross_attn",
    )(q_b, memk, memv)


def _gelu_tanh(x):
    return 0.5 * x * (1.0 + jnp.tanh(math.sqrt(2.0 / math.pi) * (x + 0.044715 * (x * x * x))))


def _ffn_kernel(h_ref, wg_ref, wu_ref, wd_ref, wc_ref, bc_ref, st_ref, g_ref, b_ref,
                o_ref, last_ref, hb_ref, carry_ref, gs_ref, *, tm, fc, seq, n_ff):
    i = pl.program_id(0)
    c = pl.program_id(1)

    @pl.when(c == 0)
    def _():
        hb_ref[...] = h_ref[...].astype(BF)
        o_ref[...] = jnp.zeros(o_ref.shape, F32)

    @pl.when((i == 0) & (c == 0))
    def _():
        carry_ref[...] = jnp.zeros(carry_ref.shape, F32)

    hb = hb_ref[...]
    halves = [slice(k, k + FFN_COLS) for k in range(0, fc, FFN_COLS)]
    gs = [jnp.dot(hb, wg_ref[:, cols], preferred_element_type=F32) for cols in halves]
    us = [jnp.dot(hb, wu_ref[:, cols], preferred_element_type=F32) for cols in halves]
    row = lax.broadcasted_iota(jnp.int32, (tm, FFN_COLS), 0)
    acts = []
    for g, u, cols in zip(gs, us, halves):
        if seq >= tm:
            tiles_per_seq = seq // tm
            at_start = (i % tiles_per_seq) == 0
            prev = jnp.where(at_start, st_ref[0, :, cols], carry_ref[c, :, cols])
            pos = row
            ex0 = prev[0:1]
            ex1 = prev[1:2]
            carry_ref[c, :, cols] = g[tm - 2:tm, :]
            last_ref[0, :, cols] = g[tm - 2:tm, :]
        else:
            ns = tm // seq
            pos = jnp.bitwise_and(row, seq - 1)
            ex0 = jnp.broadcast_to(st_ref[0, :, cols][:, None, :], (ns, seq, FFN_COLS)).reshape(tm, FFN_COLS)
            ex1 = jnp.broadcast_to(st_ref[1, :, cols][:, None, :], (ns, seq, FFN_COLS)).reshape(tm, FFN_COLS)
            for k in range(FFN_COLS // 128):
                lanes = slice(cols.start + k * 128, cols.start + (k + 1) * 128)
                gs_ref[k] = g[:, k * 128:(k + 1) * 128]
                last_ref[0, :, lanes] = gs_ref[k, pl.ds(seq - 2, ns, stride=seq), :]
                last_ref[1, :, lanes] = gs_ref[k, pl.ds(seq - 1, ns, stride=seq), :]
        g_m1 = jnp.where(pos == 0, ex1, pltpu.roll(g, 1, 0))
        g_m2 = jnp.where(pos == 0, ex0, jnp.where(pos == 1, ex1, pltpu.roll(g, 2, 0)))
        conv = (bc_ref[:, cols] + g_m2 * wc_ref[0:1, cols] + g_m1 * wc_ref[1:2, cols] + g * wc_ref[2:3, cols])
        acts.append((_gelu_tanh(conv) * u).astype(BF))
    d = None
    for a, cols in zip(acts, halves):
        dd = jnp.dot(a, wd_ref[cols, :], preferred_element_type=F32)
        d = dd if d is None else d + dd
    o_ref[...] += d

    @pl.when(c == n_ff - 1)
    def _():
        o_ref[...] = _layer_norm(DN_ALPHA * h_ref[...] + o_ref[...], g_ref[...], b_ref[...])


def _ffn(h, w_gate_b, w_up_b, w_down_b, w_conv, b_conv, state, ln_g, ln_b, seq, tm, fc):
    t = h.shape[0]
    n_ff = D_FF // fc
    nt = t // tm
    if seq >= tm:
        tiles_per_seq = seq // tm
        st_spec = pl.BlockSpec((1, 2, fc), lambda i, c: (i // tiles_per_seq, 0, c))
        last_spec = pl.BlockSpec((1, 2, fc), lambda i, c: (i, 0, c))
        last_shape = jax.ShapeDtypeStruct((nt, 2, D_FF), F32)
    else:
        ns = tm // seq
        st_spec = pl.BlockSpec((2, ns, fc), lambda i, c: (0, i, c))
        last_spec = pl.BlockSpec((2, ns, fc), lambda i, c: (0, i, c))
        last_shape = jax.ShapeDtypeStruct((2, t // seq, D_FF), F32)
    kern = functools.partial(_ffn_kernel, tm=tm, fc=fc, seq=seq, n_ff=n_ff)
    return pl.pallas_call(
        kern,
        grid=(nt, n_ff),
        in_specs=[pl.BlockSpec((tm, D_MODEL), lambda i, c: (i, 0), pipeline_mode=pl.Buffered(1)),
                  pl.BlockSpec((D_MODEL, fc), lambda i, c: (0, c)),
                  pl.BlockSpec((D_MODEL, fc), lambda i, c: (0, c)),
                  pl.BlockSpec((fc, D_MODEL), lambda i, c: (c, 0)),
                  pl.BlockSpec((CONV_W, fc), lambda i, c: (0, c)),
                  pl.BlockSpec((1, fc), lambda i, c: (0, c)),
                  st_spec,
                  pl.BlockSpec((1, D_MODEL), lambda i, c: (0, 0)),
                  pl.BlockSpec((1, D_MODEL), lambda i, c: (0, 0))],
        out_specs=[pl.BlockSpec((tm, D_MODEL), lambda i, c: (i, 0)), last_spec],
        out_shape=[jax.ShapeDtypeStruct((t, D_MODEL), F32), last_shape],
        scratch_shapes=[pltpu.VMEM((tm, D_MODEL), BF), pltpu.VMEM((n_ff, 2, fc), F32),
                        pltpu.VMEM((fc // 128, tm, 128), F32)],
        compiler_params=_cparams(2),
        name="conv_ffn",
    )(h, w_gate_b, w_up_b, w_down_b, w_conv, b_conv.reshape(1, D_FF), state,
      ln_g.reshape(1, D_MODEL), ln_b.reshape(1, D_MODEL))


def _after_mixers(x, od, os_, memk, memv, conv_state, n_batch, t_b, w):
    h1, qc = _proj_ln([od, os_], w['w_o'], x, w['ln1_g'], w['ln1_b'], MID_ROWS, "out_proj_ln1_q", wq_b=w['w_mq'])
    oc = _cross_attn(qc, memk, memv, n_batch, t_b, min(t_b, MID_ROWS))
    (h2,) = _proj_ln([oc], w['w_mo'], h1, w['ln2_g'], w['ln2_b'], MID_ROWS, "cross_out_ln2")
    return _ffn(h2, w['w_gate'], w['w_up'], w['w_down'], w['w_conv'], w['b_conv'], conv_state,
                w['ln3_g'], w['ln3_b'], t_b, FFN_ROWS, FFN_CHUNK)


def kernel(x_prompt, x_sample, mem_prompt, cache_diff_k, cache_diff_v, cache_sb_k, cache_sb_v, cache_mem_k, cache_mem_v, state_ffn_conv, w_in, lambda_q1, lambda_k1, lambda_q2, lambda_k2, subln_g, w_o, ln1_g, ln1_b, w_mq, w_mk, w_mv, w_mo, ln2_g, ln2_b, w_gate, w_up, w_conv, b_conv, w_down, ln3_g, ln3_b):
    assert w_in.shape[0] == DEPTH == 1
    n_p, t_p, _ = x_prompt.shape
    n_s, t_s, _ = x_sample.shape
    past = cache_diff_k.shape[2]
    n_mem = mem_prompt.shape[1]
    l = 0
    lam_init = 0.8 - 0.6 * math.exp(-0.3 * l)
    lam = (jnp.exp(jnp.sum(lambda_q1[l] * lambda_k1[l])) - jnp.exp(jnp.sum(lambda_q2[l] * lambda_k2[l]))
           + lam_init).reshape(1).astype(F32)
    w = {'w_in': w_in[l].astype(BF), 'w_conv': w_conv[l], 'b_conv': b_conv[l],
         'subln_g': subln_g[l], 'ln1_g': ln1_g[l], 'ln1_b': ln1_b[l], 'ln2_g': ln2_g[l], 'ln2_b': ln2_b[l],
         'ln3_g': ln3_g[l], 'ln3_b': ln3_b[l]}
    late_weights = {'w_o': w_o[l], 'w_mq': w_mq[l], 'w_mo': w_mo[l], 'w_gate': w_gate[l], 'w_up': w_up[l],
                    'w_down': w_down[l]}
    tm = 512

    w_mkv = jnp.concatenate([w_mk[l], w_mv[l]], axis=1).astype(BF)
    mem_b = mem_prompt.reshape(n_p * n_mem, D_MODEL).astype(BF)
    (mkv_f,) = _matmul(mem_b, w_mkv, [F32], n_p * n_mem, 1024, "memory_kv")
    xp = x_prompt.reshape(n_p * t_p, D_MODEL)
    pb, pf = _inproj(xp, w['w_in'], np.arange(t_p), n_p, INPROJ_ROWS_PROMPT, sb_interleave=SB_BLOCK)
    odp, late_b = _diff_attn_prompt(lam, pb['dq'], pb['dk'], pb['dv'], w['subln_g'], n_p, t_p, lam_init,
                                    cast_weights=tuple(late_weights.values()))
    w.update(zip(late_weights, late_b))
    osp = _sb_attn_prompt(pb['sq'], pb['sk'], pb['sv'], n_p, t_p)
    zero_state = jnp.zeros((n_p, CONV_W - 1, D_FF), F32)
    mem_heads = (n_p, n_mem, N_MEM_HEADS, MEM_HEAD_DIM)
    yp, lastp = _after_mixers(xp, odp, osp, _tiled_lane_rows(mkv_f[:, :D_MODEL].reshape(mem_heads), n_p, N_MEM_HEADS),
                              _tiled_lane_rows(mkv_f[:, D_MODEL:].reshape(mem_heads), n_p, N_MEM_HEADS),
                              zero_state, n_p, t_p, w)

    xs = x_sample.reshape(n_s * t_s, D_MODEL)
    sb, sf = _inproj(xs, w['w_in'], past + np.arange(t_s), n_s, tm)
    rows = lambda c: _as_lane_rows(c[l], n_s)
    ods = _diff_attn_sample(lam, sb['dq'], sb['dk'], sb['dv'], rows(cache_diff_k),
                            _tiled_lane_rows(cache_diff_v[l], n_s, N_DIFF_HEADS), w['subln_g'],
                            n_s, t_s, past, lam_init)
    oss = _sb_attn_sample(sb['sq'], sb['sk'], sb['sv'], rows(cache_sb_k), rows(cache_sb_v), n_s, t_s, past)
    state_s = jnp.transpose(state_ffn_conv[l], (1, 0, 2))
    ys, lasts = _after_mixers(xs, ods, oss, _tiled_lane_rows(cache_mem_k[l], n_s, N_MEM_HEADS),
                              _tiled_lane_rows(cache_mem_v[l], n_s, N_MEM_HEADS), state_s, n_s, t_s, w)

    tiles_per_seq = t_p // FFN_ROWS
    new_conv_p = lastp[tiles_per_seq - 1::tiles_per_seq]
    new_conv_s = jnp.transpose(lasts, (1, 0, 2))

    def new_rows(f, n, t):
        dv = jnp.transpose(f['dv'].reshape(n, t, DIFF_V_DIM // HEAD_DIM, N_DIFF_HEADS, HEAD_DIM), (0, 1, 3, 2, 4))
        return (f['dk'].reshape(1, n, t, N_DIFF_HEADS, 2, HEAD_DIM), dv.reshape(1, n, t, N_DIFF_HEADS, DIFF_V_DIM),
                f['sk'].reshape(1, n, t, N_SB_HEADS, HEAD_DIM), f['sv'].reshape(1, n, t, N_SB_HEADS, HEAD_DIM))

    return ((yp.reshape(n_p, t_p, D_MODEL), ys.reshape(n_s, t_s, D_MODEL))
            + new_rows(pf, n_p, t_p)
            + (mkv_f[:, :D_MODEL].reshape((1,) + mem_heads), mkv_f[:, D_MODEL:].reshape((1,) + mem_heads),
               new_conv_p[None])
            + new_rows(sf, n_s, t_s)
            + (new_conv_s[None],))
```

```python
import functools
import math

import jax
import jax.numpy as jnp
import numpy as np
from jax import lax
from jax.experimental import pallas as pl
from jax.experimental.pallas import tpu as pltpu

BF = jnp.bfloat16
F32 = jnp.float32

D_MODEL = 2048
CHUNK = 64
CHUNK_SHIFT = 6
HEAD_DIM = 128
N_DIFF_HEADS = 4
N_SB_HEADS = 8
DIFF_V_DIM = 2 * HEAD_DIM
ROPE_DIM = HEAD_DIM // 4
ROPE_THETA = 500000.0
N_MEM_HEADS = 4
MEM_HEAD_DIM = D_MODEL // N_MEM_HEADS
D_FF = 5632
CONV_W = 3
LN_EPS = 1e-5
DEPTH = 1
DN_ALPHA = (2 * DEPTH) ** 0.25
SEG = 1024
N_SEG = 6
VMEM_LIMIT = 56 * 1024 * 1024
MIXERS_VMEM_LIMIT = 61 * 1024 * 1024
LOG2E = math.log2(math.e)
QK_SCALE2 = HEAD_DIM ** -0.5 * LOG2E
NEG_BIG = -1e30
LN_ROWS = 128
FFN_COLS = 256
FFN_ROWS = 1024
FFN_CHUNK = 512
INPROJ_ROWS_PROMPT = 1024
INPROJ_SLICES = 4
MID_ROWS = 512
SB_BLOCK = 256

_NT = (((1,), (1,)), ((), ()))


def _cparams(n_axes, vmem_limit=VMEM_LIMIT):
    return pltpu.CompilerParams(dimension_semantics=("arbitrary",) * n_axes,
                                vmem_limit_bytes=vmem_limit)


def _layer_norm(y, g, b):
    mu = jnp.mean(y, axis=-1, keepdims=True)
    yc = y - mu
    var = jnp.mean(yc * yc, axis=-1, keepdims=True)
    return yc * lax.rsqrt(var + LN_EPS) * g + b


SLOTS = SEG // HEAD_DIM


def _inproj_kernel(*refs, rope, want_f32, tile_rows, scale, emit_xb, interleave):
    x_ref, w_ref = refs[:2]
    tabs = refs[2:5] if rope else None
    outs = refs[5:] if rope else refs[2:]
    if interleave:
        outs, perm_ref = outs[:-1], outs[-1]
    pb_ref = outs[0]
    xb = x_ref[...].astype(BF)
    if emit_xb:
        outs[-1][...] = xb
    tm = xb.shape[0]
    hm = max(tm // INPROJ_SLICES, SB_BLOCK)
    starts = range(0, tm, hm)
    accs = [jnp.dot(xb[r0:r0 + hm, :], w_ref[...], preferred_element_type=F32) for r0 in starts]
    for r0, acc in zip(starts, accs):
        for ch in range(SLOTS):
            lanes = slice(ch * HEAD_DIM, (ch + 1) * HEAD_DIM)
            r = acc[:, lanes]
            if rope:
                c_ref, s1_ref, s2_ref = tabs
                r = (r * c_ref[r0:r0 + hm, :] + pltpu.roll(r, HEAD_DIM - ROPE_DIM // 2, 1) * s1_ref[r0:r0 + hm, :]
                     + pltpu.roll(r, ROPE_DIM // 2, 1) * s2_ref[r0:r0 + hm, :])
            elif scale != 1.0:
                r = r * scale
            if interleave:
                run = interleave // SUBLANES
                for b0 in range(0, hm, interleave):
                    for s in range(SUBLANES):
                        perm_ref[ch, pl.ds(r0 + b0 + s, run, stride=SUBLANES), :] = r[b0 + s * run:b0 + (s + 1) * run, :]
                pb_ref[0, r0:r0 + hm, lanes] = perm_ref[ch, r0:r0 + hm, :].astype(BF)
            else:
                pb_ref[0, r0:r0 + hm, lanes] = r.astype(BF)
            if want_f32:
                head, col_tile = divmod(ch, SLOTS // tile_rows) if tile_rows else (0, 0)
                off = col_tile * tile_rows + head if tile_rows else ch
                outs[1][pl.ds(r0 * SLOTS + off, hm, stride=SLOTS), :] = r


def _rope_tables(pos, tile_rows, scale):
    half = ROPE_DIM // 2
    n = pos.shape[0]
    assert n % tile_rows == 0 or tile_rows % n == 0
    inv_freq = np.power(ROPE_THETA, -np.arange(half, dtype=np.float64) * (2.0 / ROPE_DIM))
    ang = pos.astype(np.float64)[:, None] * inv_freq[None, :]
    cos, sin = np.cos(ang) * scale, np.sin(ang) * scale
    c = np.concatenate([cos, cos, np.full((n, HEAD_DIM - ROPE_DIM), scale)], axis=1)
    s1 = np.concatenate([-sin, np.zeros((n, HEAD_DIM - half))], axis=1)
    s2 = np.concatenate([np.zeros((n, half)), sin, np.zeros((n, HEAD_DIM - ROPE_DIM))], axis=1)
    reps = max(1, tile_rows // n)
    return tuple(jnp.asarray(np.tile(tb, (reps, 1)), F32) for tb in (c, s1, s2))


IN_GROUPS = (("dq", True, False, 0, QK_SCALE2), ("dk", True, True, 0, 1.0), ("dv", False, True, N_DIFF_HEADS, 1.0),
             ("sq", False, False, 0, QK_SCALE2), ("sk", False, True, 0, 1.0), ("sv", False, True, 0, 1.0))


def _inproj(x, w_in_b, pos, n_seq, tm, sb_interleave=0):
    t = x.shape[0]
    assert t == n_seq * pos.shape[0]
    row = lambda i: (i, 0)
    tab_blocks = max(1, pos.shape[0] // tm)
    tab = pl.BlockSpec((tm, HEAD_DIM), lambda i: (i % tab_blocks, 0))
    bf, f32 = {}, {}
    xb = None
    for j, (name, rope, want_f32, tile_rows, scale) in enumerate(IN_GROUPS):
        assert not (want_f32 and scale != 1.0)
        emit_xb = xb is None
        interleave = sb_interleave if name in ("sk", "sv") else 0
        outs = pl.pallas_call(
            functools.partial(_inproj_kernel, rope=rope, want_f32=want_f32, tile_rows=tile_rows, scale=scale,
                              emit_xb=emit_xb, interleave=interleave),
            scratch_shapes=[pltpu.VMEM((SLOTS, tm, HEAD_DIM), F32)] if interleave else [],
            grid=(t // tm,),
            in_specs=[pl.BlockSpec((tm, D_MODEL), row),
                      pl.BlockSpec((D_MODEL, SEG), functools.partial(lambda i, j: (0, j), j=j))]
                     + ([tab, tab, tab] if rope else []),
            out_specs=[pl.BlockSpec((1, tm, SEG), lambda i: (0, i, 0))]
                      + ([pl.BlockSpec((tm * SLOTS, HEAD_DIM), row)] if want_f32 else [])
                      + ([pl.BlockSpec((tm, D_MODEL), row)] if emit_xb else []),
            out_shape=[jax.ShapeDtypeStruct((1, t, SEG), BF)]
                      + ([jax.ShapeDtypeStruct((t * SLOTS, HEAD_DIM), F32)] if want_f32 else [])
                      + ([jax.ShapeDtypeStruct((t, D_MODEL), BF)] if emit_xb else []),
            compiler_params=_cparams(1),
            name="inproj_" + name,
        )(x if emit_xb else xb, w_in_b, *(_rope_tables(pos, tm, scale) if rope else ()))
        bf[name] = outs[0]
        if want_f32:
            f32[name] = outs[1]
        if emit_xb:
            xb = outs[-1]
    return bf, f32


def _matmul_kernel(x_ref, w_ref, *o_refs):
    acc = jnp.dot(x_ref[...], w_ref[...].astype(BF), preferred_element_type=F32)
    for o in o_refs:
        o[...] = acc.astype(o.dtype)


def _matmul(x_b, w_b, out_dtypes, tm, tn, name):
    t, k = x_b.shape
    n = w_b.shape[1]
    return pl.pallas_call(
        _matmul_kernel,
        grid=(t // tm, n // tn),
        in_specs=[pl.BlockSpec((tm, k), lambda i, j: (i, 0)),
                  pl.BlockSpec((k, tn), lambda i, j: (0, j))],
        out_specs=[pl.BlockSpec((tm, tn), lambda i, j: (i, j)) for _ in out_dtypes],
        out_shape=[jax.ShapeDtypeStruct((t, n), d) for d in out_dtypes],
        compiler_params=_cparams(2),
        name=name,
    )(x_b, w_b)


def _chunk_visible(q0, k0, nq, nk):
    qpos = q0 + lax.broadcasted_iota(jnp.int32, (nq, nk), 0)
    kpos = k0 + lax.broadcasted_iota(jnp.int32, (nq, nk), 1)
    return lax.shift_right_logical(kpos, CHUNK_SHIFT) <= lax.shift_right_logical(qpos, CHUNK_SHIFT)


def _subln(o0, o1, lam, g, lam_init):
    of = o0 - lam * o1
    return of * lax.rsqrt(jnp.mean(of * of, axis=-1, keepdims=True) + LN_EPS) * g * (1.0 - lam_init)


def _diff_sample_kernel(lam_ref, q_ref, kn_ref, vn_ref, kc_ref, vc_ref, g_ref, o_ref, *, t_s, past, lam_init):
    vis = _chunk_visible(past, past, t_s, t_s)
    for h in range(N_DIFF_HEADS):
        vs = slice(h * DIFF_V_DIM, (h + 1) * DIFF_V_DIM)
        v_old = _lane_rows(vc_ref, 0, past, h, DIFF_V_DIM, SEG, tiled=True).astype(BF)
        v_new = vn_ref[0, :, vs]
        o = []
        for c in range(2):
            cs = slice((2 * h + c) * HEAD_DIM, (2 * h + c + 1) * HEAD_DIM)
            q = q_ref[0, :, cs]
            k_old = _lane_rows(kc_ref, 0, past, 2 * h + c, HEAD_DIM, SEG).astype(BF)
            s_old = lax.dot_general(q, k_old, _NT, preferred_element_type=F32)
            s_new = lax.dot_general(q, kn_ref[0, :, cs], _NT, preferred_element_type=F32)
            s_new = jnp.where(vis, s_new, -jnp.inf)
            m = jnp.maximum(jnp.max(s_old, axis=-1, keepdims=True), jnp.max(s_new, axis=-1, keepdims=True))
            p_old = jnp.exp2(s_old - m)
            p_new = jnp.exp2(s_new - m)
            l = jnp.sum(p_old, axis=-1, keepdims=True) + jnp.sum(p_new, axis=-1, keepdims=True)
            acc = (jnp.dot(p_old.astype(BF), v_old, preferred_element_type=F32)
                   + jnp.dot(p_new.astype(BF), v_new, preferred_element_type=F32))
            o.append(acc / l)
        o_ref[:, vs] = _subln(o[0], o[1], lam_ref[0], g_ref[...], lam_init).astype(BF)


def _new_rows_spec(t_s):
    return pl.BlockSpec((1, t_s, SEG), lambda b: (0, b, 0))


def _as_lane_rows(x, n_batch):
    return x.reshape(n_batch, -1, HEAD_DIM)


def _lane_rows(ref, t0, nt, slot, slot_width, width, tiled=False):
    per_row = width // HEAD_DIM
    per_slot = slot_width // HEAD_DIM
    n_slots = width // slot_width
    offs = [(k * n_slots + slot) if tiled else (slot * per_slot + k) for k in range(per_slot)]
    parts = [ref[0, pl.ds(t0 * per_row + o, nt, stride=per_row), :] for o in offs]
    return parts[0] if per_slot == 1 else jnp.concatenate(parts, axis=1)


def _tiled_lane_rows(x, n_batch, n_slots):
    rows, w = x.shape[1], x.shape[3]
    x = x.reshape(n_batch, rows, n_slots, w // HEAD_DIM, HEAD_DIM)
    return jnp.transpose(x, (0, 1, 3, 2, 4)).reshape(n_batch, -1, HEAD_DIM)


def _cache_spec(cache):
    return pl.BlockSpec((1,) + cache.shape[1:], lambda b: (b, 0, 0))


def _diff_attn_sample(lam, q, k_new, v_new, cache_k, cache_v, subln_g, n_batch, t_s, past, lam_init):
    kern = functools.partial(_diff_sample_kernel, t_s=t_s, past=past, lam_init=lam_init)
    return pl.pallas_call(
        kern,
        grid=(n_batch,),
        in_specs=[pl.BlockSpec(memory_space=pltpu.SMEM), _new_rows_spec(t_s), _new_rows_spec(t_s),
                  _new_rows_spec(t_s), _cache_spec(cache_k), _cache_spec(cache_v),
                  pl.BlockSpec((1, DIFF_V_DIM), lambda b: (0, 0))],
        out_specs=pl.BlockSpec((t_s, SEG), lambda b: (b, 0)),
        out_shape=jax.ShapeDtypeStruct((n_batch * t_s, SEG), BF),
        compiler_params=_cparams(1),
        name="diff_attn_sample",
    )(lam, q, k_new, v_new, cache_k, cache_v, subln_g.reshape(1, DIFF_V_DIM))


def _sb_neg_log_keep(z2):
    neg_abs = jnp.minimum(z2, -z2)
    return jnp.maximum(z2, 0.0) + jnp.log(1.0 + jnp.exp2(neg_abs)) * LOG2E


def _split_bf16(x):
    hi = x.astype(BF)
    return hi, (x - hi.astype(F32)).astype(BF)


SUBLANES = 8


def _interleaved_key(n):
    r = lax.broadcasted_iota(jnp.int32, (n, 1), 0)
    return jnp.bitwise_and(r, SUBLANES - 1) * (n // SUBLANES) + lax.shift_right_logical(r, 3)


def _suffix_interleaved(x, carry):
    nj = x.shape[0] // SUBLANES
    run = [None] * nj
    acc = None
    for j in reversed(range(nj)):
        part = x[SUBLANES * j:SUBLANES * (j + 1), :]
        acc = part if acc is None else acc + part
        run[j] = acc
    g = run[0]
    row = lax.broadcasted_iota(jnp.int32, g.shape, 0)
    incl = g
    for d in (1, 2, 4):
        incl = incl + jnp.where(row < SUBLANES - d, pltpu.roll(incl, SUBLANES - d, 0), 0.0)
    later = incl - g + carry
    return jnp.concatenate([r + later for r in run], axis=0), incl[0:1, :]


def _drive(*stage_gens):
    live = list(stage_gens)
    while live:
        for gen in list(live):
            try:
                next(gen)
            except StopIteration:
                live.remove(gen)


def _sb_group(items, carry, acc, key_axis):
    _drive(_sb_stages(items, carry, acc, key_axis))


def _sb_stages(items, carry, acc, key_axis):
    if key_axis == 0:
        zs = [lax.dot_general(k, q, _NT, preferred_element_type=F32) for q, k, _, _, _, _ in items]
    else:
        zs = [lax.dot_general(q, k, _NT, preferred_element_type=F32) for q, k, _, _, _, _ in items]
    yield
    splits = []
    for z, (_, _, _, tri, before, _) in zip(zs, items):
        nlk = _sb_neg_log_keep(z)
        if before is not None:
            nlk = jnp.where(before, nlk, 0.0)
        if tri is None:
            splits.append(nlk)
            continue
        hi, lo = _split_bf16(nlk)
        fused = tri.shape[1 - key_axis] == 2 * tri.shape[key_axis]
        splits.append(jnp.concatenate([hi, lo], axis=key_axis) if fused else (hi, lo))
    yield
    sufs = []
    for hl, (_, _, _, tri, _, ch) in zip(splits, items):
        if tri is None:
            suf, total = _suffix_interleaved(hl, carry[ch])
        else:
            if key_axis == 0:
                mm = lambda x: jnp.dot(tri, x, preferred_element_type=F32)
            else:
                mm = lambda x: jnp.dot(x, tri, preferred_element_type=F32)
            suf = mm(hl[0]) + mm(hl[1]) if isinstance(hl, tuple) else mm(hl)
            total = suf[0:1, :] if key_axis == 0 else suf[:, 0:1]
            suf = suf + carry[ch]
        sufs.append(suf)
        carry[ch] = carry[ch] + total
    yield
    probs = []
    for z, suf, (_, _, _, _, before, ch) in zip(zs, sufs, items):
        a = jnp.exp2(z - suf)
        if before is not None:
            a = jnp.where(before, a, 0.0)
        probs.append(a.astype(BF))
    yield
    for a, (_, _, v_op, _, _, ch) in zip(probs, items):
        if key_axis == 0:
            acc[ch] = acc[ch] + jnp.dot(v_op, a, preferred_element_type=F32)
        else:
            acc[ch] = acc[ch] + jnp.dot(a, v_op, preferred_element_type=F32)


def _tri(n, upper):
    r = lax.broadcasted_iota(jnp.int32, (n, n), 0)
    c = lax.broadcasted_iota(jnp.int32, (n, n), 1)
    return jnp.where((c >= r) if upper else (r >= c), 1.0, 0.0).astype(BF)


def _mixers_prompt_kernel(*refs, tq, sub, gk, t, lam_init, n_cast):
    lam_ref, dq_ref, dk_ref, dv_ref, g_ref, sq_ref, sk_ref, sv_ref = refs[:8]
    od_ref, os_ref = refs[8 + n_cast:10 + n_cast]
    vtd_ref, m_ref, l_ref, accd_ref, vts_ref, carry_ref, accs_ref = refs[10 + 2 * n_cast:]
    for w_ref, wb_ref in zip(refs[8:8 + n_cast], refs[10 + n_cast:10 + 2 * n_cast]):
        wb_ref[...] = w_ref[...].astype(BF)
    qi = pl.program_id(2)
    n_sub = tq // sub
    per_g = gk // sub
    sb_heads = N_SB_HEADS // N_DIFF_HEADS
    hcols = [slice(hh * HEAD_DIM, (hh + 1) * HEAD_DIM) for hh in range(sb_heads)]

    @pl.when(qi == 0)
    def _():
        for kb in range(t // sub):
            rows = slice(kb * sub, (kb + 1) * sub)
            g0, j0 = divmod(kb * sub, gk)
            vtd_ref[g0, :, j0:j0 + sub] = dv_ref[0, rows, :].astype(F32).T.astype(BF)
            for hh in range(sb_heads):
                vts_ref[hh, kb] = sv_ref[0, rows, hcols[hh]].astype(F32).T.astype(BF)

    m_ref[...] = jnp.full(m_ref.shape, NEG_BIG, F32)
    l_ref[...] = jnp.zeros(l_ref.shape, F32)
    accd_ref[...] = jnp.zeros(accd_ref.shape, F32)

    def diff_stages(k0, vt, items):
        scores = []
        for c, r, nk, vis in items:
            cs = slice(c * HEAD_DIM, (c + 1) * HEAD_DIM)
            s = lax.dot_general(dk_ref[0, pl.ds(k0, nk), cs], dq_ref[0, r * sub:(r + 1) * sub, cs], _NT,
                                preferred_element_type=F32)
            if vis is not None:
                tail = jnp.where(vis, s[nk - sub:, :], -jnp.inf)
                s = tail if nk == sub else jnp.concatenate([s[:nk - sub, :], tail], axis=0)
            scores.append(s)
        yield
        probs = []
        for s, (c, r, nk, _) in zip(scores, items):
            i = c * n_sub + r
            m_old = m_ref[i]
            m_new = jnp.maximum(m_old, jnp.max(s, axis=0, keepdims=True))
            alpha = jnp.exp2(m_old - m_new)
            p = jnp.exp2(s - m_new)
            l_ref[i] = alpha * l_ref[i] + jnp.sum(p, axis=0, keepdims=True)
            m_ref[i] = m_new
            probs.append((alpha, p.astype(BF)))
        yield
        for (alpha, p), (c, r, nk, _) in zip(probs, items):
            i = c * n_sub + r
            accd_ref[i] = alpha * accd_ref[i] + jnp.dot(vt[:, :nk], p, preferred_element_type=F32)

    def sb_items(kbs, masked_slice):
        items = []
        for kb in kbs:
            k_rows = pl.ds(pl.multiple_of(kb * sub, sub), sub)
            for hh in range(sb_heads):
                kblk = sk_ref[0, k_rows, hcols[hh]]
                first = 0 if masked_slice is None else masked_slice(kb)
                for c in range(first, n_sub):
                    mask = before if (masked_slice is not None and c == first) else None
                    items.append((sqs[hh][c], kblk, vts_ref[hh, kb], None, mask, hh * n_sub + c))
        return items

    sqs = [[sq_ref[0, c * sub:(c + 1) * sub, hcols[hh]] for c in range(n_sub)] for hh in range(sb_heads)]
    before = _interleaved_key(sub) < lax.broadcasted_iota(jnp.int32, (sub, sub), 1)
    k_i = lax.broadcasted_iota(jnp.int32, (sub, sub), 0)
    q_i = lax.broadcasted_iota(jnp.int32, (sub, sub), 1)
    vis = lax.shift_right_logical(k_i, CHUNK_SHIFT) <= lax.shift_right_logical(q_i, CHUNK_SHIFT)

    base = qi * n_sub
    k0 = pl.multiple_of(qi * tq, tq)
    g0 = qi * (tq // gk)
    vt_diag = jnp.concatenate([vtd_ref[g0 + j] for j in range(tq // gk)], axis=1)
    carry = [jnp.zeros((1, sub), F32) for _ in range(sb_heads * n_sub)]
    accs = [jnp.zeros((HEAD_DIM, sub), F32) for _ in range(sb_heads * n_sub)]
    diag_sb = []
    for j in reversed(range(n_sub)):
        diag_sb += sb_items([base + j], lambda kb, j=j: j)
    _drive(diff_stages(k0, vt_diag, [(c, r, (r + 1) * sub, vis) for r in range(n_sub) for c in range(2)]),
           _sb_stages(diag_sb, carry, accs, 0))
    for ch in range(sb_heads * n_sub):
        carry_ref[ch] = carry[ch]
        accs_ref[ch] = accs[ch]

    full_items = [(c, r, gk, None) for r in range(n_sub) for c in range(2)]

    def body(i, x):
        g = g0 - 1 - i
        cr = [carry_ref[ch] for ch in range(sb_heads * n_sub)]
        ac = [accs_ref[ch] for ch in range(sb_heads * n_sub)]
        kbs = [g * per_g + j for j in reversed(range(per_g))]
        _drive(diff_stages(pl.multiple_of(g * gk, gk), vtd_ref[g], full_items),
               _sb_stages(sb_items(kbs, None), cr, ac, 0))
        for ch in range(sb_heads * n_sub):
            carry_ref[ch] = cr[ch]
            accs_ref[ch] = ac[ch]
        return x

    lax.fori_loop(0, g0, body, 0)

    for r in range(n_sub):
        rows = slice(r * sub, (r + 1) * sub)
        o0 = (accd_ref[r] / l_ref[r]).T
        o1 = (accd_ref[n_sub + r] / l_ref[n_sub + r]).T
        od_ref[rows, :] = _subln(o0, o1, lam_ref[0], g_ref[...], lam_init).astype(BF)
        for hh in range(sb_heads):
            os_ref[rows, hcols[hh]] = accs_ref[hh * n_sub + r].T.astype(BF)


def _mixers_prompt(lam, pb, subln_g, n_batch, t, lam_init, cast_weights=(), tq=1024, sub=SB_BLOCK, gk=512):
    assert t % tq == 0 and tq % gk == 0 and gk % sub == 0 and sub % CHUNK == 0
    nq = t // tq
    n_steps = n_batch * N_DIFF_HEADS * nq
    sb_heads = N_SB_HEADS // N_DIFF_HEADS
    step = lambda b, h, i: ((b * N_DIFF_HEADS + h) * nq + i, 0)
    cast_specs = []
    for wt in cast_weights:
        assert wt.shape[0] % (16 * n_steps) == 0
        cast_specs.append(pl.BlockSpec((wt.shape[0] // n_steps, wt.shape[1]), step))
    q_spec = pl.BlockSpec((1, tq, DIFF_V_DIM), lambda b, h, i: (0, b * nq + i, h))
    kv_spec = pl.BlockSpec((1, t, DIFF_V_DIM), lambda b, h, i: (0, b, h), pipeline_mode=pl.Buffered(1))
    out_spec = pl.BlockSpec((tq, DIFF_V_DIM), lambda b, h, i: (b * nq + i, h))
    chains = sb_heads * (tq // sub)
    kern = functools.partial(_mixers_prompt_kernel, tq=tq, sub=sub, gk=gk, t=t, lam_init=lam_init,
                             n_cast=len(cast_weights))
    outs = pl.pallas_call(
        kern,
        grid=(n_batch, N_DIFF_HEADS, nq),
        in_specs=[pl.BlockSpec(memory_space=pltpu.SMEM), q_spec, kv_spec, kv_spec,
                  pl.BlockSpec((1, DIFF_V_DIM), lambda b, h, i: (0, 0)), q_spec, kv_spec, kv_spec] + cast_specs,
        out_specs=[out_spec, out_spec] + cast_specs,
        out_shape=[jax.ShapeDtypeStruct((n_batch * t, SEG), BF)] * 2
                  + [jax.ShapeDtypeStruct(wt.shape, BF) for wt in cast_weights],
        scratch_shapes=[pltpu.VMEM((t // gk, DIFF_V_DIM, gk), BF),
                        pltpu.VMEM((2 * tq // sub, 1, sub), F32), pltpu.VMEM((2 * tq // sub, 1, sub), F32),
                        pltpu.VMEM((2 * tq // sub, DIFF_V_DIM, sub), F32),
                        pltpu.VMEM((sb_heads, t // sub, HEAD_DIM, sub), BF),
                        pltpu.VMEM((chains, 1, sub), F32), pltpu.VMEM((chains, HEAD_DIM, sub), F32)],
        compiler_params=_cparams(3, MIXERS_VMEM_LIMIT),
        name="mixers_prompt",
    )(lam, pb['dq'], pb['dk'], pb['dv'], subln_g.reshape(1, DIFF_V_DIM), pb['sq'], pb['sk'], pb['sv'],
      *cast_weights)
    return outs[0], outs[1], outs[2:]


def _sb_sample_kernel(q_ref, kn_ref, vn_ref, kc_ref, vc_ref, o_ref, *, t_s, past, tk):
    tri_new = _tri(t_s, False)
    l_old = _tri(tk, False)
    tri_old = jnp.concatenate([l_old, l_old], axis=0)
    before = (lax.broadcasted_iota(jnp.int32, (t_s, t_s), 1) < lax.broadcasted_iota(jnp.int32, (t_s, t_s), 0))
    heads = range(N_SB_HEADS)
    cols = [slice(h * HEAD_DIM, (h + 1) * HEAD_DIM) for h in heads]
    qs = [q_ref[0, :, cs] for cs in cols]
    carry = [jnp.zeros((t_s, 1), F32) for _ in heads]
    acc = [jnp.zeros((t_s, HEAD_DIM), F32) for _ in heads]
    _sb_group([(qs[h], kn_ref[0, :, cols[h]], vn_ref[0, :, cols[h]], tri_new, before, h) for h in heads],
              carry, acc, 1)
    for kb in reversed(range(past // tk)):
        rows = slice(kb * tk, (kb + 1) * tk)
        _sb_group([(qs[h], _lane_rows(kc_ref, kb * tk, tk, h, HEAD_DIM, SEG).astype(BF),
                    _lane_rows(vc_ref, kb * tk, tk, h, HEAD_DIM, SEG).astype(BF), tri_old, None, h)
                   for h in heads], carry, acc, 1)
    for h in heads:
        o_ref[:, cols[h]] = acc[h].astype(BF)


def _sb_attn_sample(q, k_new, v_new, cache_k, cache_v, n_batch, t_s, past, tk=256):
    assert past % tk == 0
    kern = functools.partial(_sb_sample_kernel, t_s=t_s, past=past, tk=tk)
    return pl.pallas_call(
        kern,
        grid=(n_batch,),
        in_specs=[_new_rows_spec(t_s), _new_rows_spec(t_s), _new_rows_spec(t_s),
                  _cache_spec(cache_k), _cache_spec(cache_v)],
        out_specs=pl.BlockSpec((t_s, SEG), lambda b: (b, 0)),
        out_shape=jax.ShapeDtypeStruct((n_batch * t_s, SEG), BF),
        compiler_params=_cparams(1),
        name="sb_attn_sample",
    )(q, k_new, v_new, cache_k, cache_v)


MEM_SCALE2 = MEM_HEAD_DIM ** -0.5 * LOG2E


def _proj_ln_kernel(*refs, n_in, with_q):
    a_refs = refs[:n_in]
    w_refs = refs[n_in:2 * n_in]
    res_ref, g_ref, b_ref = refs[2 * n_in:2 * n_in + 3]
    rest = refs[2 * n_in + 3:]
    tm = res_ref.shape[0]
    slices = [slice(r, r + LN_ROWS) for r in range(0, tm, LN_ROWS)]
    accs = []
    for rows in slices:
        acc = None
        for a, w in zip(a_refs, w_refs):
            d = jnp.dot(a[rows, :], w[...], preferred_element_type=F32)
            acc = d if acc is None else acc + d
        accs.append(acc)
    outs = [_layer_norm(DN_ALPHA * res_ref[rows, :] + acc, g_ref[...], b_ref[...]) for rows, acc in zip(slices, accs)]
    if with_q:
        wq_ref, of_ref, oq_ref = rest
        for rows, out in zip(slices, outs):
            of_ref[rows, :] = out
        for rows, out in zip(slices, outs):
            q = jnp.dot(out.astype(BF), wq_ref[...], preferred_element_type=F32)
            oq_ref[rows, :] = (q * MEM_SCALE2).astype(BF)
    else:
        (of_ref,) = rest
        for rows, out in zip(slices, outs):
            of_ref[rows, :] = out


def _proj_ln(a_list, w_b, res, g, b, tm, name, wq_b=None):
    t = res.shape[0]
    n_in = len(a_list)
    ka = a_list[0].shape[1]
    row = lambda i: (i, 0)
    once = dict(pipeline_mode=pl.Buffered(1))
    full = pl.BlockSpec((tm, D_MODEL), row)
    vec = pl.BlockSpec((1, D_MODEL), lambda i: (0, 0))
    in_specs = ([pl.BlockSpec((tm, ka), row) for _ in a_list]
                + [pl.BlockSpec((ka, D_MODEL), functools.partial(lambda i, r: (r, 0), r=r), **once) for r in range(n_in)]
                + [full, vec, vec])
    args = [*a_list, *([w_b] * n_in), res, g.reshape(1, D_MODEL), b.reshape(1, D_MODEL)]
    out_specs = [full]
    out_shape = [jax.ShapeDtypeStruct((t, D_MODEL), F32)]
    if wq_b is not None:
        in_specs.append(pl.BlockSpec((D_MODEL, D_MODEL), lambda i: (0, 0), **once))
        args.append(wq_b)
        out_specs.append(full)
        out_shape.append(jax.ShapeDtypeStruct((t, D_MODEL), BF))
    return pl.pallas_call(
        functools.partial(_proj_ln_kernel, n_in=n_in, with_q=wq_b is not None),
        grid=(t // tm,),
        in_specs=in_specs,
        out_specs=out_specs,
        out_shape=out_shape,
        compiler_params=_cparams(1),
        name=name,
    )(*args)


def _cross_kernel(q_ref, k_ref, v_ref, o_ref):
    n_mem = k_ref.shape[1] * HEAD_DIM // D_MODEL
    for h in range(N_MEM_HEADS):
        sl = slice(h * MEM_HEAD_DIM, (h + 1) * MEM_HEAD_DIM)
        k = _lane_rows(k_ref, 0, n_mem, h, MEM_HEAD_DIM, D_MODEL, tiled=True).astype(BF)
        v = _lane_rows(v_ref, 0, n_mem, h, MEM_HEAD_DIM, D_MODEL, tiled=True).astype(BF)
        s = lax.dot_general(q_ref[:, sl], k, _NT, preferred_element_type=F32)
        p = jnp.exp2(s - jnp.max(s, axis=-1, keepdims=True))
        l = jnp.sum(p, axis=-1, keepdims=True)
        o = jnp.dot(p.astype(BF), v, preferred_element_type=F32) / l
        o_ref[:, sl] = o.astype(BF)


def _cross_attn(q_b, memk, memv, n_batch, t_b, tm):
    nt = t_b // tm
    mem_spec = pl.BlockSpec((1,) + memk.shape[1:], lambda b, i: (b, 0, 0))
    return pl.pallas_call(
        _cross_kernel,
        grid=(n_batch, nt),
        in_specs=[pl.BlockSpec((tm, D_MODEL), lambda b, i: (b * nt + i, 0)), mem_spec, mem_spec],
        out_specs=pl.BlockSpec((tm, D_MODEL), lambda b, i: (b * nt + i, 0)),
        out_shape=jax.ShapeDtypeStruct((n_batch * t_b, D_MODEL), BF),
        compiler_params=_cparams(2),
        name="cross_attn",
    )(q_b, memk, memv)


def _gelu_tanh(x):
    return 0.5 * x * (1.0 + jnp.tanh(math.sqrt(2.0 / math.pi) * (x + 0.044715 * (x * x * x))))


def _ffn_kernel(h_ref, wg_ref, wu_ref, wd_ref, wc_ref, bc_ref, st_ref, g_ref, b_ref,
                o_ref, last_ref, hb_ref, carry_ref, gs_ref, *, tm, fc, seq, n_ff):
    i = pl.program_id(0)
    c = pl.program_id(1)

    @pl.when(c == 0)
    def _():
        h = h_ref[...]
        hb_ref[...] = h.astype(BF)
        o_ref[...] = DN_ALPHA * h

    @pl.when((i == 0) & (c == 0))
    def _():
        carry_ref[...] = jnp.zeros(carry_ref.shape, F32)

    hb = hb_ref[...]
    halves = [slice(k, k + FFN_COLS) for k in range(0, fc, FFN_COLS)]
    gs = [jnp.dot(hb, wg_ref[:, cols], preferred_element_type=F32) for cols in halves]
    us = [jnp.dot(hb, wu_ref[:, cols], preferred_element_type=F32) for cols in halves]
    row = lax.broadcasted_iota(jnp.int32, (tm, FFN_COLS), 0)
    acts = []
    for g, u, cols in zip(gs, us, halves):
        if seq >= tm:
            tiles_per_seq = seq // tm
            at_start = (i % tiles_per_seq) == 0
            prev = jnp.where(at_start, st_ref[0, :, cols], carry_ref[c, :, cols])
            pos = row
            ex0 = prev[0:1]
            ex1 = prev[1:2]
            carry_ref[c, :, cols] = g[tm - 2:tm, :]
            last_ref[0, :, cols] = g[tm - 2:tm, :]
        else:
            ns = tm // seq
            pos = jnp.bitwise_and(row, seq - 1)
            ex0 = jnp.broadcast_to(st_ref[0, :, cols][:, None, :], (ns, seq, FFN_COLS)).reshape(tm, FFN_COLS)
            ex1 = jnp.broadcast_to(st_ref[1, :, cols][:, None, :], (ns, seq, FFN_COLS)).reshape(tm, FFN_COLS)
            for k in range(FFN_COLS // 128):
                lanes = slice(cols.start + k * 128, cols.start + (k + 1) * 128)
                gs_ref[k] = g[:, k * 128:(k + 1) * 128]
                last_ref[0, :, lanes] = gs_ref[k, pl.ds(seq - 2, ns, stride=seq), :]
                last_ref[1, :, lanes] = gs_ref[k, pl.ds(seq - 1, ns, stride=seq), :]
        g_m1 = jnp.where(pos == 0, ex1, pltpu.roll(g, 1, 0))
        g_m2 = jnp.where(pos == 0, ex0, jnp.where(pos == 1, ex1, pltpu.roll(g, 2, 0)))
        conv = (bc_ref[:, cols] + g_m2 * wc_ref[0:1, cols] + g_m1 * wc_ref[1:2, cols] + g * wc_ref[2:3, cols])
        acts.append((_gelu_tanh(conv) * u).astype(BF))
    d = None
    for a, cols in zip(acts, halves):
        dd = jnp.dot(a, wd_ref[cols, :], preferred_element_type=F32)
        d = dd if d is None else d + dd
    o_ref[...] += d

    @pl.when(c == n_ff - 1)
    def _():
        o_ref[...] = _layer_norm(o_ref[...], g_ref[...], b_ref[...])


def _ffn(h, w_gate_b, w_up_b, w_down_b, w_conv, b_conv, state, ln_g, ln_b, seq, tm, fc):
    t = h.shape[0]
    n_ff = D_FF // fc
    nt = t // tm
    if seq >= tm:
        tiles_per_seq = seq // tm
        st_spec = pl.BlockSpec((1, 2, fc), lambda i, c: (i // tiles_per_seq, 0, c))
        last_spec = pl.BlockSpec((1, 2, fc), lambda i, c: (i, 0, c))
        last_shape = jax.ShapeDtypeStruct((nt, 2, D_FF), F32)
    else:
        ns = tm // seq
        st_spec = pl.BlockSpec((2, ns, fc), lambda i, c: (0, i, c))
        last_spec = pl.BlockSpec((2, ns, fc), lambda i, c: (0, i, c))
        last_shape = jax.ShapeDtypeStruct((2, t // seq, D_FF), F32)
    kern = functools.partial(_ffn_kernel, tm=tm, fc=fc, seq=seq, n_ff=n_ff)
    return pl.pallas_call(
        kern,
        grid=(nt, n_ff),
        in_specs=[pl.BlockSpec((tm, D_MODEL), lambda i, c: (i, 0), pipeline_mode=pl.Buffered(1)),
                  pl.BlockSpec((D_MODEL, fc), lambda i, c: (0, c)),
                  pl.BlockSpec((D_MODEL, fc), lambda i, c: (0, c)),
                  pl.BlockSpec((fc, D_MODEL), lambda i, c: (c, 0)),
                  pl.BlockSpec((CONV_W, fc), lambda i, c: (0, c)),
                  pl.BlockSpec((1, fc), lambda i, c: (0, c)),
                  st_spec,
                  pl.BlockSpec((1, D_MODEL), lambda i, c: (0, 0)),
                  pl.BlockSpec((1, D_MODEL), lambda i, c: (0, 0))],
        out_specs=[pl.BlockSpec((tm, D_MODEL), lambda i, c: (i, 0)), last_spec],
        out_shape=[jax.ShapeDtypeStruct((t, D_MODEL), F32), last_shape],
        scratch_shapes=[pltpu.VMEM((tm, D_MODEL), BF), pltpu.VMEM((n_ff, 2, fc), F32),
                        pltpu.VMEM((fc // 128, tm, 128), F32)],
        compiler_params=_cparams(2),
        name="conv_ffn",
    )(h, w_gate_b, w_up_b, w_down_b, w_conv, b_conv.reshape(1, D_FF), state,
      ln_g.reshape(1, D_MODEL), ln_b.reshape(1, D_MODEL))


def _after_mixers(x, od, os_, memk, memv, conv_state, n_batch, t_b, w):
    h1, qc = _proj_ln([od, os_], w['w_o'], x, w['ln1_g'], w['ln1_b'], MID_ROWS, "out_proj_ln1_q", wq_b=w['w_mq'])
    oc = _cross_attn(qc, memk, memv, n_batch, t_b, min(t_b, MID_ROWS))
    (h2,) = _proj_ln([oc], w['w_mo'], h1, w['ln2_g'], w['ln2_b'], MID_ROWS, "cross_out_ln2")
    return _ffn(h2, w['w_gate'], w['w_up'], w['w_down'], w['w_conv'], w['b_conv'], conv_state,
                w['ln3_g'], w['ln3_b'], t_b, FFN_ROWS, FFN_CHUNK)


def kernel(x_prompt, x_sample, mem_prompt, cache_diff_k, cache_diff_v, cache_sb_k, cache_sb_v, cache_mem_k, cache_mem_v, state_ffn_conv, w_in, lambda_q1, lambda_k1, lambda_q2, lambda_k2, subln_g, w_o, ln1_g, ln1_b, w_mq, w_mk, w_mv, w_mo, ln2_g, ln2_b, w_gate, w_up, w_conv, b_conv, w_down, ln3_g, ln3_b):
    assert w_in.shape[0] == DEPTH == 1
    n_p, t_p, _ = x_prompt.shape
    n_s, t_s, _ = x_sample.shape
    past = cache_diff_k.shape[2]
    n_mem = mem_prompt.shape[1]
    l = 0
    lam_init = 0.8 - 0.6 * math.exp(-0.3 * l)
    lam = (jnp.exp(jnp.sum(lambda_q1[l] * lambda_k1[l])) - jnp.exp(jnp.sum(lambda_q2[l] * lambda_k2[l]))
           + lam_init).reshape(1).astype(F32)
    w = {'w_in': w_in[l].astype(BF), 'w_conv': w_conv[l], 'b_conv': b_conv[l],
         'subln_g': subln_g[l], 'ln1_g': ln1_g[l], 'ln1_b': ln1_b[l], 'ln2_g': ln2_g[l], 'ln2_b': ln2_b[l],
         'ln3_g': ln3_g[l], 'ln3_b': ln3_b[l]}
    late_weights = {'w_o': w_o[l], 'w_mq': w_mq[l], 'w_mo': w_mo[l], 'w_gate': w_gate[l], 'w_up': w_up[l],
                    'w_down': w_down[l]}
    tm = 512

    mem_b = mem_prompt.reshape(n_p * n_mem, D_MODEL).astype(BF)
    (mk_f,) = _matmul(mem_b, w_mk[l], [F32], n_p * n_mem, 1024, "memory_k")
    (mv_f,) = _matmul(mem_b, w_mv[l], [F32], n_p * n_mem, 1024, "memory_v")
    xp = x_prompt.reshape(n_p * t_p, D_MODEL)
    pb, pf = _inproj(xp, w['w_in'], np.arange(t_p), n_p, INPROJ_ROWS_PROMPT, sb_interleave=SB_BLOCK)
    odp, osp, late_b = _mixers_prompt(lam, pb, w['subln_g'], n_p, t_p, lam_init,
                                      cast_weights=tuple(late_weights.values()))
    w.update(zip(late_weights, late_b))
    zero_state = jnp.zeros((n_p, CONV_W - 1, D_FF), F32)
    mem_heads = (n_p, n_mem, N_MEM_HEADS, MEM_HEAD_DIM)
    yp, lastp = _after_mixers(xp, odp, osp, _tiled_lane_rows(mk_f.reshape(mem_heads), n_p, N_MEM_HEADS),
                              _tiled_lane_rows(mv_f.reshape(mem_heads), n_p, N_MEM_HEADS),
                              zero_state, n_p, t_p, w)

    xs = x_sample.reshape(n_s * t_s, D_MODEL)
    sb, sf = _inproj(xs, w['w_in'], past + np.arange(t_s), n_s, tm)
    rows = lambda c: _as_lane_rows(c[l], n_s)
    ods = _diff_attn_sample(lam, sb['dq'], sb['dk'], sb['dv'], rows(cache_diff_k),
                            _tiled_lane_rows(cache_diff_v[l], n_s, N_DIFF_HEADS), w['subln_g'],
                            n_s, t_s, past, lam_init)
    oss = _sb_attn_sample(sb['sq'], sb['sk'], sb['sv'], rows(cache_sb_k), rows(cache_sb_v), n_s, t_s, past)
    state_s = jnp.transpose(state_ffn_conv[l], (1, 0, 2))
    ys, lasts = _after_mixers(xs, ods, oss, _tiled_lane_rows(cache_mem_k[l], n_s, N_MEM_HEADS),
                              _tiled_lane_rows(cache_mem_v[l], n_s, N_MEM_HEADS), state_s, n_s, t_s, w)

    tiles_per_seq = t_p // FFN_ROWS
    new_conv_p = lastp[tiles_per_seq - 1::tiles_per_seq]
    new_conv_s = jnp.transpose(lasts, (1, 0, 2))

    def new_rows(f, n, t):
        dv = jnp.transpose(f['dv'].reshape(n, t, DIFF_V_DIM // HEAD_DIM, N_DIFF_HEADS, HEAD_DIM), (0, 1, 3, 2, 4))
        return (f['dk'].reshape(1, n, t, N_DIFF_HEADS, 2, HEAD_DIM), dv.reshape(1, n, t, N_DIFF_HEADS, DIFF_V_DIM),
                f['sk'].reshape(1, n, t, N_SB_HEADS, HEAD_DIM), f['sv'].reshape(1, n, t, N_SB_HEADS, HEAD_DIM))

    return ((yp.reshape(n_p, t_p, D_MODEL), ys.reshape(n_s, t_s, D_MODEL))
            + new_rows(pf, n_p, t_p)
            + (mk_f.reshape((1,) + mem_heads), mv_f.reshape((1,) + mem_heads),
               new_conv_p[None])
            + new_rows(sf, n_s, t_s)
            + (new_conv_s[None],))
```

```python
import functools
import math

import jax
import jax.numpy as jnp
import numpy as np
from jax import lax
from jax.experimental import pallas as pl
from jax.experimental.pallas import tpu as pltpu

BF = jnp.bfloat16
F32 = jnp.float32

D_MODEL = 2048
CHUNK = 64
CHUNK_SHIFT = 6
HEAD_DIM = 128
N_DIFF_HEADS = 4
N_SB_HEADS = 8
DIFF_V_DIM = 2 * HEAD_DIM
ROPE_DIM = HEAD_DIM // 4
ROPE_THETA = 500000.0
N_MEM_HEADS = 4
MEM_HEAD_DIM = D_MODEL // N_MEM_HEADS
D_FF = 5632
CONV_W = 3
LN_EPS = 1e-5
DEPTH = 1
DN_ALPHA = (2 * DEPTH) ** 0.25
SEG = 1024
N_SEG = 6
VMEM_LIMIT = 56 * 1024 * 1024
LOG2E = math.log2(math.e)
QK_SCALE2 = HEAD_DIM ** -0.5 * LOG2E
NEG_BIG = -1e30
LN_ROWS = 128
FFN_COLS = 256
FFN_ROWS = 1024
FFN_CHUNK = 512
INPROJ_ROWS = 1024
INPROJ_SLICES = 4
MID_ROWS = 512
SB_BLOCK = 256

_NT = (((1,), (1,)), ((), ()))


def _cparams(n_axes):
    return pltpu.CompilerParams(dimension_semantics=("arbitrary",) * n_axes,
                                vmem_limit_bytes=VMEM_LIMIT)


def _layer_norm(y, g, b):
    mu = jnp.mean(y, axis=-1, keepdims=True)
    yc = y - mu
    var = jnp.mean(yc * yc, axis=-1, keepdims=True)
    return yc * lax.rsqrt(var + LN_EPS) * g + b


SLOTS = SEG // HEAD_DIM


def _inproj_kernel(*refs, rope, want_f32, tile_rows, scale, emit_xb, interleave):
    x_ref, w_ref = refs[:2]
    tabs = refs[2:5] if rope else None
    outs = refs[5:] if rope else refs[2:]
    if interleave:
        outs, perm_ref = outs[:-1], outs[-1]
    pb_ref = outs[0]
    xb = x_ref[...].astype(BF)
    if emit_xb:
        outs[-1][...] = xb
    tm = xb.shape[0]
    hm = max(tm // INPROJ_SLICES, SB_BLOCK)
    starts = range(0, tm, hm)
    accs = [jnp.dot(xb[r0:r0 + hm, :], w_ref[...], preferred_element_type=F32) for r0 in starts]
    for r0, acc in zip(starts, accs):
        for ch in range(SLOTS):
            lanes = slice(ch * HEAD_DIM, (ch + 1) * HEAD_DIM)
            r = acc[:, lanes]
            if rope:
                c_ref, s1_ref, s2_ref = tabs
                r = (r * c_ref[r0:r0 + hm, :] + pltpu.roll(r, HEAD_DIM - ROPE_DIM // 2, 1) * s1_ref[r0:r0 + hm, :]
                     + pltpu.roll(r, ROPE_DIM // 2, 1) * s2_ref[r0:r0 + hm, :])
            elif scale != 1.0:
                r = r * scale
            if interleave:
                run = interleave // SUBLANES
                for b0 in range(0, hm, interleave):
                    for s in range(SUBLANES):
                        perm_ref[ch, pl.ds(r0 + b0 + s, run, stride=SUBLANES), :] = r[b0 + s * run:b0 + (s + 1) * run, :]
                pb_ref[0, r0:r0 + hm, lanes] = perm_ref[ch, r0:r0 + hm, :].astype(BF)
            else:
                pb_ref[0, r0:r0 + hm, lanes] = r.astype(BF)
            if want_f32:
                head, col_tile = divmod(ch, SLOTS // tile_rows) if tile_rows else (0, 0)
                off = col_tile * tile_rows + head if tile_rows else ch
                outs[1][pl.ds(r0 * SLOTS + off, hm, stride=SLOTS), :] = r


def _rope_tables(pos, tile_rows, scale):
    half = ROPE_DIM // 2
    n = pos.shape[0]
    assert n % tile_rows == 0 or tile_rows % n == 0
    inv_freq = np.power(ROPE_THETA, -np.arange(half, dtype=np.float64) * (2.0 / ROPE_DIM))
    ang = pos.astype(np.float64)[:, None] * inv_freq[None, :]
    cos, sin = np.cos(ang) * scale, np.sin(ang) * scale
    c = np.concatenate([cos, cos, np.full((n, HEAD_DIM - ROPE_DIM), scale)], axis=1)
    s1 = np.concatenate([-sin, np.zeros((n, HEAD_DIM - half))], axis=1)
    s2 = np.concatenate([np.zeros((n, half)), sin, np.zeros((n, HEAD_DIM - ROPE_DIM))], axis=1)
    reps = max(1, tile_rows // n)
    return tuple(jnp.asarray(np.tile(tb, (reps, 1)), F32) for tb in (c, s1, s2))


IN_GROUPS = (("dq", True, False, 0, QK_SCALE2), ("dk", True, True, 0, 1.0), ("dv", False, True, N_DIFF_HEADS, 1.0),
             ("sq", False, False, 0, QK_SCALE2), ("sk", False, True, 0, 1.0), ("sv", False, True, 0, 1.0))


def _inproj(x, w_in_b, pos, n_seq, tm, sb_interleave=0):
    t = x.shape[0]
    assert t == n_seq * pos.shape[0]
    row = lambda i: (i, 0)
    tab_blocks = max(1, pos.shape[0] // tm)
    tab = pl.BlockSpec((tm, HEAD_DIM), lambda i: (i % tab_blocks, 0))
    bf, f32 = {}, {}
    xb = None
    for j, (name, rope, want_f32, tile_rows, scale) in enumerate(IN_GROUPS):
        assert not (want_f32 and scale != 1.0)
        emit_xb = xb is None
        interleave = sb_interleave if name in ("sk", "sv") else 0
        outs = pl.pallas_call(
            functools.partial(_inproj_kernel, rope=rope, want_f32=want_f32, tile_rows=tile_rows, scale=scale,
                              emit_xb=emit_xb, interleave=interleave),
            scratch_shapes=[pltpu.VMEM((SLOTS, tm, HEAD_DIM), F32)] if interleave else [],
            grid=(t // tm,),
            in_specs=[pl.BlockSpec((tm, D_MODEL), row),
                      pl.BlockSpec((D_MODEL, SEG), functools.partial(lambda i, j: (0, j), j=j))]
                     + ([tab, tab, tab] if rope else []),
            out_specs=[pl.BlockSpec((1, tm, SEG), lambda i: (0, i, 0))]
                      + ([pl.BlockSpec((tm * SLOTS, HEAD_DIM), row)] if want_f32 else [])
                      + ([pl.BlockSpec((tm, D_MODEL), row)] if emit_xb else []),
            out_shape=[jax.ShapeDtypeStruct((1, t, SEG), BF)]
                      + ([jax.ShapeDtypeStruct((t * SLOTS, HEAD_DIM), F32)] if want_f32 else [])
                      + ([jax.ShapeDtypeStruct((t, D_MODEL), BF)] if emit_xb else []),
            compiler_params=_cparams(1),
            name="inproj_" + name,
        )(x if emit_xb else xb, w_in_b, *(_rope_tables(pos, tm, scale) if rope else ()))
        bf[name] = outs[0]
        if want_f32:
            f32[name] = outs[1]
        if emit_xb:
            xb = outs[-1]
    return bf, f32


def _matmul_kernel(x_ref, w_ref, *o_refs):
    acc = jnp.dot(x_ref[...], w_ref[...].astype(BF), preferred_element_type=F32)
    for o in o_refs:
        o[...] = acc.astype(o.dtype)


def _matmul(x_b, w_b, out_dtypes, tm, tn, name):
    t, k = x_b.shape
    n = w_b.shape[1]
    return pl.pallas_call(
        _matmul_kernel,
        grid=(t // tm, n // tn),
        in_specs=[pl.BlockSpec((tm, k), lambda i, j: (i, 0)),
                  pl.BlockSpec((k, tn), lambda i, j: (0, j))],
        out_specs=[pl.BlockSpec((tm, tn), lambda i, j: (i, j)) for _ in out_dtypes],
        out_shape=[jax.ShapeDtypeStruct((t, n), d) for d in out_dtypes],
        compiler_params=_cparams(2),
        name=name,
    )(x_b, w_b)


def _chunk_visible(q0, k0, nq, nk):
    qpos = q0 + lax.broadcasted_iota(jnp.int32, (nq, nk), 0)
    kpos = k0 + lax.broadcasted_iota(jnp.int32, (nq, nk), 1)
    return lax.shift_right_logical(kpos, CHUNK_SHIFT) <= lax.shift_right_logical(qpos, CHUNK_SHIFT)


def _subln(o0, o1, lam, g, lam_init):
    of = o0 - lam * o1
    return of * lax.rsqrt(jnp.mean(of * of, axis=-1, keepdims=True) + LN_EPS) * g * (1.0 - lam_init)


def _diff_prompt_kernel(*refs, tq, sub, lam_init, n_cast):
    lam_ref, q_ref, k_ref, v_ref, g_ref = refs[:5]
    o_ref = refs[5 + n_cast]
    vt_ref, m_ref, l_ref, acc_ref = refs[6 + 2 * n_cast:]
    for w_ref, wb_ref in zip(refs[5:5 + n_cast], refs[6 + n_cast:6 + 2 * n_cast]):
        wb_ref[...] = w_ref[...].astype(BF)
    qi = pl.program_id(2)
    n_sub = tq // sub

    @pl.when(qi == 0)
    def _():
        for g in range(vt_ref.shape[0]):
            for j in range(n_sub):
                blk = v_ref[0, g * tq + j * sub:g * tq + (j + 1) * sub, :]
                vt_ref[g, :, j * sub:(j + 1) * sub] = blk.astype(F32).T.astype(BF)

    m_ref[...] = jnp.full(m_ref.shape, NEG_BIG, F32)
    l_ref[...] = jnp.zeros(l_ref.shape, F32)
    acc_ref[...] = jnp.zeros(acc_ref.shape, F32)

    def run(g, items):
        k0 = pl.multiple_of(g * tq, tq)
        scores = []
        for c, r, nk, vis in items:
            cs = slice(c * HEAD_DIM, (c + 1) * HEAD_DIM)
            s = lax.dot_general(k_ref[0, pl.ds(k0, nk), cs], q_ref[0, r * sub:(r + 1) * sub, cs], _NT,
                                preferred_element_type=F32)
            if vis is not None:
                tail = jnp.where(vis, s[nk - sub:, :], -jnp.inf)
                s = tail if nk == sub else jnp.concatenate([s[:nk - sub, :], tail], axis=0)
            scores.append(s)
        probs = []
        for s, (c, r, nk, _) in zip(scores, items):
            i = c * n_sub + r
            m_old = m_ref[i]
            m_new = jnp.maximum(m_old, jnp.max(s, axis=0, keepdims=True))
            alpha = jnp.exp2(m_old - m_new)
            p = jnp.exp2(s - m_new)
            l_ref[i] = alpha * l_ref[i] + jnp.sum(p, axis=0, keepdims=True)
            m_ref[i] = m_new
            probs.append((alpha, p.astype(BF)))
        for (alpha, p), (c, r, nk, _) in zip(probs, items):
            i = c * n_sub + r
            acc_ref[i] = alpha * acc_ref[i] + jnp.dot(vt_ref[g, :, :nk], p, preferred_element_type=F32)

    full_items = [(c, r, tq, None) for r in range(n_sub) for c in range(2)]

    def body(g, carry):
        run(g, full_items)
        return carry

    lax.fori_loop(0, qi, body, 0)

    k_i = lax.broadcasted_iota(jnp.int32, (sub, sub), 0)
    q_i = lax.broadcasted_iota(jnp.int32, (sub, sub), 1)
    vis = lax.shift_right_logical(k_i, CHUNK_SHIFT) <= lax.shift_right_logical(q_i, CHUNK_SHIFT)
    run(qi, [(c, r, (r + 1) * sub, vis) for r in range(n_sub) for c in range(2)])

    for r in range(n_sub):
        o0 = (acc_ref[r] / l_ref[r]).T
        o1 = (acc_ref[n_sub + r] / l_ref[n_sub + r]).T
        o_ref[r * sub:(r + 1) * sub, :] = _subln(o0, o1, lam_ref[0], g_ref[...], lam_init).astype(BF)


def _diff_attn_prompt(lam, q, k, v, subln_g, n_batch, t, lam_init, cast_weights=(), tq=1024, sub=256):
    assert t % tq == 0 and tq % sub == 0 and sub % CHUNK == 0
    nq = t // tq
    n_steps = n_batch * N_DIFF_HEADS * nq
    step = lambda b, h, i: ((b * N_DIFF_HEADS + h) * nq + i, 0)
    cast_specs = []
    for wt in cast_weights:
        assert wt.shape[0] % (16 * n_steps) == 0
        cast_specs.append(pl.BlockSpec((wt.shape[0] // n_steps, wt.shape[1]), step))
    kern = functools.partial(_diff_prompt_kernel, tq=tq, sub=sub, lam_init=lam_init, n_cast=len(cast_weights))
    outs = pl.pallas_call(
        kern,
        grid=(n_batch, N_DIFF_HEADS, nq),
        in_specs=[pl.BlockSpec(memory_space=pltpu.SMEM),
                  pl.BlockSpec((1, tq, DIFF_V_DIM), lambda b, h, i: (0, b * nq + i, h)),
                  pl.BlockSpec((1, t, DIFF_V_DIM), lambda b, h, i: (0, b, h)),
                  pl.BlockSpec((1, t, DIFF_V_DIM), lambda b, h, i: (0, b, h)),
                  pl.BlockSpec((1, DIFF_V_DIM), lambda b, h, i: (0, 0))] + cast_specs,
        out_specs=[pl.BlockSpec((tq, DIFF_V_DIM), lambda b, h, i: (b * nq + i, h))] + cast_specs,
        out_shape=[jax.ShapeDtypeStruct((n_batch * t, SEG), BF)]
                  + [jax.ShapeDtypeStruct(wt.shape, BF) for wt in cast_weights],
        scratch_shapes=[pltpu.VMEM((nq, DIFF_V_DIM, tq), BF),
                        pltpu.VMEM((2 * tq // sub, 1, sub), F32), pltpu.VMEM((2 * tq // sub, 1, sub), F32),
                        pltpu.VMEM((2 * tq // sub, DIFF_V_DIM, sub), F32)],
        compiler_params=_cparams(3),
        name="diff_attn_prompt",
    )(lam, q, k, v, subln_g.reshape(1, DIFF_V_DIM), *cast_weights)
    return outs[0], outs[1:]


def _diff_sample_kernel(lam_ref, q_ref, kn_ref, vn_ref, kc_ref, vc_ref, g_ref, o_ref, *, t_s, past, lam_init):
    vis = _chunk_visible(past, past, t_s, t_s)
    pairs = [(h, c) for h in range(N_DIFF_HEADS) for c in range(2)]
    cols = [slice((2 * h + c) * HEAD_DIM, (2 * h + c + 1) * HEAD_DIM) for h, c in pairs]
    s_old = [lax.dot_general(q_ref[0, :, cs], _lane_rows(kc_ref, 0, past, 2 * h + c, HEAD_DIM, SEG).astype(BF), _NT,
                             preferred_element_type=F32) for (h, c), cs in zip(pairs, cols)]
    s_new = [jnp.where(vis, lax.dot_general(q_ref[0, :, cs], kn_ref[0, :, cs], _NT, preferred_element_type=F32),
                       -jnp.inf) for cs in cols]
    probs = []
    for so, sn in zip(s_old, s_new):
        m = jnp.maximum(jnp.max(so, axis=-1, keepdims=True), jnp.max(sn, axis=-1, keepdims=True))
        p_old = jnp.exp2(so - m)
        p_new = jnp.exp2(sn - m)
        l = jnp.sum(p_old, axis=-1, keepdims=True) + jnp.sum(p_new, axis=-1, keepdims=True)
        probs.append((p_old.astype(BF), p_new.astype(BF), l))
    vals = [slice(h * DIFF_V_DIM, (h + 1) * DIFF_V_DIM) for h in range(N_DIFF_HEADS)]
    v_old = [_lane_rows(vc_ref, 0, past, h, DIFF_V_DIM, SEG, tiled=True).astype(BF) for h in range(N_DIFF_HEADS)]
    outs = [(jnp.dot(p_old, v_old[h], preferred_element_type=F32)
             + jnp.dot(p_new, vn_ref[0, :, vals[h]], preferred_element_type=F32)) / l
            for (h, _), (p_old, p_new, l) in zip(pairs, probs)]
    for h in range(N_DIFF_HEADS):
        o_ref[:, vals[h]] = _subln(outs[2 * h], outs[2 * h + 1], lam_ref[0], g_ref[...], lam_init).astype(BF)


def _new_rows_spec(t_s):
    return pl.BlockSpec((1, t_s, SEG), lambda b: (0, b, 0))


def _as_lane_rows(x, n_batch):
    return x.reshape(n_batch, -1, HEAD_DIM)


def _lane_rows(ref, t0, nt, slot, slot_width, width, tiled=False):
    per_row = width // HEAD_DIM
    per_slot = slot_width // HEAD_DIM
    n_slots = width // slot_width
    offs = [(k * n_slots + slot) if tiled else (slot * per_slot + k) for k in range(per_slot)]
    parts = [ref[0, pl.ds(t0 * per_row + o, nt, stride=per_row), :] for o in offs]
    return parts[0] if per_slot == 1 else jnp.concatenate(parts, axis=1)


def _tiled_lane_rows(x, n_batch, n_slots):
    rows, w = x.shape[1], x.shape[3]
    x = x.reshape(n_batch, rows, n_slots, w // HEAD_DIM, HEAD_DIM)
    return jnp.transpose(x, (0, 1, 3, 2, 4)).reshape(n_batch, -1, HEAD_DIM)


def _cache_spec(cache):
    return pl.BlockSpec((1,) + cache.shape[1:], lambda b: (b, 0, 0))


def _diff_attn_sample(lam, q, k_new, v_new, cache_k, cache_v, subln_g, n_batch, t_s, past, lam_init):
    kern = functools.partial(_diff_sample_kernel, t_s=t_s, past=past, lam_init=lam_init)
    return pl.pallas_call(
        kern,
        grid=(n_batch,),
        in_specs=[pl.BlockSpec(memory_space=pltpu.SMEM), _new_rows_spec(t_s), _new_rows_spec(t_s),
                  _new_rows_spec(t_s), _cache_spec(cache_k), _cache_spec(cache_v),
                  pl.BlockSpec((1, DIFF_V_DIM), lambda b: (0, 0))],
        out_specs=pl.BlockSpec((t_s, SEG), lambda b: (b, 0)),
        out_shape=jax.ShapeDtypeStruct((n_batch * t_s, SEG), BF),
        compiler_params=_cparams(1),
        name="diff_attn_sample",
    )(lam, q, k_new, v_new, cache_k, cache_v, subln_g.reshape(1, DIFF_V_DIM))


def _sb_neg_log_keep(z2):
    neg_abs = jnp.minimum(z2, -z2)
    return jnp.maximum(z2, 0.0) + jnp.log(1.0 + jnp.exp2(neg_abs)) * LOG2E


def _split_bf16(x):
    hi = x.astype(BF)
    return hi, (x - hi.astype(F32)).astype(BF)


SUBLANES = 8


def _interleaved_key(n):
    r = lax.broadcasted_iota(jnp.int32, (n, 1), 0)
    return jnp.bitwise_and(r, SUBLANES - 1) * (n // SUBLANES) + lax.shift_right_logical(r, 3)


def _suffix_interleaved(x, carry):
    nj = x.shape[0] // SUBLANES
    run = [None] * nj
    acc = None
    for j in reversed(range(nj)):
        part = x[SUBLANES * j:SUBLANES * (j + 1), :]
        acc = part if acc is None else acc + part
        run[j] = acc
    g = run[0]
    row = lax.broadcasted_iota(jnp.int32, g.shape, 0)
    incl = g
    for d in (1, 2, 4):
        incl = incl + jnp.where(row < SUBLANES - d, pltpu.roll(incl, SUBLANES - d, 0), 0.0)
    later = incl - g + carry
    return jnp.concatenate([r + later for r in run], axis=0), incl[0:1, :]


def _sb_group(items, carry, acc, key_axis):
    if key_axis == 0:
        zs = [lax.dot_general(k, q, _NT, preferred_element_type=F32) for q, k, _, _, _, _ in items]
    else:
        zs = [lax.dot_general(q, k, _NT, preferred_element_type=F32) for q, k, _, _, _, _ in items]
    splits = []
    for z, (_, _, _, tri, before, _) in zip(zs, items):
        nlk = _sb_neg_log_keep(z)
        if before is not None:
            nlk = jnp.where(before, nlk, 0.0)
        if tri is None:
            splits.append(nlk)
            continue
        hi, lo = _split_bf16(nlk)
        fused = tri.shape[1 - key_axis] == 2 * tri.shape[key_axis]
        splits.append(jnp.concatenate([hi, lo], axis=key_axis) if fused else (hi, lo))
    sufs = []
    for hl, (_, _, _, tri, _, ch) in zip(splits, items):
        if tri is None:
            suf, total = _suffix_interleaved(hl, carry[ch])
        else:
            if key_axis == 0:
                mm = lambda x: jnp.dot(tri, x, preferred_element_type=F32)
            else:
                mm = lambda x: jnp.dot(x, tri, preferred_element_type=F32)
            suf = mm(hl[0]) + mm(hl[1]) if isinstance(hl, tuple) else mm(hl)
            total = suf[0:1, :] if key_axis == 0 else suf[:, 0:1]
            suf = suf + carry[ch]
        sufs.append(suf)
        carry[ch] = carry[ch] + total
    probs = []
    for z, suf, (_, _, _, _, before, ch) in zip(zs, sufs, items):
        a = jnp.exp2(z - suf)
        if before is not None:
            a = jnp.where(before, a, 0.0)
        probs.append(a.astype(BF))
    for a, (_, _, v_op, _, _, ch) in zip(probs, items):
        if key_axis == 0:
            acc[ch] = acc[ch] + jnp.dot(v_op, a, preferred_element_type=F32)
        else:
            acc[ch] = acc[ch] + jnp.dot(a, v_op, preferred_element_type=F32)


def _tri(n, upper):
    r = lax.broadcasted_iota(jnp.int32, (n, n), 0)
    c = lax.broadcasted_iota(jnp.int32, (n, n), 1)
    return jnp.where((c >= r) if upper else (r >= c), 1.0, 0.0).astype(BF)


def _sb_prompt_kernel(q_ref, k_ref, v_ref, o_ref, vt_ref, carry_ref, acc_ref, *, tq, sub, t, unroll):
    qi = pl.program_id(2)
    n_sub = tq // sub

    @pl.when(qi == 0)
    def _():
        for kb in range(t // sub):
            vt_ref[kb] = v_ref[0, kb * sub:(kb + 1) * sub, :].astype(F32).T.astype(BF)

    tri2 = None
    before = _interleaved_key(sub) < lax.broadcasted_iota(jnp.int32, (sub, sub), 1)
    qs = [q_ref[0, c * sub:(c + 1) * sub, :] for c in range(n_sub)]
    carry = [jnp.zeros((1, sub), F32) for _ in range(n_sub)]
    acc = [jnp.zeros((HEAD_DIM, sub), F32) for _ in range(n_sub)]

    base = qi * n_sub
    items = []
    for j in reversed(range(n_sub)):
        kblk = k_ref[0, pl.ds(pl.multiple_of((base + j) * sub, sub), sub), :]
        vt_blk = vt_ref[base + j]
        items += [(qs[c], kblk, vt_blk, tri2, before if c == j else None, c) for c in range(j, n_sub)]
    _sb_group(items, carry, acc, 0)
    for c in range(n_sub):
        carry_ref[c] = carry[c]
        acc_ref[c] = acc[c]

    def body(i, x):
        cr = [carry_ref[c] for c in range(n_sub)]
        ac = [acc_ref[c] for c in range(n_sub)]
        items = []
        for uu in range(unroll):
            kb = base - 1 - (i * unroll + uu)
            kblk = k_ref[0, pl.ds(pl.multiple_of(kb * sub, sub), sub), :]
            vt_blk = vt_ref[kb]
            items += [(qs[c], kblk, vt_blk, tri2, None, c) for c in range(n_sub)]
        _sb_group(items, cr, ac, 0)
        for c in range(n_sub):
            carry_ref[c] = cr[c]
            acc_ref[c] = ac[c]
        return x

    lax.fori_loop(0, (qi * n_sub) // unroll, body, 0)
    for c in range(n_sub):
        o_ref[c * sub:(c + 1) * sub, :] = acc_ref[c].T.astype(BF)


def _sb_attn_prompt(q, k, v, n_batch, t, tq=1024, sub=SB_BLOCK, unroll=2):
    assert t % tq == 0 and tq % sub == 0 and (tq // sub) % unroll == 0
    nq = t // tq
    n_sub = tq // sub
    kern = functools.partial(_sb_prompt_kernel, tq=tq, sub=sub, t=t, unroll=unroll)
    return pl.pallas_call(
        kern,
        grid=(n_batch, N_SB_HEADS, nq),
        in_specs=[pl.BlockSpec((1, tq, HEAD_DIM), lambda b, h, i: (0, b * nq + i, h)),
                  pl.BlockSpec((1, t, HEAD_DIM), lambda b, h, i: (0, b, h)),
                  pl.BlockSpec((1, t, HEAD_DIM), lambda b, h, i: (0, b, h))],
        out_specs=pl.BlockSpec((tq, HEAD_DIM), lambda b, h, i: (b * nq + i, h)),
        out_shape=jax.ShapeDtypeStruct((n_batch * t, SEG), BF),
        scratch_shapes=[pltpu.VMEM((t // sub, HEAD_DIM, sub), BF), pltpu.VMEM((n_sub, 1, sub), F32),
                        pltpu.VMEM((n_sub, HEAD_DIM, sub), F32)],
        compiler_params=_cparams(3),
        name="sb_attn_prompt",
    )(q, k, v)


def _sb_sample_kernel(q_ref, kn_ref, vn_ref, kc_ref, vc_ref, o_ref, *, t_s, past, tk):
    tri_new = _tri(t_s, False)
    l_old = _tri(tk, False)
    tri_old = jnp.concatenate([l_old, l_old], axis=0)
    before = (lax.broadcasted_iota(jnp.int32, (t_s, t_s), 1) < lax.broadcasted_iota(jnp.int32, (t_s, t_s), 0))
    heads = range(N_SB_HEADS)
    cols = [slice(h * HEAD_DIM, (h + 1) * HEAD_DIM) for h in heads]
    qs = [q_ref[0, :, cs] for cs in cols]
    carry = [jnp.zeros((t_s, 1), F32) for _ in heads]
    acc = [jnp.zeros((t_s, HEAD_DIM), F32) for _ in heads]
    _sb_group([(qs[h], kn_ref[0, :, cols[h]], vn_ref[0, :, cols[h]], tri_new, before, h) for h in heads],
              carry, acc, 1)
    for kb in reversed(range(past // tk)):
        rows = slice(kb * tk, (kb + 1) * tk)
        _sb_group([(qs[h], _lane_rows(kc_ref, kb * tk, tk, h, HEAD_DIM, SEG).astype(BF),
                    _lane_rows(vc_ref, kb * tk, tk, h, HEAD_DIM, SEG).astype(BF), tri_old, None, h)
                   for h in heads], carry, acc, 1)
    for h in heads:
        o_ref[:, cols[h]] = acc[h].astype(BF)


def _sb_attn_sample(q, k_new, v_new, cache_k, cache_v, n_batch, t_s, past, tk=256):
    assert past % tk == 0
    kern = functools.partial(_sb_sample_kernel, t_s=t_s, past=past, tk=tk)
    return pl.pallas_call(
        kern,
        grid=(n_batch,),
        in_specs=[_new_rows_spec(t_s), _new_rows_spec(t_s), _new_rows_spec(t_s),
                  _cache_spec(cache_k), _cache_spec(cache_v)],
        out_specs=pl.BlockSpec((t_s, SEG), lambda b: (b, 0)),
        out_shape=jax.ShapeDtypeStruct((n_batch * t_s, SEG), BF),
        compiler_params=_cparams(1),
        name="sb_attn_sample",
    )(q, k_new, v_new, cache_k, cache_v)


MEM_SCALE2 = MEM_HEAD_DIM ** -0.5 * LOG2E


def _proj_ln_kernel(*refs, n_in, with_q):
    a_refs = refs[:n_in]
    w_refs = refs[n_in:2 * n_in]
    res_ref, g_ref, b_ref = refs[2 * n_in:2 * n_in + 3]
    rest = refs[2 * n_in + 3:]
    tm = res_ref.shape[0]
    slices = [slice(r, r + LN_ROWS) for r in range(0, tm, LN_ROWS)]
    accs = []
    for rows in slices:
        acc = None
        for a, w in zip(a_refs, w_refs):
            d = jnp.dot(a[rows, :], w[...], preferred_element_type=F32)
            acc = d if acc is None else acc + d
        accs.append(acc)
    outs = [_layer_norm(DN_ALPHA * res_ref[rows, :] + acc, g_ref[...], b_ref[...]) for rows, acc in zip(slices, accs)]
    if with_q:
        wq_ref, of_ref, oq_ref = rest
        for rows, out in zip(slices, outs):
            of_ref[rows, :] = out
        for rows, out in zip(slices, outs):
            q = jnp.dot(out.astype(BF), wq_ref[...], preferred_element_type=F32)
            oq_ref[rows, :] = (q * MEM_SCALE2).astype(BF)
    else:
        (of_ref,) = rest
        for rows, out in zip(slices, outs):
            of_ref[rows, :] = out


def _proj_ln(a_list, w_b, res, g, b, tm, name, wq_b=None):
    t = res.shape[0]
    n_in = len(a_list)
    ka = a_list[0].shape[1]
    row = lambda i: (i, 0)
    once = dict(pipeline_mode=pl.Buffered(1))
    full = pl.BlockSpec((tm, D_MODEL), row)
    vec = pl.BlockSpec((1, D_MODEL), lambda i: (0, 0))
    in_specs = ([pl.BlockSpec((tm, ka), row) for _ in a_list]
                + [pl.BlockSpec((ka, D_MODEL), functools.partial(lambda i, r: (r, 0), r=r), **once) for r in range(n_in)]
                + [full, vec, vec])
    args = [*a_list, *([w_b] * n_in), res, g.reshape(1, D_MODEL), b.reshape(1, D_MODEL)]
    out_specs = [full]
    out_shape = [jax.ShapeDtypeStruct((t, D_MODEL), F32)]
    if wq_b is not None:
        in_specs.append(pl.BlockSpec((D_MODEL, D_MODEL), lambda i: (0, 0), **once))
        args.append(wq_b)
        out_specs.append(full)
        out_shape.append(jax.ShapeDtypeStruct((t, D_MODEL), BF))
    return pl.pallas_call(
        functools.partial(_proj_ln_kernel, n_in=n_in, with_q=wq_b is not None),
        grid=(t // tm,),
        in_specs=in_specs,
        out_specs=out_specs,
        out_shape=out_shape,
        compiler_params=_cparams(1),
        name=name,
    )(*args)


def _cross_kernel(q_ref, k_ref, v_ref, o_ref):
    n_mem = k_ref.shape[1] * HEAD_DIM // D_MODEL
    for h in range(N_MEM_HEADS):
        sl = slice(h * MEM_HEAD_DIM, (h + 1) * MEM_HEAD_DIM)
        k = _lane_rows(k_ref, 0, n_mem, h, MEM_HEAD_DIM, D_MODEL, tiled=True).astype(BF)
        v = _lane_rows(v_ref, 0, n_mem, h, MEM_HEAD_DIM, D_MODEL, tiled=True).astype(BF)
        s = lax.dot_general(q_ref[:, sl], k, _NT, preferred_element_type=F32)
        p = jnp.exp2(s - jnp.max(s, axis=-1, keepdims=True))
        l = jnp.sum(p, axis=-1, keepdims=True)
        o = jnp.dot(p.astype(BF), v, preferred_element_type=F32) / l
        o_ref[:, sl] = o.astype(BF)


def _cross_attn(q_b, memk, memv, n_batch, t_b, tm):
    nt = t_b // tm
    mem_spec = pl.BlockSpec((1,) + memk.shape[1:], lambda b, i: (b, 0, 0))
    return pl.pallas_call(
        _cross_kernel,
        grid=(n_batch, nt),
        in_specs=[pl.BlockSpec((tm, D_MODEL), lambda b, i: (b * nt + i, 0)), mem_spec, mem_spec],
        out_specs=pl.BlockSpec((tm, D_MODEL), lambda b, i: (b * nt + i, 0)),
        out_shape=jax.ShapeDtypeStruct((n_batch * t_b, D_MODEL), BF),
        compiler_params=_cparams(2),
        name="cross_attn",
    )(q_b, memk, memv)


def _gelu_tanh(x):
    return 0.5 * x * (1.0 + jnp.tanh(math.sqrt(2.0 / math.pi) * (x + 0.044715 * (x * x * x))))


def _ffn_kernel(h_ref, wg_ref, wu_ref, wd_ref, wc_ref, bc_ref, st_ref, g_ref, b_ref,
                o_ref, last_ref, hb_ref, carry_ref, gs_ref, *, tm, fc, seq, n_ff):
    i = pl.program_id(0)
    c = pl.program_id(1)

    @pl.when(c == 0)
    def _():
        h = h_ref[...]
        hb_ref[...] = h.astype(BF)
        o_ref[...] = DN_ALPHA * h

    @pl.when((i == 0) & (c == 0))
    def _():
        carry_ref[...] = jnp.zeros(carry_ref.shape, F32)

    hb = hb_ref[...]
    halves = [slice(k, k + FFN_COLS) for k in range(0, fc, FFN_COLS)]
    gs = [jnp.dot(hb, wg_ref[:, cols], preferred_element_type=F32) for cols in halves]
    us = [jnp.dot(hb, wu_ref[:, cols], preferred_element_type=F32) for cols in halves]
    row = lax.broadcasted_iota(jnp.int32, (tm, FFN_COLS), 0)
    acts = []
    for g, u, cols in zip(gs, us, halves):
        if seq >= tm:
            tiles_per_seq = seq // tm
            at_start = (i % tiles_per_seq) == 0
            prev = jnp.where(at_start, st_ref[0, :, cols], carry_ref[c, :, cols])
            pos = row
            ex0 = prev[0:1]
            ex1 = prev[1:2]
            carry_ref[c, :, cols] = g[tm - 2:tm, :]
            last_ref[0, :, cols] = g[tm - 2:tm, :]
        else:
            ns = tm // seq
            pos = jnp.bitwise_and(row, seq - 1)
            ex0 = jnp.broadcast_to(st_ref[0, :, cols][:, None, :], (ns, seq, FFN_COLS)).reshape(tm, FFN_COLS)
            ex1 = jnp.broadcast_to(st_ref[1, :, cols][:, None, :], (ns, seq, FFN_COLS)).reshape(tm, FFN_COLS)
            for k in range(FFN_COLS // 128):
                lanes = slice(cols.start + k * 128, cols.start + (k + 1) * 128)
                gs_ref[k] = g[:, k * 128:(k + 1) * 128]
                last_ref[0, :, lanes] = gs_ref[k, pl.ds(seq - 2, ns, stride=seq), :]
                last_ref[1, :, lanes] = gs_ref[k, pl.ds(seq - 1, ns, stride=seq), :]
        g_m1 = jnp.where(pos == 0, ex1, pltpu.roll(g, 1, 0))
        g_m2 = jnp.where(pos == 0, ex0, jnp.where(pos == 1, ex1, pltpu.roll(g, 2, 0)))
        conv = (bc_ref[:, cols] + g_m2 * wc_ref[0:1, cols] + g_m1 * wc_ref[1:2, cols] + g * wc_ref[2:3, cols])
        acts.append((_gelu_tanh(conv) * u).astype(BF))
    d = None
    for a, cols in zip(acts, halves):
        dd = jnp.dot(a, wd_ref[cols, :], preferred_element_type=F32)
        d = dd if d is None else d + dd
    o_ref[...] += d

    @pl.when(c == n_ff - 1)
    def _():
        o_ref[...] = _layer_norm(o_ref[...], g_ref[...], b_ref[...])


def _ffn(h, w_gate_b, w_up_b, w_down_b, w_conv, b_conv, state, ln_g, ln_b, seq, tm, fc):
    t = h.shape[0]
    n_ff = D_FF // fc
    nt = t // tm
    if seq >= tm:
        tiles_per_seq = seq // tm
        st_spec = pl.BlockSpec((1, 2, fc), lambda i, c: (i // tiles_per_seq, 0, c))
        last_spec = pl.BlockSpec((1, 2, fc), lambda i, c: (i, 0, c))
        last_shape = jax.ShapeDtypeStruct((nt, 2, D_FF), F32)
    else:
        ns = tm // seq
        st_spec = pl.BlockSpec((2, ns, fc), lambda i, c: (0, i, c))
        last_spec = pl.BlockSpec((2, ns, fc), lambda i, c: (0, i, c))
        last_shape = jax.ShapeDtypeStruct((2, t // seq, D_FF), F32)
    kern = functools.partial(_ffn_kernel, tm=tm, fc=fc, seq=seq, n_ff=n_ff)
    return pl.pallas_call(
        kern,
        grid=(nt, n_ff),
        in_specs=[pl.BlockSpec((tm, D_MODEL), lambda i, c: (i, 0), pipeline_mode=pl.Buffered(1)),
                  pl.BlockSpec((D_MODEL, fc), lambda i, c: (0, c)),
                  pl.BlockSpec((D_MODEL, fc), lambda i, c: (0, c)),
                  pl.BlockSpec((fc, D_MODEL), lambda i, c: (c, 0)),
                  pl.BlockSpec((CONV_W, fc), lambda i, c: (0, c)),
                  pl.BlockSpec((1, fc), lambda i, c: (0, c)),
                  st_spec,
                  pl.BlockSpec((1, D_MODEL), lambda i, c: (0, 0)),
                  pl.BlockSpec((1, D_MODEL), lambda i, c: (0, 0))],
        out_specs=[pl.BlockSpec((tm, D_MODEL), lambda i, c: (i, 0)), last_spec],
        out_shape=[jax.ShapeDtypeStruct((t, D_MODEL), F32), last_shape],
        scratch_shapes=[pltpu.VMEM((tm, D_MODEL), BF), pltpu.VMEM((n_ff, 2, fc), F32),
                        pltpu.VMEM((fc // 128, tm, 128), F32)],
        compiler_params=_cparams(2),
        name="conv_ffn",
    )(h, w_gate_b, w_up_b, w_down_b, w_conv, b_conv.reshape(1, D_FF), state,
      ln_g.reshape(1, D_MODEL), ln_b.reshape(1, D_MODEL))


def _after_mixers(x, od, os_, memk, memv, conv_state, n_batch, t_b, w):
    h1, qc = _proj_ln([od, os_], w['w_o'], x, w['ln1_g'], w['ln1_b'], MID_ROWS, "out_proj_ln1_q", wq_b=w['w_mq'])
    oc = _cross_attn(qc, memk, memv, n_batch, t_b, min(t_b, MID_ROWS))
    (h2,) = _proj_ln([oc], w['w_mo'], h1, w['ln2_g'], w['ln2_b'], MID_ROWS, "cross_out_ln2")
    return _ffn(h2, w['w_gate'], w['w_up'], w['w_down'], w['w_conv'], w['b_conv'], conv_state,
                w['ln3_g'], w['ln3_b'], t_b, FFN_ROWS, FFN_CHUNK)


def kernel(x_prompt, x_sample, mem_prompt, cache_diff_k, cache_diff_v, cache_sb_k, cache_sb_v, cache_mem_k, cache_mem_v, state_ffn_conv, w_in, lambda_q1, lambda_k1, lambda_q2, lambda_k2, subln_g, w_o, ln1_g, ln1_b, w_mq, w_mk, w_mv, w_mo, ln2_g, ln2_b, w_gate, w_up, w_conv, b_conv, w_down, ln3_g, ln3_b):
    assert w_in.shape[0] == DEPTH == 1
    n_p, t_p, _ = x_prompt.shape
    n_s, t_s, _ = x_sample.shape
    past = cache_diff_k.shape[2]
    n_mem = mem_prompt.shape[1]
    l = 0
    lam_init = 0.8 - 0.6 * math.exp(-0.3 * l)
    lam = (jnp.exp(jnp.sum(lambda_q1[l] * lambda_k1[l])) - jnp.exp(jnp.sum(lambda_q2[l] * lambda_k2[l]))
           + lam_init).reshape(1).astype(F32)
    w = {'w_in': w_in[l].astype(BF), 'w_conv': w_conv[l], 'b_conv': b_conv[l],
         'subln_g': subln_g[l], 'ln1_g': ln1_g[l], 'ln1_b': ln1_b[l], 'ln2_g': ln2_g[l], 'ln2_b': ln2_b[l],
         'ln3_g': ln3_g[l], 'ln3_b': ln3_b[l]}
    late_weights = {'w_o': w_o[l], 'w_mq': w_mq[l], 'w_mo': w_mo[l], 'w_gate': w_gate[l], 'w_up': w_up[l],
                    'w_down': w_down[l]}

    mem_b = mem_prompt.reshape(n_p * n_mem, D_MODEL).astype(BF)
    (mk_f,) = _matmul(mem_b, w_mk[l], [F32], n_p * n_mem, 1024, "memory_k")
    (mv_f,) = _matmul(mem_b, w_mv[l], [F32], n_p * n_mem, 1024, "memory_v")
    xp = x_prompt.reshape(n_p * t_p, D_MODEL)
    pb, pf = _inproj(xp, w['w_in'], np.arange(t_p), n_p, INPROJ_ROWS, sb_interleave=SB_BLOCK)
    odp, late_b = _diff_attn_prompt(lam, pb['dq'], pb['dk'], pb['dv'], w['subln_g'], n_p, t_p, lam_init,
                                    cast_weights=tuple(late_weights.values()))
    w.update(zip(late_weights, late_b))
    osp = _sb_attn_prompt(pb['sq'], pb['sk'], pb['sv'], n_p, t_p)
    zero_state = jnp.zeros((n_p, CONV_W - 1, D_FF), F32)
    mem_heads = (n_p, n_mem, N_MEM_HEADS, MEM_HEAD_DIM)
    yp, lastp = _after_mixers(xp, odp, osp, _tiled_lane_rows(mk_f.reshape(mem_heads), n_p, N_MEM_HEADS),
                              _tiled_lane_rows(mv_f.reshape(mem_heads), n_p, N_MEM_HEADS),
                              zero_state, n_p, t_p, w)

    xs = x_sample.reshape(n_s * t_s, D_MODEL)
    sb, sf = _inproj(xs, w['w_in'], past + np.arange(t_s), n_s, INPROJ_ROWS)
    rows = lambda c: _as_lane_rows(c[l], n_s)
    ods = _diff_attn_sample(lam, sb['dq'], sb['dk'], sb['dv'], rows(cache_diff_k),
                            _tiled_lane_rows(cache_diff_v[l], n_s, N_DIFF_HEADS), w['subln_g'],
                            n_s, t_s, past, lam_init)
    oss = _sb_attn_sample(sb['sq'], sb['sk'], sb['sv'], rows(cache_sb_k), rows(cache_sb_v), n_s, t_s, past)
    state_s = jnp.transpose(state_ffn_conv[l], (1, 0, 2))
    ys, lasts = _after_mixers(xs, ods, oss, _tiled_lane_rows(cache_mem_k[l], n_s, N_MEM_HEADS),
                              _tiled_lane_rows(cache_mem_v[l], n_s, N_MEM_HEADS), state_s, n_s, t_s, w)

    tiles_per_seq = t_p // FFN_ROWS
    new_conv_p = lastp[tiles_per_seq - 1::tiles_per_seq]
    new_conv_s = jnp.transpose(lasts, (1, 0, 2))

    def new_rows(f, n, t):
        dv = jnp.transpose(f['dv'].reshape(n, t, DIFF_V_DIM // HEAD_DIM, N_DIFF_HEADS, HEAD_DIM), (0, 1, 3, 2, 4))
        return (f['dk'].reshape(1, n, t, N_DIFF_HEADS, 2, HEAD_DIM), dv.reshape(1, n, t, N_DIFF_HEADS, DIFF_V_DIM),
                f['sk'].reshape(1, n, t, N_SB_HEADS, HEAD_DIM), f['sv'].reshape(1, n, t, N_SB_HEADS, HEAD_DIM))

    return ((yp.reshape(n_p, t_p, D_MODEL), ys.reshape(n_s, t_s, D_MODEL))
            + new_rows(pf, n_p, t_p)
            + (mk_f.reshape((1,) + mem_heads), mv_f.reshape((1,) + mem_heads),
               new_conv_p[None])
            + new_rows(sf, n_s, t_s)
            + (new_conv_s[None],))
```
